```python
import math
import jax, jax.numpy as jnp
from jax import lax
import numpy as np

D_MODEL = 1024
BATCH = 2
SEQ = 16384
DEPTH = 2
DEC_BATCH = 16
DEC_SEQ = 4096
PAST_LEN = 128

HEAD_DIM = 64
N_HEADS_A = 8
N_KV_HEADS_A = 2
N_HEADS_B = 8
QA_W = N_HEADS_A * HEAD_DIM
KVA_W = N_KV_HEADS_A * HEAD_DIM
QKVB_W = N_HEADS_B * HEAD_DIM
MIX_WIDTH = QA_W + QKVB_W
PROJ_W = QA_W + 2 * KVA_W + 3 * QKVB_W
WINDOW = 128
BLOCK = 128
GRID_W = 64
NA_ROWS = 8
NA_COLS = 16
N_EXPERTS = 16
EC_CAPACITY = 2
D_FF = 2048
EPS = 1e-6
NEG = -1e30

kernel_name = "hymba_swa_natten_ec_encoder"


def rms_norm(x, g):
    xf = x.astype(jnp.float32)
    y = xf * lax.rsqrt(jnp.mean(xf * xf, axis=-1, keepdims=True) + EPS)
    return (y * g.astype(jnp.float32)).astype(x.dtype)


def alibi_slopes(n_heads):
    return jnp.exp2(-8.0 * (jnp.arange(n_heads, dtype=jnp.float32) + 1.0) / n_heads)


def window_attention(q, k, v, sink):
    B, S, H, Dh = q.shape
    KVH = k.shape[2]
    G = H // KVH
    nb = S // BLOCK
    scale = 1.0 / math.sqrt(Dh)
    qb = q.reshape(B, nb, BLOCK, KVH, G, Dh)
    pad = ((0, 0), (BLOCK, BLOCK), (0, 0), (0, 0))
    kp = jnp.pad(k, pad).reshape(B, nb + 2, BLOCK, KVH, Dh)
    vp = jnp.pad(v, pad).reshape(B, nb + 2, BLOCK, KVH, Dh)
    kb = jnp.concatenate([kp[:, :-2], kp[:, 1:-1], kp[:, 2:]], axis=2)
    vb = jnp.concatenate([vp[:, :-2], vp[:, 1:-1], vp[:, 2:]], axis=2)
    s = jnp.einsum('bnqhgd,bnshd->bnhgqs', qb, kb,
                   preferred_element_type=jnp.float32) * scale
    i = jnp.arange(BLOCK)[:, None]
    j = jnp.arange(3 * BLOCK)[None, :]
    dist = jnp.abs(i + BLOCK - j).astype(jnp.float32)
    key_pos = (jnp.arange(nb)[:, None] - 1) * BLOCK + jnp.arange(3 * BLOCK)[None, :]
    valid = (key_pos >= 0) & (key_pos < S)
    mask = valid[:, None, None, None, :] & (dist <= WINDOW)[None, None, None, :, :]
    slopes = alibi_slopes(H).reshape(KVH, G, 1, 1)
    s = jnp.where(mask, s - slopes * dist, NEG)
    sk = sink.astype(jnp.float32).reshape(KVH, G, 1)
    m = jnp.maximum(jnp.max(s, axis=-1), sk)
    p = jnp.exp(s - m[..., None])
    denom = jnp.sum(p, axis=-1) + jnp.exp(sk - m)
    o = jnp.einsum('bnhgqs,bnshd->bnqhgd', p.astype(v.dtype), vb,
                   preferred_element_type=jnp.float32)
    o = o / jnp.transpose(denom, (0, 1, 4, 2, 3))[..., None]
    return o.reshape(B, S, H * Dh).astype(q.dtype)


def neighbourhood_attention(q, k, v, rpb):
    B, S, H, Dh = q.shape
    rows = S // GRID_W
    kr = min(NA_ROWS, rows)
    scale = 1.0 / math.sqrt(Dh)
    r = jnp.arange(rows)
    r0 = jnp.clip(r - NA_ROWS // 2, 0, rows - kr)
    ridx = r0[:, None] + jnp.arange(kr)[None, :]
    c = jnp.arange(GRID_W)
    c0 = jnp.clip(c - NA_COLS // 2, 0, GRID_W - NA_COLS)
    col_ok = (c[None, :] >= c0[:, None]) & (c[None, :] < c0[:, None] + NA_COLS)
    qg = q.reshape(B, rows, GRID_W, H, Dh)
    kg = jnp.take(k.reshape(B, rows, GRID_W, H, Dh), ridx, axis=1)
    vg = jnp.take(v.reshape(B, rows, GRID_W, H, Dh), ridx, axis=1)
    s = jnp.einsum('brchd,brkwhd->brhckw', qg, kg,
                   preferred_element_type=jnp.float32) * scale
    dr = ridx - r[:, None] + (NA_ROWS - 1)
    dc = jnp.clip(c[None, :] - c[:, None] + (NA_COLS - 1), 0, 2 * NA_COLS - 2)
    bias = rpb.astype(jnp.float32)[:, dr[:, None, :, None], dc[None, :, None, :]]
    bias = jnp.transpose(bias, (1, 0, 2, 3, 4))
    s = jnp.where(col_ok[:, None, :], s + bias, NEG)
    p = jax.nn.softmax(s.reshape(B, rows, H, GRID_W, kr * GRID_W), axis=-1)
    p = p.reshape(B, rows, H, GRID_W, kr, GRID_W)
    o = jnp.einsum('brhckw,brkwhd->brchd', p.astype(v.dtype), vg,
                   preferred_element_type=jnp.float32)
    return o.reshape(B, S, H * Dh).astype(q.dtype)


def mixer(h, w_in, qnorm_a, knorm_a, sink_a, qnorm_b, knorm_b, rpb_b, onorm_a, onorm_b, w_out):
    B, S, _ = h.shape
    proj = jnp.einsum('bsd,de->bse', h, w_in)
    splits = [QA_W, QA_W + KVA_W, QA_W + 2 * KVA_W,
              QA_W + 2 * KVA_W + QKVB_W, QA_W + 2 * KVA_W + 2 * QKVB_W]
    qa, ka, va, qb, kb, vb = jnp.split(proj, splits, axis=-1)
    qa = rms_norm(qa.reshape(B, S, N_HEADS_A, HEAD_DIM), qnorm_a)
    ka = rms_norm(ka.reshape(B, S, N_KV_HEADS_A, HEAD_DIM), knorm_a)
    va = va.reshape(B, S, N_KV_HEADS_A, HEAD_DIM)
    qb = rms_norm(qb.reshape(B, S, N_HEADS_B, HEAD_DIM), qnorm_b)
    kb = rms_norm(kb.reshape(B, S, N_HEADS_B, HEAD_DIM), knorm_b)
    vb = vb.reshape(B, S, N_HEADS_B, HEAD_DIM)
    out_a = rms_norm(window_attention(qa, ka, va, sink_a), onorm_a)
    out_b = rms_norm(neighbourhood_attention(qb, kb, vb, rpb_b), onorm_b)
    return jnp.einsum('bse,ed->bsd', jnp.concatenate([out_a, out_b], axis=-1), w_out)


def expert_choice_ffn(h, w_router, w_gate, w_up, w_down):
    B, S, D = h.shape
    n_tok = B * S
    cap = EC_CAPACITY * n_tok // N_EXPERTS
    xf = h.reshape(n_tok, D)
    logits = jnp.einsum('nd,de->ne', xf, w_router, preferred_element_type=jnp.float32)
    aff = jax.nn.softmax(logits, axis=-1)
    gate, idx = lax.top_k(aff.T, cap)
    xe = jnp.take(xf, idx, axis=0)
    g = jnp.einsum('ecd,edf->ecf', xe, w_gate)
    u = jnp.einsum('ecd,edf->ecf', xe, w_up)
    ye = jnp.einsum('ecf,efd->ecd', jax.nn.silu(g) * u, w_down)
    ye = ye * gate[..., None].astype(ye.dtype)
    out = jnp.zeros_like(xf).at[idx.reshape(-1)].add(ye.reshape(-1, D))
    return out.reshape(B, S, D)


def trunk(x, norm_mix, w_in, qnorm_a, knorm_a, sink_a, qnorm_b, knorm_b, rpb_b,
          onorm_a, onorm_b, w_out, norm_ffn, w_router, w_gate, w_up, w_down):
    for l in range(DEPTH):
        h = rms_norm(x, norm_mix[l])
        x = x + mixer(h, w_in[l], qnorm_a[l], knorm_a[l], sink_a[l], qnorm_b[l], knorm_b[l],
                      rpb_b[l], onorm_a[l], onorm_b[l], w_out[l])
        h = rms_norm(x, norm_ffn[l])
        x = x + expert_choice_ffn(h, w_router[l], w_gate[l], w_up[l], w_down[l])
    return x


def setup_inputs(seed: int = 0) -> dict:
    key = jax.random.key(seed)
    ks = jax.random.split(key, 20)
    f32 = jnp.float32

    def nrm(k, shape, scale):
        return jax.random.normal(k, shape, f32) * scale

    def gain(k, shape):
        return 1.0 + 0.02 * jax.random.normal(k, shape, f32)

    return {
        "x_prompt": nrm(ks[0], (BATCH, SEQ, D_MODEL), 1.0),
        "x_sample": nrm(ks[1], (DEC_BATCH, DEC_SEQ, D_MODEL), 1.0),
        "norm_mix": gain(ks[2], (DEPTH, D_MODEL)),
        "w_in": nrm(ks[3], (DEPTH, D_MODEL, PROJ_W), D_MODEL ** -0.5),
        "qnorm_a": gain(ks[4], (DEPTH, HEAD_DIM)),
        "knorm_a": gain(ks[5], (DEPTH, HEAD_DIM)),
        "sink_a": nrm(ks[6], (DEPTH, N_HEADS_A), 0.5),
        "qnorm_b": gain(ks[7], (DEPTH, HEAD_DIM)),
        "knorm_b": gain(ks[8], (DEPTH, HEAD_DIM)),
        "rpb_b": nrm(ks[9], (DEPTH, N_HEADS_B, 2 * NA_ROWS - 1, 2 * NA_COLS - 1), 0.1),
        "onorm_a": gain(ks[10], (DEPTH, QA_W)),
        "onorm_b": gain(ks[11], (DEPTH, QKVB_W)),
        "w_out": nrm(ks[12], (DEPTH, MIX_WIDTH, D_MODEL), MIX_WIDTH ** -0.5),
        "norm_ffn": gain(ks[13], (DEPTH, D_MODEL)),
        "w_router": nrm(ks[14], (DEPTH, D_MODEL, N_EXPERTS), D_MODEL ** -0.5),
        "w_gate": nrm(ks[15], (DEPTH, N_EXPERTS, D_MODEL, D_FF), D_MODEL ** -0.5),
        "w_up": nrm(ks[16], (DEPTH, N_EXPERTS, D_MODEL, D_FF), D_MODEL ** -0.5),
        "w_down": nrm(ks[17], (DEPTH, N_EXPERTS, D_FF, D_MODEL), D_FF ** -0.5),
    }


def reference(x_prompt, x_sample, norm_mix, w_in, qnorm_a, knorm_a, sink_a, qnorm_b, knorm_b,
              rpb_b, onorm_a, onorm_b, w_out, norm_ffn, w_router, w_gate, w_up, w_down):
    y_prompt = trunk(x_prompt, norm_mix, w_in, qnorm_a, knorm_a, sink_a, qnorm_b, knorm_b, rpb_b,
                     onorm_a, onorm_b, w_out, norm_ffn, w_router, w_gate, w_up, w_down)
    y_sample = trunk(x_sample, norm_mix, w_in, qnorm_a, knorm_a, sink_a, qnorm_b, knorm_b, rpb_b,
                     onorm_a, onorm_b, w_out, norm_ffn, w_router, w_gate, w_up, w_down)
    return (y_prompt, y_sample)
```

```python
import functools
import math

import jax
import jax.numpy as jnp
from jax import lax
from jax.experimental import pallas as pl
from jax.experimental.pallas import tpu as pltpu

HEAD_DIM = 64
N_HEADS_A = 8
N_KV_HEADS_A = 2
N_HEADS_B = 8
QA_W = N_HEADS_A * HEAD_DIM
KVA_W = N_KV_HEADS_A * HEAD_DIM
QKVB_W = N_HEADS_B * HEAD_DIM
PROJ_W = QA_W + 2 * KVA_W + 3 * QKVB_W
WINDOW = 128
GRID_W = 64
NA_ROWS = 8
NA_COLS = 16
N_EXPERTS = 16
EC_CAPACITY = 2
EPS = 1e-6
NEG = -1e30

LANE = 128
V7X_VMEM_BYTES = 64 * 1024 * 1024

PROJ_TM = 256
WIN_TQ = 512
WIN_SUB = WINDOW
NA_GROUP_ROWS = 4
NA_TQ = NA_GROUP_ROWS * GRID_W
NA_KEY_ROWS = 3 * NA_GROUP_ROWS
POST_TM = 256
FFN_TM = 512
FFN_TF = 512

_NT = (((1,), (1,)), ((), ()))


def _vmem_limit(nbytes):
    return int(min(nbytes, V7X_VMEM_BYTES - 4 * 1024 * 1024))


def _proj_kernel(x_ref, g_ref, w_ref, hg_ref,
                 qa_ref, ka_ref, va_ref, qb_ref, kb_ref, vb_ref):
    x = x_ref[...]
    ms = jnp.mean(x * x, axis=-1, keepdims=True)
    h = (x * lax.rsqrt(ms + EPS) * g_ref[...]).astype(jnp.bfloat16)

    def seg(lo, hi):
        return lax.dot_general(w_ref[lo:hi, :], h, _NT,
                               preferred_element_type=jnp.float32)

    def head_norm(blk, gain):
        ssq = jnp.sum(blk * blk, axis=0, keepdims=True)
        return blk * lax.rsqrt(ssq * (1.0 / HEAD_DIM) + EPS) * gain

    g_qa = hg_ref[0 * HEAD_DIM:1 * HEAD_DIM, :]
    g_ka = hg_ref[1 * HEAD_DIM:2 * HEAD_DIM, :]
    g_qb = hg_ref[2 * HEAD_DIM:3 * HEAD_DIM, :]
    g_kb = hg_ref[3 * HEAD_DIM:4 * HEAD_DIM, :]

    o = 0
    p = seg(o, o + QA_W)
    for hd in range(N_HEADS_A):
        r = slice(hd * HEAD_DIM, (hd + 1) * HEAD_DIM)
        qa_ref[r, :] = head_norm(p[r, :], g_qa).astype(qa_ref.dtype)
    o += QA_W
    p = seg(o, o + 2 * KVA_W)
    kn = jnp.concatenate(
        [head_norm(p[hd * HEAD_DIM:(hd + 1) * HEAD_DIM, :], g_ka)
         for hd in range(N_KV_HEADS_A)], axis=0)
    ka_ref[...] = kn.T.astype(ka_ref.dtype)
    va_ref[...] = p[KVA_W:2 * KVA_W, :].astype(va_ref.dtype)
    o += 2 * KVA_W
    p = seg(o, o + QKVB_W)
    for hd in range(N_HEADS_B):
        r = slice(hd * HEAD_DIM, (hd + 1) * HEAD_DIM)
        qb_ref[r, :] = head_norm(p[r, :], g_qb).astype(qb_ref.dtype)
    o += QKVB_W
    p = seg(o, o + QKVB_W)
    kn = jnp.concatenate(
        [head_norm(p[hd * HEAD_DIM:(hd + 1) * HEAD_DIM, :], g_kb)
         for hd in range(N_HEADS_B)], axis=0)
    kb_ref[...] = kn.T.astype(kb_ref.dtype)
    o += QKVB_W
    vb_ref[...] = seg(o, o + QKVB_W).astype(vb_ref.dtype)


def _in_proj(x2d, g_mix, w_in_t, head_gains):
    n, d = x2d.shape
    tm = PROJ_TM
    bf = jnp.bfloat16
    col = lambda i: (0, i)
    row = lambda i: (i, 0)
    const = lambda i: (0, 0)
    out_shape = (
        jax.ShapeDtypeStruct((QA_W, n), bf),
        jax.ShapeDtypeStruct((n, KVA_W), bf),
        jax.ShapeDtypeStruct((KVA_W, n), bf),
        jax.ShapeDtypeStruct((QKVB_W, n), bf),
        jax.ShapeDtypeStruct((n, QKVB_W), bf),
        jax.ShapeDtypeStruct((QKVB_W, n), bf),
    )
    out_specs = (
        pl.BlockSpec((QA_W, tm), col),
        pl.BlockSpec((tm, KVA_W), row),
        pl.BlockSpec((KVA_W, tm), col),
        pl.BlockSpec((QKVB_W, tm), col),
        pl.BlockSpec((tm, QKVB_W), row),
        pl.BlockSpec((QKVB_W, tm), col),
    )
    return pl.pallas_call(
        _proj_kernel,
        grid=(n // tm,),
        in_specs=[
            pl.BlockSpec((tm, d), row),
            pl.BlockSpec((1, d), const),
            pl.BlockSpec((PROJ_W, d), const),
            pl.BlockSpec((4 * HEAD_DIM, tm), const),
        ],
        out_specs=out_specs,
        out_shape=out_shape,
        compiler_params=pltpu.CompilerParams(
            dimension_semantics=("arbitrary",),
            vmem_limit_bytes=_vmem_limit(48 * 1024 * 1024)),
        name="in_proj",
    )(x2d, g_mix, w_in_t, head_gains)


def _window_kernel(blocks_per_seq, q_ref, kp_ref, kc_ref, kn_ref,
                   vp_ref, vc_ref, vn_ref, bias_ref, sink_ref, o_ref):
    i = pl.program_id(0)
    pos = i % blocks_per_seq
    pen_prev = jnp.where(pos == 0, NEG, 0.0).astype(jnp.float32)
    pen_next = jnp.where(pos == blocks_per_seq - 1, NEG, 0.0).astype(jnp.float32)

    kcat = jnp.concatenate([kp_ref[...], kc_ref[...], kn_ref[...]], axis=0)
    vcat = jnp.concatenate([vp_ref[...], vc_ref[...], vn_ref[...]], axis=1)
    sink = sink_ref[...]
    n_sub = WIN_TQ // WIN_SUB
    gq = N_HEADS_A // N_KV_HEADS_A
    zero = jnp.zeros((HEAD_DIM, WIN_SUB), jnp.bfloat16)
    for j in range(n_sub):
        cols = slice(j * WIN_SUB, (j + 1) * WIN_SUB)
        kwin = kcat[j * WIN_SUB:(j + 3) * WIN_SUB, :]
        vwin = vcat[:, j * WIN_SUB:(j + 3) * WIN_SUB]
        halves = []
        for kv in range(N_KV_HEADS_A):
            parts = []
            for hd in range(N_HEADS_A):
                if hd // gq == kv:
                    parts.append(q_ref[hd * HEAD_DIM:(hd + 1) * HEAD_DIM, cols])
                else:
                    parts.append(zero)
            halves.append(jnp.concatenate(parts, axis=1))
        qblk = jnp.concatenate(halves, axis=0)
        s = jnp.dot(kwin, qblk, preferred_element_type=jnp.float32)
        s = s + bias_ref[...]
        if j == 0:
            s = jnp.concatenate([s[:WIN_SUB] + pen_prev, s[WIN_SUB:]], axis=0)
        if j == n_sub - 1:
            s = jnp.concatenate([s[:2 * WIN_SUB], s[2 * WIN_SUB:] + pen_next], axis=0)
        m = jnp.maximum(jnp.max(s, axis=0, keepdims=True), sink)
        p = jnp.exp(s - m)
        denom = jnp.sum(p, axis=0, keepdims=True) + jnp.exp(sink - m)
        pb = p.astype(jnp.bfloat16)
        outs = []
        for kv in range(N_KV_HEADS_A):
            lanes = slice(kv * gq * WIN_SUB, (kv + 1) * gq * WIN_SUB)
            o_t = jnp.dot(vwin[kv * HEAD_DIM:(kv + 1) * HEAD_DIM, :], pb[:, lanes],
                          preferred_element_type=jnp.float32)
            o_t = o_t / denom[:, lanes]
            for g in range(gq):
                outs.append(o_t[:, g * WIN_SUB:(g + 1) * WIN_SUB])
        for a in range(N_HEADS_A // 2):
            pair = jnp.concatenate([outs[2 * a], outs[2 * a + 1]], axis=0)
            o_ref[cols, a * LANE:(a + 1) * LANE] = pair.T.astype(o_ref.dtype)


def _window_attention(qa_t, ka, va_t, bias_t, sink_row, seq_len):
    n = ka.shape[0]
    nblk = n // WIN_TQ
    bps = seq_len // WIN_TQ
    r = WIN_TQ // WIN_SUB
    nsub = n // WIN_SUB
    prev_i = lambda i: jnp.maximum(r * i - 1, 0)
    next_i = lambda i: jnp.minimum(r * i + r, nsub - 1)
    const = lambda i: (0, 0)
    return pl.pallas_call(
        functools.partial(_window_kernel, bps),
        grid=(nblk,),
        in_specs=[
            pl.BlockSpec((QA_W, WIN_TQ), lambda i: (0, i)),
            pl.BlockSpec((WIN_SUB, KVA_W), lambda i: (prev_i(i), 0)),
            pl.BlockSpec((WIN_TQ, KVA_W), lambda i: (i, 0)),
            pl.BlockSpec((WIN_SUB, KVA_W), lambda i: (next_i(i), 0)),
            pl.BlockSpec((KVA_W, WIN_SUB), lambda i: (0, prev_i(i))),
            pl.BlockSpec((KVA_W, WIN_TQ), lambda i: (0, i)),
            pl.BlockSpec((KVA_W, WIN_SUB), lambda i: (0, next_i(i))),
            pl.BlockSpec((3 * WIN_SUB, N_HEADS_A * WIN_SUB), const),
            pl.BlockSpec((1, N_HEADS_A * WIN_SUB), const),
        ],
        out_specs=pl.BlockSpec((WIN_TQ, QA_W), lambda i: (i, 0)),
        out_shape=jax.ShapeDtypeStruct((n, QA_W), jnp.bfloat16),
        compiler_params=pltpu.CompilerParams(
            dimension_semantics=("arbitrary",),
            vmem_limit_bytes=_vmem_limit(40 * 1024 * 1024)),
        name="window_attn",
    )(qa_t, ka, ka, ka, va_t, va_t, va_t, bias_t, sink_row)


def _na_kernel(q_ref, kp_ref, kc_ref, kn_ref, vp_ref, vc_ref, vn_ref,
               bias_ref, o_ref):
    zero = jnp.zeros((HEAD_DIM, NA_TQ), jnp.bfloat16)
    for pr in range(N_HEADS_B // 2):
        lanes = slice(pr * LANE, (pr + 1) * LANE)
        kwin = jnp.concatenate(
            [kp_ref[:, lanes], kc_ref[:, lanes], kn_ref[:, lanes]], axis=0)
        vwin = jnp.concatenate(
            [vp_ref[lanes, :], vc_ref[lanes, :], vn_ref[lanes, :]], axis=1)
        q0 = q_ref[(2 * pr) * HEAD_DIM:(2 * pr + 1) * HEAD_DIM, :]
        q1 = q_ref[(2 * pr + 1) * HEAD_DIM:(2 * pr + 2) * HEAD_DIM, :]
        qblk = jnp.concatenate(
            [jnp.concatenate([q0, zero], axis=1),
             jnp.concatenate([zero, q1], axis=1)], axis=0)
        s = jnp.dot(kwin, qblk, preferred_element_type=jnp.float32)
        outs = []
        for t in range(2):
            st = s[:, t * NA_TQ:(t + 1) * NA_TQ] + bias_ref[0, 2 * pr + t]
            m = jnp.max(st, axis=0, keepdims=True)
            p = jnp.exp(st - m)
            denom = jnp.sum(p, axis=0, keepdims=True)
            o_t = jnp.dot(vwin[t * HEAD_DIM:(t + 1) * HEAD_DIM, :],
                          p.astype(jnp.bfloat16),
                          preferred_element_type=jnp.float32)
            outs.append(o_t / denom)
        pair = jnp.concatenate(outs, axis=0)
        o_ref[:, lanes] = pair.T.astype(o_ref.dtype)


def _na_attention(qb_t, kb, vb_t, bias, seq_len):
    n = kb.shape[0]
    ng = n // NA_TQ
    gps = seq_len // NA_TQ
    prev_i = lambda g: jnp.maximum(g - 1, 0)
    next_i = lambda g: jnp.minimum(g + 1, ng - 1)

    def variant(g):
        pos = g % gps
        return jnp.where(pos == 0, 0, jnp.where(pos == gps - 1, 2, 1))

    return pl.pallas_call(
        _na_kernel,
        grid=(ng,),
        in_specs=[
            pl.BlockSpec((QKVB_W, NA_TQ), lambda g: (0, g)),
            pl.BlockSpec((NA_TQ, QKVB_W), lambda g: (prev_i(g), 0)),
            pl.BlockSpec((NA_TQ, QKVB_W), lambda g: (g, 0)),
            pl.BlockSpec((NA_TQ, QKVB_W), lambda g: (next_i(g), 0)),
            pl.BlockSpec((QKVB_W, NA_TQ), lambda g: (0, prev_i(g))),
            pl.BlockSpec((QKVB_W, NA_TQ), lambda g: (0, g)),
            pl.BlockSpec((QKVB_W, NA_TQ), lambda g: (0, next_i(g))),
            pl.BlockSpec((1, N_HEADS_B, NA_KEY_ROWS * GRID_W, NA_TQ),
                         lambda g: (variant(g), 0, 0, 0)),
        ],
        out_specs=pl.BlockSpec((NA_TQ, QKVB_W), lambda g: (g, 0)),
        out_shape=jax.ShapeDtypeStruct((n, QKVB_W), jnp.bfloat16),
        compiler_params=pltpu.CompilerParams(
            dimension_semantics=("arbitrary",),
            vmem_limit_bytes=_vmem_limit(48 * 1024 * 1024)),
        name="na_attn",
    )(qb_t, kb, kb, kb, vb_t, vb_t, vb_t, bias)


def _post_kernel(a_ref, b_ref, x_ref, ga_ref, gb_ref, w_ref, gf_ref,
                 wrh_ref, wrl_ref, x1_ref, h_ref, aff_ref):
    def rms(v, g):
        ms = jnp.mean(v * v, axis=-1, keepdims=True)
        return v * lax.rsqrt(ms + EPS) * g

    an = rms(a_ref[...].astype(jnp.float32), ga_ref[...]).astype(jnp.bfloat16)
    bn = rms(b_ref[...].astype(jnp.float32), gb_ref[...]).astype(jnp.bfloat16)
    y = jnp.dot(an, w_ref[:QA_W, :], preferred_element_type=jnp.float32)
    y = y + jnp.dot(bn, w_ref[QA_W:, :], preferred_element_type=jnp.float32)
    x1 = x_ref[...] + y
    x1_ref[...] = x1
    h = rms(x1, gf_ref[...])
    h_hi = h.astype(jnp.bfloat16)
    h_lo = (h - h_hi.astype(jnp.float32)).astype(jnp.bfloat16)
    h_ref[...] = h_hi
    wrh = wrh_ref[...]
    logits = lax.dot_general(wrh, h_hi, _NT, preferred_element_type=jnp.float32)
    logits = logits + lax.dot_general(wrh, h_lo, _NT, preferred_element_type=jnp.float32)
    logits = logits + lax.dot_general(wrl_ref[...], h_hi, _NT,
                                      preferred_element_type=jnp.float32)
    m = jnp.max(logits, axis=0, keepdims=True)
    e = jnp.exp(logits - m)
    aff_ref[...] = e / jnp.sum(e, axis=0, keepdims=True)


def _post_attn(out_a, out_b, x2d, g_a, g_b, w_out, g_ffn, wr_hi, wr_lo):
    n, d = x2d.shape
    tm = POST_TM
    row = lambda i: (i, 0)
    const = lambda i: (0, 0)
    return pl.pallas_call(
        _post_kernel,
        grid=(n // tm,),
        in_specs=[
            pl.BlockSpec((tm, QA_W), row),
            pl.BlockSpec((tm, QKVB_W), row),
            pl.BlockSpec((tm, d), row),
            pl.BlockSpec((1, QA_W), const),
            pl.BlockSpec((1, QKVB_W), const),
            pl.BlockSpec((QA_W + QKVB_W, d), const),
            pl.BlockSpec((1, d), const),
            pl.BlockSpec((N_EXPERTS, d), const),
            pl.BlockSpec((N_EXPERTS, d), const),
        ],
        out_specs=(
            pl.BlockSpec((tm, d), row),
            pl.BlockSpec((tm, d), row),
            pl.BlockSpec((N_EXPERTS, tm), lambda i: (0, i)),
        ),
        out_shape=(
            jax.ShapeDtypeStruct((n, d), jnp.float32),
            jax.ShapeDtypeStruct((n, d), jnp.bfloat16),
            jax.ShapeDtypeStruct((N_EXPERTS, n), jnp.float32),
        ),
        compiler_params=pltpu.CompilerParams(
            dimension_semantics=("arbitrary",),
            vmem_limit_bytes=_vmem_limit(40 * 1024 * 1024)),
        name="post_attn",
    )(out_a, out_b, x2d, g_a, g_b, w_out, g_ffn, wr_hi, wr_lo)


def _ffn_kernel(x_ref, gate_ref, wg_ref, wu_ref, wd_ref, o_ref):
    x = x_ref[0]
    d_ff = wg_ref.shape[2]
    acc = None
    for c in range(d_ff // FFN_TF):
        f = slice(c * FFN_TF, (c + 1) * FFN_TF)
        g = jnp.dot(x, wg_ref[0, :, f], preferred_element_type=jnp.float32)
        u = jnp.dot(x, wu_ref[0, :, f], preferred_element_type=jnp.float32)
        act = (g * jax.nn.sigmoid(g) * u).astype(jnp.bfloat16)
        part = jnp.dot(act, wd_ref[0, f, :], preferred_element_type=jnp.float32)
        acc = part if acc is None else acc + part
    o_ref[0] = (acc * gate_ref[0]).astype(o_ref.dtype)


def _expert_ffn(xe, gate, w_gate, w_up, w_down):
    e, c, d = xe.shape
    d_ff = w_gate.shape[2]
    tm = min(FFN_TM, c)
    return pl.pallas_call(
        _ffn_kernel,
        grid=(e, c // tm),
        in_specs=[
            pl.BlockSpec((1, tm, d), lambda i, j: (i, j, 0)),
            pl.BlockSpec((1, tm, 1), lambda i, j: (i, j, 0)),
            pl.BlockSpec((1, d, d_ff), lambda i, j: (i, 0, 0)),
            pl.BlockSpec((1, d, d_ff), lambda i, j: (i, 0, 0)),
            pl.BlockSpec((1, d_ff, d), lambda i, j: (i, 0, 0)),
        ],
        out_specs=pl.BlockSpec((1, tm, d), lambda i, j: (i, j, 0)),
        out_shape=jax.ShapeDtypeStruct((e, c, d), jnp.bfloat16),
        compiler_params=pltpu.CompilerParams(
            dimension_semantics=("arbitrary", "arbitrary"),
            vmem_limit_bytes=_vmem_limit(56 * 1024 * 1024)),
        name="expert_ffn",
    )(xe, gate, w_gate, w_up, w_down)


def _window_bias_t():
    j = jnp.arange(3 * WIN_SUB)[:, None]
    i = jnp.arange(WIN_SUB)[None, :]
    dist = jnp.abs(i + WIN_SUB - j).astype(jnp.float32)
    slopes = jnp.exp2(-8.0 * (jnp.arange(N_HEADS_A, dtype=jnp.float32) + 1.0) / N_HEADS_A)
    b = jnp.where(dist[None] <= WINDOW, -(slopes[:, None, None] * dist[None]), NEG)
    return jnp.transpose(b, (1, 0, 2)).reshape(3 * WIN_SUB, N_HEADS_A * WIN_SUB)


def _na_bias_t(rpb):
    kk = jnp.arange(NA_KEY_ROWS)[:, None]
    rho = jnp.arange(NA_GROUP_ROWS)[None, :]
    rel = kk - NA_GROUP_ROWS
    r0 = jnp.stack([
        jnp.zeros_like(rho),
        rho - NA_ROWS // 2,
        jnp.full_like(rho, NA_GROUP_ROWS - NA_ROWS),
    ])
    row_ok = (rel[None] >= r0) & (rel[None] < r0 + NA_ROWS)
    dr = jnp.clip(rel - rho + (NA_ROWS - 1), 0, 2 * NA_ROWS - 2)
    ck = jnp.arange(GRID_W)[:, None]
    cq = jnp.arange(GRID_W)[None, :]
    c0 = jnp.clip(cq - NA_COLS // 2, 0, GRID_W - NA_COLS)
    col_ok = (ck >= c0) & (ck < c0 + NA_COLS)
    dc = jnp.clip(ck - cq + (NA_COLS - 1), 0, 2 * NA_COLS - 2)
    vals = rpb.astype(jnp.float32)[:, dr[:, None, :, None], dc[None, :, None, :]]
    ok = row_ok[:, :, None, :, None] & col_ok[None, None, :, None, :]
    b = jnp.where(ok[:, None], vals[None], NEG)
    h = rpb.shape[0]
    return b.reshape(3, h, NA_KEY_ROWS * GRID_W, NA_TQ)


def _trunk(x, p):
    b, s, d = x.shape
    n = b * s
    assert s % WIN_TQ == 0 and s // NA_TQ >= 3 and n % PROJ_TM == 0
    cap = EC_CAPACITY * n // N_EXPERTS
    assert cap % min(FFN_TM, cap) == 0
    x2 = x.reshape(n, d)
    win_bias = _window_bias_t()
    for l in range(p["w_in"].shape[0]):
        scale = 1.0 / math.sqrt(HEAD_DIM)
        gains = jnp.concatenate([p["qnorm_a"][l] * scale, p["knorm_a"][l],
                                 p["qnorm_b"][l] * scale, p["knorm_b"][l]])
        head_gains = jnp.broadcast_to(gains[:, None], (4 * HEAD_DIM, PROJ_TM))
        w_in_t = p["w_in"][l].T.astype(jnp.bfloat16)
        qa_t, ka, va_t, qb_t, kb, vb_t = _in_proj(
            x2, p["norm_mix"][l][None, :], w_in_t, head_gains)
        sink_row = jnp.repeat(p["sink_a"][l].astype(jnp.float32), WIN_SUB)[None, :]
        out_a = _window_attention(qa_t, ka, va_t, win_bias, sink_row, s)
        out_b = _na_attention(qb_t, kb, vb_t, _na_bias_t(p["rpb_b"][l]), s)
        wr = p["w_router"][l].T
        wr_hi = wr.astype(jnp.bfloat16)
        wr_lo = (wr - wr_hi.astype(jnp.float32)).astype(jnp.bfloat16)
        x1, h, aff_t = _post_attn(
            out_a, out_b, x2, p["onorm_a"][l][None, :], p["onorm_b"][l][None, :],
            p["w_out"][l].astype(jnp.bfloat16), p["norm_ffn"][l][None, :], wr_hi, wr_lo)
        gate, idx = lax.top_k(aff_t, cap)
        xe = jnp.take(h, idx, axis=0)
        ye = _expert_ffn(xe, gate[..., None],
                         p["w_gate"][l].astype(jnp.bfloat16),
                         p["w_up"][l].astype(jnp.bfloat16),
                         p["w_down"][l].astype(jnp.bfloat16))
        x2 = x1.at[idx.reshape(-1)].add(ye.reshape(-1, d).astype(jnp.float32))
    return x2.reshape(b, s, d)


def kernel(x_prompt, x_sample, norm_mix, w_in, qnorm_a, knorm_a, sink_a, qnorm_b, knorm_b,
           rpb_b, onorm_a, onorm_b, w_out, norm_ffn, w_router, w_gate, w_up, w_down):
    p = dict(norm_mix=norm_mix, w_in=w_in, qnorm_a=qnorm_a, knorm_a=knorm_a, sink_a=sink_a,
             qnorm_b=qnorm_b, knorm_b=knorm_b, rpb_b=rpb_b, onorm_a=onorm_a, onorm_b=onorm_b,
             w_out=w_out, norm_ffn=norm_ffn, w_router=w_router, w_gate=w_gate, w_up=w_up,
             w_down=w_down)
    return (_trunk(x_prompt, p), _trunk(x_sample, p))
```

```python
import functools
import math

import jax
import jax.numpy as jnp
from jax import lax
from jax.experimental import pallas as pl
from jax.experimental.pallas import tpu as pltpu

HEAD_DIM = 64
N_HEADS_A = 8
N_KV_HEADS_A = 2
N_HEADS_B = 8
QA_W = N_HEADS_A * HEAD_DIM
KVA_W = N_KV_HEADS_A * HEAD_DIM
QKVB_W = N_HEADS_B * HEAD_DIM
PROJ_W = QA_W + 2 * KVA_W + 3 * QKVB_W
WINDOW = 128
GRID_W = 64
NA_ROWS = 8
NA_COLS = 16
N_EXPERTS = 16
EC_CAPACITY = 2
EPS = 1e-6
NEG = -1e30

LANE = 128
V7X_VMEM_BYTES = 64 * 1024 * 1024

PROJ_TM = 256
WIN_TQ = 512
WIN_SUB = WINDOW
NA_GROUP_ROWS = 4
NA_TQ = NA_GROUP_ROWS * GRID_W
NA_KEY_ROWS = 3 * NA_GROUP_ROWS
POST_TM = 256
FFN_TM = 512
FFN_TF = 512
RT_T = 256
RT_CH = 64

_NT = (((1,), (1,)), ((), ()))


def _vmem_limit(nbytes):
    return int(min(nbytes, V7X_VMEM_BYTES - 4 * 1024 * 1024))


def _proj_kernel(x_ref, g_ref, w_ref, hg_ref,
                 qa_ref, ka_ref, va_ref, qb_ref, kb_ref, vb_ref):
    x = x_ref[...]
    ms = jnp.mean(x * x, axis=-1, keepdims=True)
    h = (x * lax.rsqrt(ms + EPS) * g_ref[...]).astype(jnp.bfloat16)

    def seg(lo, hi):
        return lax.dot_general(w_ref[lo:hi, :], h, _NT,
                               preferred_element_type=jnp.float32)

    def head_norm(blk, gain):
        ssq = jnp.sum(blk * blk, axis=0, keepdims=True)
        return blk * lax.rsqrt(ssq * (1.0 / HEAD_DIM) + EPS) * gain

    g_qa = hg_ref[0 * HEAD_DIM:1 * HEAD_DIM, :]
    g_ka = hg_ref[1 * HEAD_DIM:2 * HEAD_DIM, :]
    g_qb = hg_ref[2 * HEAD_DIM:3 * HEAD_DIM, :]
    g_kb = hg_ref[3 * HEAD_DIM:4 * HEAD_DIM, :]

    o = 0
    p = seg(o, o + QA_W)
    for hd in range(N_HEADS_A):
        r = slice(hd * HEAD_DIM, (hd + 1) * HEAD_DIM)
        qa_ref[r, :] = head_norm(p[r, :], g_qa).astype(qa_ref.dtype)
    o += QA_W
    p = seg(o, o + 2 * KVA_W)
    kn = jnp.concatenate(
        [head_norm(p[hd * HEAD_DIM:(hd + 1) * HEAD_DIM, :], g_ka)
         for hd in range(N_KV_HEADS_A)], axis=0)
    ka_ref[...] = kn.T.astype(ka_ref.dtype)
    va_ref[...] = p[KVA_W:2 * KVA_W, :].astype(va_ref.dtype)
    o += 2 * KVA_W
    p = seg(o, o + QKVB_W)
    for hd in range(N_HEADS_B):
        r = slice(hd * HEAD_DIM, (hd + 1) * HEAD_DIM)
        qb_ref[r, :] = head_norm(p[r, :], g_qb).astype(qb_ref.dtype)
    o += QKVB_W
    p = seg(o, o + QKVB_W)
    kn = jnp.concatenate(
        [head_norm(p[hd * HEAD_DIM:(hd + 1) * HEAD_DIM, :], g_kb)
         for hd in range(N_HEADS_B)], axis=0)
    kb_ref[...] = kn.T.astype(kb_ref.dtype)
    o += QKVB_W
    vb_ref[...] = seg(o, o + QKVB_W).astype(vb_ref.dtype)


def _in_proj(x2d, g_mix, w_in_t, head_gains):
    n, d = x2d.shape
    tm = PROJ_TM
    bf = jnp.bfloat16
    col = lambda i: (0, i)
    row = lambda i: (i, 0)
    const = lambda i: (0, 0)
    out_shape = (
        jax.ShapeDtypeStruct((QA_W, n), bf),
        jax.ShapeDtypeStruct((n, KVA_W), bf),
        jax.ShapeDtypeStruct((KVA_W, n), bf),
        jax.ShapeDtypeStruct((QKVB_W, n), bf),
        jax.ShapeDtypeStruct((n, QKVB_W), bf),
        jax.ShapeDtypeStruct((QKVB_W, n), bf),
    )
    out_specs = (
        pl.BlockSpec((QA_W, tm), col),
        pl.BlockSpec((tm, KVA_W), row),
        pl.BlockSpec((KVA_W, tm), col),
        pl.BlockSpec((QKVB_W, tm), col),
        pl.BlockSpec((tm, QKVB_W), row),
        pl.BlockSpec((QKVB_W, tm), col),
    )
    return pl.pallas_call(
        _proj_kernel,
        grid=(n // tm,),
        in_specs=[
            pl.BlockSpec((tm, d), row),
            pl.BlockSpec((1, d), const),
            pl.BlockSpec((PROJ_W, d), const),
            pl.BlockSpec((4 * HEAD_DIM, tm), const),
        ],
        out_specs=out_specs,
        out_shape=out_shape,
        compiler_params=pltpu.CompilerParams(
            dimension_semantics=("arbitrary",),
            vmem_limit_bytes=_vmem_limit(48 * 1024 * 1024)),
        name="in_proj",
    )(x2d, g_mix, w_in_t, head_gains)


def _window_kernel(blocks_per_seq, q_ref, kp_ref, kc_ref, kn_ref,
                   vp_ref, vc_ref, vn_ref, bias_ref, sink_ref, o_ref):
    i = pl.program_id(0)
    pos = i % blocks_per_seq
    pen_prev = jnp.where(pos == 0, NEG, 0.0).astype(jnp.float32)
    pen_next = jnp.where(pos == blocks_per_seq - 1, NEG, 0.0).astype(jnp.float32)

    kcat = jnp.concatenate([kp_ref[...], kc_ref[...], kn_ref[...]], axis=0)
    vcat = jnp.concatenate([vp_ref[...], vc_ref[...], vn_ref[...]], axis=1)
    sink = sink_ref[...]
    n_sub = WIN_TQ // WIN_SUB
    gq = N_HEADS_A // N_KV_HEADS_A
    zero = jnp.zeros((HEAD_DIM, WIN_SUB), jnp.bfloat16)
    for j in range(n_sub):
        cols = slice(j * WIN_SUB, (j + 1) * WIN_SUB)
        kwin = kcat[j * WIN_SUB:(j + 3) * WIN_SUB, :]
        vwin = vcat[:, j * WIN_SUB:(j + 3) * WIN_SUB]
        halves = []
        for kv in range(N_KV_HEADS_A):
            parts = []
            for hd in range(N_HEADS_A):
                if hd // gq == kv:
                    parts.append(q_ref[hd * HEAD_DIM:(hd + 1) * HEAD_DIM, cols])
                else:
                    parts.append(zero)
            halves.append(jnp.concatenate(parts, axis=1))
        qblk = jnp.concatenate(halves, axis=0)
        s = jnp.dot(kwin, qblk, preferred_element_type=jnp.float32)
        s = s + bias_ref[...]
        if j == 0:
            s = jnp.concatenate([s[:WIN_SUB] + pen_prev, s[WIN_SUB:]], axis=0)
        if j == n_sub - 1:
            s = jnp.concatenate([s[:2 * WIN_SUB], s[2 * WIN_SUB:] + pen_next], axis=0)
        m = jnp.maximum(jnp.max(s, axis=0, keepdims=True), sink)
        p = jnp.exp(s - m)
        denom = jnp.sum(p, axis=0, keepdims=True) + jnp.exp(sink - m)
        pb = p.astype(jnp.bfloat16)
        outs = []
        for kv in range(N_KV_HEADS_A):
            lanes = slice(kv * gq * WIN_SUB, (kv + 1) * gq * WIN_SUB)
            o_t = jnp.dot(vwin[kv * HEAD_DIM:(kv + 1) * HEAD_DIM, :], pb[:, lanes],
                          preferred_element_type=jnp.float32)
            o_t = o_t / denom[:, lanes]
            for g in range(gq):
                outs.append(o_t[:, g * WIN_SUB:(g + 1) * WIN_SUB])
        for a in range(N_HEADS_A // 2):
            pair = jnp.concatenate([outs[2 * a], outs[2 * a + 1]], axis=0)
            o_ref[cols, a * LANE:(a + 1) * LANE] = pair.T.astype(o_ref.dtype)


def _window_attention(qa_t, ka, va_t, bias_t, sink_row, seq_len):
    n = ka.shape[0]
    nblk = n // WIN_TQ
    bps = seq_len // WIN_TQ
    r = WIN_TQ // WIN_SUB
    nsub = n // WIN_SUB
    prev_i = lambda i: jnp.maximum(r * i - 1, 0)
    next_i = lambda i: jnp.minimum(r * i + r, nsub - 1)
    const = lambda i: (0, 0)
    return pl.pallas_call(
        functools.partial(_window_kernel, bps),
        grid=(nblk,),
        in_specs=[
            pl.BlockSpec((QA_W, WIN_TQ), lambda i: (0, i)),
            pl.BlockSpec((WIN_SUB, KVA_W), lambda i: (prev_i(i), 0)),
            pl.BlockSpec((WIN_TQ, KVA_W), lambda i: (i, 0)),
            pl.BlockSpec((WIN_SUB, KVA_W), lambda i: (next_i(i), 0)),
            pl.BlockSpec((KVA_W, WIN_SUB), lambda i: (0, prev_i(i))),
            pl.BlockSpec((KVA_W, WIN_TQ), lambda i: (0, i)),
            pl.BlockSpec((KVA_W, WIN_SUB), lambda i: (0, next_i(i))),
            pl.BlockSpec((3 * WIN_SUB, N_HEADS_A * WIN_SUB), const),
            pl.BlockSpec((1, N_HEADS_A * WIN_SUB), const),
        ],
        out_specs=pl.BlockSpec((WIN_TQ, QA_W), lambda i: (i, 0)),
        out_shape=jax.ShapeDtypeStruct((n, QA_W), jnp.bfloat16),
        compiler_params=pltpu.CompilerParams(
            dimension_semantics=("arbitrary",),
            vmem_limit_bytes=_vmem_limit(40 * 1024 * 1024)),
        name="window_attn",
    )(qa_t, ka, ka, ka, va_t, va_t, va_t, bias_t, sink_row)


def _na_kernel(q_ref, kp_ref, kc_ref, kn_ref, vp_ref, vc_ref, vn_ref,
               bias_ref, o_ref):
    zero = jnp.zeros((HEAD_DIM, NA_TQ), jnp.bfloat16)
    for pr in range(N_HEADS_B // 2):
        lanes = slice(pr * LANE, (pr + 1) * LANE)
        kwin = jnp.concatenate(
            [kp_ref[:, lanes], kc_ref[:, lanes], kn_ref[:, lanes]], axis=0)
        vwin = jnp.concatenate(
            [vp_ref[lanes, :], vc_ref[lanes, :], vn_ref[lanes, :]], axis=1)
        q0 = q_ref[(2 * pr) * HEAD_DIM:(2 * pr + 1) * HEAD_DIM, :]
        q1 = q_ref[(2 * pr + 1) * HEAD_DIM:(2 * pr + 2) * HEAD_DIM, :]
        qblk = jnp.concatenate(
            [jnp.concatenate([q0, zero], axis=1),
             jnp.concatenate([zero, q1], axis=1)], axis=0)
        s = jnp.dot(kwin, qblk, preferred_element_type=jnp.float32)
        outs = []
        for t in range(2):
            st = s[:, t * NA_TQ:(t + 1) * NA_TQ] + bias_ref[0, 2 * pr + t]
            m = jnp.max(st, axis=0, keepdims=True)
            p = jnp.exp(st - m)
            denom = jnp.sum(p, axis=0, keepdims=True)
            o_t = jnp.dot(vwin[t * HEAD_DIM:(t + 1) * HEAD_DIM, :],
                          p.astype(jnp.bfloat16),
                          preferred_element_type=jnp.float32)
            outs.append(o_t / denom)
        pair = jnp.concatenate(outs, axis=0)
        o_ref[:, lanes] = pair.T.astype(o_ref.dtype)


def _na_attention(qb_t, kb, vb_t, bias, seq_len):
    n = kb.shape[0]
    ng = n // NA_TQ
    gps = seq_len // NA_TQ
    prev_i = lambda g: jnp.maximum(g - 1, 0)
    next_i = lambda g: jnp.minimum(g + 1, ng - 1)

    def variant(g):
        pos = g % gps
        return jnp.where(pos == 0, 0, jnp.where(pos == gps - 1, 2, 1))

    return pl.pallas_call(
        _na_kernel,
        grid=(ng,),
        in_specs=[
            pl.BlockSpec((QKVB_W, NA_TQ), lambda g: (0, g)),
            pl.BlockSpec((NA_TQ, QKVB_W), lambda g: (prev_i(g), 0)),
            pl.BlockSpec((NA_TQ, QKVB_W), lambda g: (g, 0)),
            pl.BlockSpec((NA_TQ, QKVB_W), lambda g: (next_i(g), 0)),
            pl.BlockSpec((QKVB_W, NA_TQ), lambda g: (0, prev_i(g))),
            pl.BlockSpec((QKVB_W, NA_TQ), lambda g: (0, g)),
            pl.BlockSpec((QKVB_W, NA_TQ), lambda g: (0, next_i(g))),
            pl.BlockSpec((1, N_HEADS_B, NA_KEY_ROWS * GRID_W, NA_TQ),
                         lambda g: (variant(g), 0, 0, 0)),
        ],
        out_specs=pl.BlockSpec((NA_TQ, QKVB_W), lambda g: (g, 0)),
        out_shape=jax.ShapeDtypeStruct((n, QKVB_W), jnp.bfloat16),
        compiler_params=pltpu.CompilerParams(
            dimension_semantics=("arbitrary",),
            vmem_limit_bytes=_vmem_limit(48 * 1024 * 1024)),
        name="na_attn",
    )(qb_t, kb, kb, kb, vb_t, vb_t, vb_t, bias)


def _post_kernel(a_ref, b_ref, x_ref, ga_ref, gb_ref, w_ref, gf_ref,
                 wrh_ref, wrl_ref, x1_ref, h_ref, aff_ref):
    def rms(v, g):
        ms = jnp.mean(v * v, axis=-1, keepdims=True)
        return v * lax.rsqrt(ms + EPS) * g

    an = rms(a_ref[...].astype(jnp.float32), ga_ref[...]).astype(jnp.bfloat16)
    bn = rms(b_ref[...].astype(jnp.float32), gb_ref[...]).astype(jnp.bfloat16)
    y = jnp.dot(an, w_ref[:QA_W, :], preferred_element_type=jnp.float32)
    y = y + jnp.dot(bn, w_ref[QA_W:, :], preferred_element_type=jnp.float32)
    x1 = x_ref[...] + y
    x1_ref[...] = x1
    h = rms(x1, gf_ref[...])
    h_hi = h.astype(jnp.bfloat16)
    h_lo = (h - h_hi.astype(jnp.float32)).astype(jnp.bfloat16)
    h_ref[...] = h_hi
    wrh = wrh_ref[...]
    logits = lax.dot_general(wrh, h_hi, _NT, preferred_element_type=jnp.float32)
    logits = logits + lax.dot_general(wrh, h_lo, _NT, preferred_element_type=jnp.float32)
    logits = logits + lax.dot_general(wrl_ref[...], h_hi, _NT,
                                      preferred_element_type=jnp.float32)
    m = jnp.max(logits, axis=0, keepdims=True)
    e = jnp.exp(logits - m)
    aff_ref[...] = e / jnp.sum(e, axis=0, keepdims=True)


def _post_attn(out_a, out_b, x2d, g_a, g_b, w_out, g_ffn, wr_hi, wr_lo):
    n, d = x2d.shape
    tm = POST_TM
    row = lambda i: (i, 0)
    const = lambda i: (0, 0)
    return pl.pallas_call(
        _post_kernel,
        grid=(n // tm,),
        in_specs=[
            pl.BlockSpec((tm, QA_W), row),
            pl.BlockSpec((tm, QKVB_W), row),
            pl.BlockSpec((tm, d), row),
            pl.BlockSpec((1, QA_W), const),
            pl.BlockSpec((1, QKVB_W), const),
            pl.BlockSpec((QA_W + QKVB_W, d), const),
            pl.BlockSpec((1, d), const),
            pl.BlockSpec((N_EXPERTS, d), const),
            pl.BlockSpec((N_EXPERTS, d), const),
        ],
        out_specs=(
            pl.BlockSpec((tm, d), row),
            pl.BlockSpec((tm, d), row),
            pl.BlockSpec((N_EXPERTS, tm), lambda i: (0, i)),
        ),
        out_shape=(
            jax.ShapeDtypeStruct((n, d), jnp.float32),
            jax.ShapeDtypeStruct((n, d), jnp.bfloat16),
            jax.ShapeDtypeStruct((N_EXPERTS, n), jnp.float32),
        ),
        compiler_params=pltpu.CompilerParams(
            dimension_semantics=("arbitrary",),
            vmem_limit_bytes=_vmem_limit(40 * 1024 * 1024)),
        name="post_attn",
    )(out_a, out_b, x2d, g_a, g_b, w_out, g_ffn, wr_hi, wr_lo)


def _strict_upper(n):
    r = lax.broadcasted_iota(jnp.int32, (n, n), 0)
    c = lax.broadcasted_iota(jnp.int32, (n, n), 1)
    return jnp.where(r < c, 1.0, 0.0).astype(jnp.bfloat16)


def _route_kernel(cap, aff_ref, rel_ref, lo_ref):
    n = aff_ref.shape[1]
    nt = n // RT_T
    w = lo_ref.shape[1]
    cap_f = jnp.float32(cap)

    def count(mask):
        return jnp.sum(jnp.where(mask, 1.0, 0.0), axis=1, keepdims=True)

    def search(b, ans):
        cand = ans | jnp.left_shift(jnp.int32(1), 30 - b)
        bits = pltpu.bitcast(aff_ref[...], jnp.int32)
        return jnp.where(count(bits >= cand) >= cap_f, cand, ans)

    thr = lax.fori_loop(0, 31, search, jnp.zeros((N_EXPERTS, 1), jnp.int32))
    need = cap_f - count(pltpu.bitcast(aff_ref[...], jnp.int32) > thr)

    tri = _strict_upper(RT_T)
    lane = lax.broadcasted_iota(jnp.int32, (N_EXPERTS, w), 1)

    lo_ref[...] = jnp.zeros_like(lo_ref)

    def tile(c, carry):
        run_sel, run_eq = carry
        start = pl.multiple_of(c * RT_T, RT_T)
        bits = pltpu.bitcast(aff_ref[:, pl.ds(start, RT_T)], jnp.int32)
        gt = bits > thr
        eq = bits == thr
        eq_b = jnp.where(eq, 1.0, 0.0).astype(jnp.bfloat16)
        eq_rank = jnp.dot(eq_b, tri, preferred_element_type=jnp.float32)
        sel = gt | (eq & (run_eq + eq_rank < need))
        sel_b = jnp.where(sel, 1.0, 0.0).astype(jnp.bfloat16)
        rank = jnp.dot(sel_b, tri, preferred_element_type=jnp.float32)
        rel_ref[:, pl.ds(start, RT_T)] = jnp.where(sel, rank, -1.0).astype(jnp.int32)
        lo_ref[...] = jnp.where(lane == c, run_sel.astype(jnp.int32), lo_ref[...])
        return run_sel + count(sel), run_eq + count(eq)

    zero = need * 0.0
    run_sel, _ = lax.fori_loop(0, nt, tile, (zero, zero))
    lo_ref[...] = jnp.where(lane >= nt, run_sel.astype(jnp.int32), lo_ref[...])


def _route(aff_t, cap):
    e, n = aff_t.shape
    nt = n // RT_T
    w = nt + LANE
    full = lambda i: (0, 0)
    return pl.pallas_call(
        functools.partial(_route_kernel, cap),
        grid=(1,),
        in_specs=[pl.BlockSpec((e, n), full)],
        out_specs=(pl.BlockSpec((e, n), full), pl.BlockSpec((e, w), full)),
        out_shape=(jax.ShapeDtypeStruct((e, n), jnp.int32),
                   jax.ShapeDtypeStruct((e, w), jnp.int32)),
        compiler_params=pltpu.CompilerParams(
            dimension_semantics=("arbitrary",),
            vmem_limit_bytes=_vmem_limit(40 * 1024 * 1024)),
        name="route",
    )(aff_t)


def _pack_pairs(x):
    w = x.shape[1] // 2
    lo = pltpu.bitcast(x[:, :w], jnp.uint32)
    hi = pltpu.bitcast(x[:, w:], jnp.uint32)
    return lo | (hi >> 16)


def _unpack_pairs(p):
    lo = pltpu.bitcast(p & jnp.uint32(0xFFFF0000), jnp.float32).astype(jnp.bfloat16)
    hi = pltpu.bitcast(p << 16, jnp.float32).astype(jnp.bfloat16)
    return lo, hi


def _one_hot_rows(rel_ref, shift):
    kio = lax.broadcasted_iota(jnp.int32, (RT_CH, RT_T), 0)
    blocks = []
    for e in range(N_EXPERTS):
        hit = (rel_ref[e:e + 1, :] - shift) == kio
        blocks.append(jnp.where(hit, 1.0, 0.0).astype(jnp.bfloat16))
    return jnp.concatenate(blocks, axis=0)


def _dispatch_kernel(cap, cpad, nt, w, lo_ref, h_ref, rel_ref, xe_hbm,
                     stage, stage_x, sem, sem_x):
    j = pl.program_id(0)
    slot = j % 2

    def dst(e, jj, c):
        row = e * cpad + lo_ref[e * w + jj] + c * RT_CH
        return xe_hbm.at[pl.ds(row, RT_CH), 0]

    def chunk_copy(e, jj, sl):
        return pltpu.make_async_copy(
            stage.at[sl, pl.ds(e * RT_CH, RT_CH)], dst(e, jj, 0), sem.at[sl])

    @pl.when(j == 0)
    def _():
        pad = cpad - cap
        stage_x[...] = jnp.zeros_like(stage_x)
        fills = [pltpu.make_async_copy(
            stage_x.at[pl.ds(0, pad)], xe_hbm.at[pl.ds(e * cpad + cap, pad), 0], sem_x)
            for e in range(N_EXPERTS)]
        for f in fills:
            f.start()
        for f in fills:
            f.wait()

    x = jnp.dot(_one_hot_rows(rel_ref, 0), h_ref[...], preferred_element_type=jnp.float32)
    stage[slot] = _pack_pairs(x)

    @pl.when(j > 0)
    def _():
        for e in range(N_EXPERTS):
            chunk_copy(e, j - 1, 1 - slot).wait()

    for e in range(N_EXPERTS):
        chunk_copy(e, j, slot).start()

    cnts = [lo_ref[e * w + j + 1] - lo_ref[e * w + j] for e in range(N_EXPERTS)]
    most = functools.reduce(jnp.maximum, cnts)
    n_pass = jnp.right_shift(most + (RT_CH - 1), RT_CH.bit_length() - 1)

    def extra(c, carry):
        xx = jnp.dot(_one_hot_rows(rel_ref, c * RT_CH), h_ref[...],
                     preferred_element_type=jnp.float32)
        stage_x[...] = _pack_pairs(xx)
        for e in range(N_EXPERTS):
            @pl.when(cnts[e] > c * RT_CH)
            def _():
                cp = pltpu.make_async_copy(
                    stage_x.at[pl.ds(e * RT_CH, RT_CH)], dst(e, j, c), sem_x)
                cp.start()
                cp.wait()
        return carry

    lax.fori_loop(1, n_pass, extra, 0)

    @pl.when(j == nt - 1)
    def _():
        for e in range(N_EXPERTS):
            chunk_copy(e, j, slot).wait()


def _dispatch(h, rel_t, lo_flat, cap, cpad):
    n, d = h.shape
    nt = n // RT_T
    w = lo_flat.shape[0] // N_EXPERTS
    rows = N_EXPERTS * RT_CH
    assert cpad - cap <= rows
    return pl.pallas_call(
        functools.partial(_dispatch_kernel, cap, cpad, nt, w),
        grid_spec=pltpu.PrefetchScalarGridSpec(
            num_scalar_prefetch=1,
            grid=(nt,),
            in_specs=[
                pl.BlockSpec((RT_T, d), lambda j, lo: (j, 0)),
                pl.BlockSpec((N_EXPERTS, RT_T), lambda j, lo: (0, j)),
            ],
            out_specs=pl.BlockSpec(memory_space=pl.ANY),
            scratch_shapes=[
                pltpu.VMEM((2, rows, d // 2), jnp.uint32),
                pltpu.VMEM((rows, d // 2), jnp.uint32),
                pltpu.SemaphoreType.DMA((2,)),
                pltpu.SemaphoreType.DMA(()),
            ],
        ),
        out_shape=jax.ShapeDtypeStruct((N_EXPERTS * cpad, 1, d // 2), jnp.uint32),
        compiler_params=pltpu.CompilerParams(
            dimension_semantics=("arbitrary",),
            vmem_limit_bytes=_vmem_limit(40 * 1024 * 1024)),
        name="dispatch",
    )(lo_flat, h, rel_t)


def _ffn_kernel(x_ref, wg_ref, wu_ref, wd_ref, o_ref):
    x_lo, x_hi = _unpack_pairs(x_ref[:, 0, :])
    half = x_lo.shape[1]
    d_ff = wg_ref.shape[2]
    acc = None
    for c in range(d_ff // FFN_TF):
        f = slice(c * FFN_TF, (c + 1) * FFN_TF)
        g = jnp.dot(x_lo, wg_ref[0, :half, f], preferred_element_type=jnp.float32)
        g = g + jnp.dot(x_hi, wg_ref[0, half:, f], preferred_element_type=jnp.float32)
        u = jnp.dot(x_lo, wu_ref[0, :half, f], preferred_element_type=jnp.float32)
        u = u + jnp.dot(x_hi, wu_ref[0, half:, f], preferred_element_type=jnp.float32)
        act = (g * jax.nn.sigmoid(g) * u).astype(jnp.bfloat16)
        part = jnp.dot(act, wd_ref[0, f, :], preferred_element_type=jnp.float32)
        acc = part if acc is None else acc + part
    o_ref[...] = acc.astype(o_ref.dtype)


def _expert_ffn(xe, w_gate, w_up, w_down, cap, cpad, tm):
    e, d, d_ff = w_gate.shape
    tiles = cap // tm
    stride = cpad // tm
    return pl.pallas_call(
        _ffn_kernel,
        grid=(e, tiles),
        in_specs=[
            pl.BlockSpec((tm, 1, d // 2), lambda i, j: (i * stride + j, 0, 0)),
            pl.BlockSpec((1, d, d_ff), lambda i, j: (i, 0, 0)),
            pl.BlockSpec((1, d, d_ff), lambda i, j: (i, 0, 0)),
            pl.BlockSpec((1, d_ff, d), lambda i, j: (i, 0, 0)),
        ],
        out_specs=pl.BlockSpec((tm, d), lambda i, j: (i * tiles + j, 0)),
        out_shape=jax.ShapeDtypeStruct((e * cap, d), jnp.bfloat16),
        compiler_params=pltpu.CompilerParams(
            dimension_semantics=("arbitrary", "arbitrary"),
            vmem_limit_bytes=_vmem_limit(56 * 1024 * 1024)),
        name="expert_ffn",
    )(xe, w_gate, w_up, w_down)


def _combine_kernel(cap, nt, w, lo_ref, x_ref, rel_ref, aff_ref, ye_hbm, o_ref,
                    ybuf, ybuf_x, sem, sem_x):
    j = pl.program_id(0)
    slot = j % 2
    last_start = N_EXPERTS * cap - RT_CH
    align = 16

    def start_row(e, jj, c):
        lo = lo_ref[e * w + jj]
        a = e * cap + lo - (lo & (align - 1)) + c * RT_CH
        return pl.multiple_of(jnp.minimum(a, last_start), align)

    def fetch(e, jj, sl):
        return pltpu.make_async_copy(
            ye_hbm.at[pl.ds(start_row(e, jj, 0), RT_CH)],
            ybuf.at[sl, pl.ds(e * RT_CH, RT_CH)], sem.at[sl])

    @pl.when(j == 0)
    def _():
        for e in range(N_EXPERTS):
            fetch(e, 0, 0).start()

    @pl.when(j + 1 < nt)
    def _():
        for e in range(N_EXPERTS):
            fetch(e, j + 1, 1 - slot).start()

    los =[lo_ref[e * w + j] for e in range(N_EXPERTS)]
    cnts = [lo_ref[e * w + j + 1] - los[e] for e in range(N_EXPERTS)]
    lead = [los[e] & (align - 1) for e in range(N_EXPERTS)]

    def weights(c):
        kio = lax.broadcasted_iota(jnp.int32, (RT_CH, RT_T), 0)
        blocks = []
        for e in range(N_EXPERTS):
            r = rel_ref[e:e + 1, :]
            p = r + lead[e]
            member = (r >= 0) & (p >= c * RT_CH) & (p < (c + 1) * RT_CH)
            off = e * cap + los[e] - start_row(e, j, c)
            hit = member & ((r + off) == kio)
            blocks.append(jnp.where(hit, aff_ref[e:e + 1, :], 0.0).astype(jnp.bfloat16))
        return jnp.concatenate(blocks, axis=0)

    tn = (((0,), (0,)), ((), ()))
    wt0 = weights(0)
    for e in range(N_EXPERTS):
        fetch(e, j, slot).wait()
    o_ref[...] = x_ref[...] + lax.dot_general(
        wt0, ybuf[slot], tn, preferred_element_type=jnp.float32)

    spans = [lead[e] + cnts[e] for e in range(N_EXPERTS)]
    most = functools.reduce(jnp.maximum, spans)
    n_pass = jnp.right_shift(most + (RT_CH - 1), RT_CH.bit_length() - 1)

    def extra(c, carry):
        for e in range(N_EXPERTS):
            pltpu.make_async_copy(
                ye_hbm.at[pl.ds(start_row(e, j, c), RT_CH)],
                ybuf_x.at[pl.ds(e * RT_CH, RT_CH)], sem_x).start()
        wtc = weights(c)
        for e in range(N_EXPERTS):
            pltpu.make_async_copy(
                ye_hbm.at[pl.ds(start_row(e, j, c), RT_CH)],
                ybuf_x.at[pl.ds(e * RT_CH, RT_CH)], sem_x).wait()
        o_ref[...] += lax.dot_general(wtc, ybuf_x[...], tn,
                                      preferred_element_type=jnp.float32)
        return carry

    lax.fori_loop(1, n_pass, extra, 0)


def _combine(x1, rel_t, aff_t, ye, lo_flat, cap):
    n, d = x1.shape
    nt = n // RT_T
    w = lo_flat.shape[0] // N_EXPERTS
    rows = N_EXPERTS * RT_CH
    return pl.pallas_call(
        functools.partial(_combine_kernel, cap, nt, w),
        grid_spec=pltpu.PrefetchScalarGridSpec(
            num_scalar_prefetch=1,
            grid=(nt,),
            in_specs=[
                pl.BlockSpec((RT_T, d), lambda j, lo: (j, 0)),
                pl.BlockSpec((N_EXPERTS, RT_T), lambda j, lo: (0, j)),
                pl.BlockSpec((N_EXPERTS, RT_T), lambda j, lo: (0, j)),
                pl.BlockSpec(memory_space=pl.ANY),
            ],
            out_specs=pl.BlockSpec((RT_T, d), lambda j, lo: (j, 0)),
            scratch_shapes=[
                pltpu.VMEM((2, rows, d), jnp.bfloat16),
                pltpu.VMEM((rows, d), jnp.bfloat16),
                pltpu.SemaphoreType.DMA((2,)),
                pltpu.SemaphoreType.DMA(()),
            ],
        ),
        out_shape=jax.ShapeDtypeStruct((n, d), jnp.float32),
        compiler_params=pltpu.CompilerParams(
            dimension_semantics=("arbitrary",),
            vmem_limit_bytes=_vmem_limit(40 * 1024 * 1024)),
        name="combine",
    )(lo_flat, x1, rel_t, aff_t, ye)


def _window_bias_t():
    j = jnp.arange(3 * WIN_SUB)[:, None]
    i = jnp.arange(WIN_SUB)[None, :]
    dist = jnp.abs(i + WIN_SUB - j).astype(jnp.float32)
    slopes = jnp.exp2(-8.0 * (jnp.arange(N_HEADS_A, dtype=jnp.float32) + 1.0) / N_HEADS_A)
    b = jnp.where(dist[None] <= WINDOW, -(slopes[:, None, None] * dist[None]), NEG)
    return jnp.transpose(b, (1, 0, 2)).reshape(3 * WIN_SUB, N_HEADS_A * WIN_SUB)


def _na_bias_t(rpb):
    kk = jnp.arange(NA_KEY_ROWS)[:, None]
    rho = jnp.arange(NA_GROUP_ROWS)[None, :]
    rel = kk - NA_GROUP_ROWS
    r0 = jnp.stack([
        jnp.zeros_like(rho),
        rho - NA_ROWS // 2,
        jnp.full_like(rho, NA_GROUP_ROWS - NA_ROWS),
    ])
    row_ok = (rel[None] >= r0) & (rel[None] < r0 + NA_ROWS)
    dr = jnp.clip(rel - rho + (NA_ROWS - 1), 0, 2 * NA_ROWS - 2)
    ck = jnp.arange(GRID_W)[:, None]
    cq = jnp.arange(GRID_W)[None, :]
    c0 = jnp.clip(cq - NA_COLS // 2, 0, GRID_W - NA_COLS)
    col_ok = (ck >= c0) & (ck < c0 + NA_COLS)
    dc = jnp.clip(ck - cq + (NA_COLS - 1), 0, 2 * NA_COLS - 2)
    vals = rpb.astype(jnp.float32)[:, dr[:, None, :, None], dc[None, :, None, :]]
    ok = row_ok[:, :, None, :, None] & col_ok[None, None, :, None, :]
    b = jnp.where(ok[:, None], vals[None], NEG)
    h = rpb.shape[0]
    return b.reshape(3, h, NA_KEY_ROWS * GRID_W, NA_TQ)


def _layer_params(p, l):
    scale = 1.0 / math.sqrt(HEAD_DIM)
    gains = jnp.concatenate([p["qnorm_a"][l] * scale, p["knorm_a"][l],
                             p["qnorm_b"][l] * scale, p["knorm_b"][l]])
    wr = p["w_router"][l].T
    wr_hi = wr.astype(jnp.bfloat16)
    return dict(
        g_mix=p["norm_mix"][l][None, :],
        w_in_t=p["w_in"][l].T.astype(jnp.bfloat16),
        head_gains=jnp.broadcast_to(gains[:, None], (4 * HEAD_DIM, PROJ_TM)),
        sink_row=jnp.repeat(p["sink_a"][l].astype(jnp.float32), WIN_SUB)[None, :],
        na_bias=_na_bias_t(p["rpb_b"][l]),
        g_a=p["onorm_a"][l][None, :],
        g_b=p["onorm_b"][l][None, :],
        w_out=p["w_out"][l].astype(jnp.bfloat16),
        g_ffn=p["norm_ffn"][l][None, :],
        wr_hi=wr_hi,
        wr_lo=(wr - wr_hi.astype(jnp.float32)).astype(jnp.bfloat16),
        w_gate=p["w_gate"][l].astype(jnp.bfloat16),
        w_up=p["w_up"][l].astype(jnp.bfloat16),
        w_down=p["w_down"][l].astype(jnp.bfloat16),
    )


def _trunk(x, layers, win_bias):
    b, s, d = x.shape
    n = b * s
    assert s % WIN_TQ == 0 and s // NA_TQ >= 3 and n % PROJ_TM == 0 and n % RT_T == 0
    cap = EC_CAPACITY * n // N_EXPERTS
    tm = min(FFN_TM, cap)
    assert cap % tm == 0 and tm % RT_CH == 0
    cpad = cap + tm
    x2 = x.reshape(n, d)
    for q in layers:
        qa_t, ka, va_t, qb_t, kb, vb_t = _in_proj(x2, q["g_mix"], q["w_in_t"], q["head_gains"])
        out_a = _window_attention(qa_t, ka, va_t, win_bias, q["sink_row"], s)
        out_b = _na_attention(qb_t, kb, vb_t, q["na_bias"], s)
        x1, h, aff_t = _post_attn(out_a, out_b, x2, q["g_a"], q["g_b"], q["w_out"],
                                  q["g_ffn"], q["wr_hi"], q["wr_lo"])
        rel_t, lo = _route(aff_t, cap)
        lo_flat = lo.reshape(-1)
        xe = _dispatch(h, rel_t, lo_flat, cap, cpad)
        ye = _expert_ffn(xe, q["w_gate"], q["w_up"], q["w_down"], cap, cpad, tm)
        x2 = _combine(x1, rel_t, aff_t, ye, lo_flat, cap)
    return x2.reshape(b, s, d)


def kernel(x_prompt, x_sample, norm_mix, w_in, qnorm_a, knorm_a, sink_a, qnorm_b, knorm_b,
           rpb_b, onorm_a, onorm_b, w_out, norm_ffn, w_router, w_gate, w_up, w_down):
    p = dict(norm_mix=norm_mix, w_in=w_in, qnorm_a=qnorm_a, knorm_a=knorm_a, sink_a=sink_a,
             qnorm_b=qnorm_b, knorm_b=knorm_b, rpb_b=rpb_b, onorm_a=onorm_a, onorm_b=onorm_b,
             w_out=w_out, norm_ffn=norm_ffn, w_router=w_router, w_gate=w_gate, w_up=w_up,
             w_down=w_down)
    layers = [_layer_params(p, l) for l in range(w_in.shape[0])]
    win_bias = _window_bias_t()
    return (_trunk(x_prompt, layers, win_bias), _trunk(x_sample, layers, win_bias))
```

```python
import functools
import math

import jax
import jax.numpy as jnp
from jax import lax
from jax.experimental import pallas as pl
from jax.experimental.pallas import tpu as pltpu

HEAD_DIM = 64
N_HEADS_A = 8
N_KV_HEADS_A = 2
N_HEADS_B = 8
QA_W = N_HEADS_A * HEAD_DIM
KVA_W = N_KV_HEADS_A * HEAD_DIM
QKVB_W = N_HEADS_B * HEAD_DIM
PROJ_W = QA_W + 2 * KVA_W + 3 * QKVB_W
WINDOW = 128
GRID_W = 64
NA_ROWS = 8
NA_COLS = 16
N_EXPERTS = 16
EC_CAPACITY = 2
EPS = 1e-6
NEG = -1e30

LANE = 128
V7X_VMEM_BYTES = 64 * 1024 * 1024

PROJ_TM = 512
WIN_TQ = 512
WIN_SUB = WINDOW
NA_GROUP_ROWS = 4
NA_TQ = NA_GROUP_ROWS * GRID_W
NA_KEY_ROWS = 3 * NA_GROUP_ROWS
POST_TM = 512
FFN_TM = 512
FFN_TF = 512
RT_T = 256
RT_CH = 64

_NT = (((1,), (1,)), ((), ()))


def _vmem_limit(nbytes):
    return int(min(nbytes, V7X_VMEM_BYTES - 4 * 1024 * 1024))


def _proj_kernel(x_ref, g_ref, w_ref, hg_ref,
                 qa_ref, ka_ref, va_ref, qb_ref, kb_ref, vb_ref):
    x = x_ref[...]
    ms = jnp.mean(x * x, axis=-1, keepdims=True)
    h = (x * lax.rsqrt(ms + EPS) * g_ref[...]).astype(jnp.bfloat16)

    def seg(lo, hi):
        return lax.dot_general(w_ref[lo:hi, :], h, _NT,
                               preferred_element_type=jnp.float32)

    def head_norm(blk, gain):
        ssq = jnp.sum(blk * blk, axis=0, keepdims=True)
        return blk * lax.rsqrt(ssq * (1.0 / HEAD_DIM) + EPS) * gain

    g_qa = hg_ref[0 * HEAD_DIM:1 * HEAD_DIM, :]
    g_ka = hg_ref[1 * HEAD_DIM:2 * HEAD_DIM, :]
    g_qb = hg_ref[2 * HEAD_DIM:3 * HEAD_DIM, :]
    g_kb = hg_ref[3 * HEAD_DIM:4 * HEAD_DIM, :]

    o = 0
    p = seg(o, o + QA_W)
    for hd in range(N_HEADS_A):
        r = slice(hd * HEAD_DIM, (hd + 1) * HEAD_DIM)
        qa_ref[r, :] = head_norm(p[r, :], g_qa).astype(qa_ref.dtype)
    o += QA_W
    p = seg(o, o + 2 * KVA_W)
    kn = jnp.concatenate(
        [head_norm(p[hd * HEAD_DIM:(hd + 1) * HEAD_DIM, :], g_ka)
         for hd in range(N_KV_HEADS_A)], axis=0)
    ka_ref[...] = kn.T.astype(ka_ref.dtype)
    va_ref[...] = p[KVA_W:2 * KVA_W, :].astype(va_ref.dtype)
    o += 2 * KVA_W
    p = seg(o, o + QKVB_W)
    for hd in range(N_HEADS_B):
        r = slice(hd * HEAD_DIM, (hd + 1) * HEAD_DIM)
        qb_ref[r, :] = head_norm(p[r, :], g_qb).astype(qb_ref.dtype)
    o += QKVB_W
    p = seg(o, o + QKVB_W)
    kn = jnp.concatenate(
        [head_norm(p[hd * HEAD_DIM:(hd + 1) * HEAD_DIM, :], g_kb)
         for hd in range(N_HEADS_B)], axis=0)
    kb_ref[...] = kn.T.astype(kb_ref.dtype)
    o += QKVB_W
    vb_ref[...] = seg(o, o + QKVB_W).astype(vb_ref.dtype)


def _in_proj(x2d, g_mix, w_in_t, head_gains):
    n, d = x2d.shape
    tm = PROJ_TM
    bf = jnp.bfloat16
    col = lambda i: (0, i)
    row = lambda i: (i, 0)
    const = lambda i: (0, 0)
    out_shape = (
        jax.ShapeDtypeStruct((QA_W, n), bf),
        jax.ShapeDtypeStruct((n, KVA_W), bf),
        jax.ShapeDtypeStruct((KVA_W, n), bf),
        jax.ShapeDtypeStruct((QKVB_W, n), bf),
        jax.ShapeDtypeStruct((n, QKVB_W), bf),
        jax.ShapeDtypeStruct((QKVB_W, n), bf),
    )
    out_specs = (
        pl.BlockSpec((QA_W, tm), col),
        pl.BlockSpec((tm, KVA_W), row),
        pl.BlockSpec((KVA_W, tm), col),
        pl.BlockSpec((QKVB_W, tm), col),
        pl.BlockSpec((tm, QKVB_W), row),
        pl.BlockSpec((QKVB_W, tm), col),
    )
    return pl.pallas_call(
        _proj_kernel,
        grid=(n // tm,),
        in_specs=[
            pl.BlockSpec((tm, d), row),
            pl.BlockSpec((1, d), const),
            pl.BlockSpec((PROJ_W, d), const),
            pl.BlockSpec((4 * HEAD_DIM, tm), const),
        ],
        out_specs=out_specs,
        out_shape=out_shape,
        compiler_params=pltpu.CompilerParams(
            dimension_semantics=("arbitrary",),
            vmem_limit_bytes=_vmem_limit(48 * 1024 * 1024)),
        name="in_proj",
    )(x2d, g_mix, w_in_t, head_gains)


def _window_kernel(blocks_per_seq, q_ref, kp_ref, kc_ref, kn_ref,
                   vp_ref, vc_ref, vn_ref, bias_ref, sink_ref, o_ref):
    i = pl.program_id(0)
    pos = i % blocks_per_seq
    pen_prev = jnp.where(pos == 0, NEG, 0.0).astype(jnp.float32)
    pen_next = jnp.where(pos == blocks_per_seq - 1, NEG, 0.0).astype(jnp.float32)

    kcat = jnp.concatenate([kp_ref[...], kc_ref[...], kn_ref[...]], axis=0)
    vcat = jnp.concatenate([vp_ref[...], vc_ref[...], vn_ref[...]], axis=1)
    sink = sink_ref[...]
    n_sub = WIN_TQ // WIN_SUB
    gq = N_HEADS_A // N_KV_HEADS_A
    zero = jnp.zeros((HEAD_DIM, WIN_SUB), jnp.bfloat16)
    for j in range(n_sub):
        cols = slice(j * WIN_SUB, (j + 1) * WIN_SUB)
        kwin = kcat[j * WIN_SUB:(j + 3) * WIN_SUB, :]
        vwin = vcat[:, j * WIN_SUB:(j + 3) * WIN_SUB]
        halves = []
        for kv in range(N_KV_HEADS_A):
            parts = []
            for hd in range(N_HEADS_A):
                if hd // gq == kv:
                    parts.append(q_ref[hd * HEAD_DIM:(hd + 1) * HEAD_DIM, cols])
                else:
                    parts.append(zero)
            halves.append(jnp.concatenate(parts, axis=1))
        qblk = jnp.concatenate(halves, axis=0)
        s = jnp.dot(kwin, qblk, preferred_element_type=jnp.float32)
        s = s + bias_ref[...]
        if j == 0:
            s = jnp.concatenate([s[:WIN_SUB] + pen_prev, s[WIN_SUB:]], axis=0)
        if j == n_sub - 1:
            s = jnp.concatenate([s[:2 * WIN_SUB], s[2 * WIN_SUB:] + pen_next], axis=0)
        m = jnp.maximum(jnp.max(s, axis=0, keepdims=True), sink)
        p = jnp.exp(s - m)
        denom = jnp.sum(p, axis=0, keepdims=True) + jnp.exp(sink - m)
        pb = p.astype(jnp.bfloat16)
        outs = []
        for kv in range(N_KV_HEADS_A):
            lanes = slice(kv * gq * WIN_SUB, (kv + 1) * gq * WIN_SUB)
            o_t = jnp.dot(vwin[kv * HEAD_DIM:(kv + 1) * HEAD_DIM, :], pb[:, lanes],
                          preferred_element_type=jnp.float32)
            o_t = o_t / denom[:, lanes]
            for g in range(gq):
                outs.append(o_t[:, g * WIN_SUB:(g + 1) * WIN_SUB])
        for a in range(N_HEADS_A // 2):
            pair = jnp.concatenate([outs[2 * a], outs[2 * a + 1]], axis=0)
            o_ref[cols, a * LANE:(a + 1) * LANE] = pair.T.astype(o_ref.dtype)


def _window_attention(qa_t, ka, va_t, bias_t, sink_row, seq_len):
    n = ka.shape[0]
    nblk = n // WIN_TQ
    bps = seq_len // WIN_TQ
    r = WIN_TQ // WIN_SUB
    nsub = n // WIN_SUB
    prev_i = lambda i: jnp.maximum(r * i - 1, 0)
    next_i = lambda i: jnp.minimum(r * i + r, nsub - 1)
    const = lambda i: (0, 0)
    return pl.pallas_call(
        functools.partial(_window_kernel, bps),
        grid=(nblk,),
        in_specs=[
            pl.BlockSpec((QA_W, WIN_TQ), lambda i: (0, i)),
            pl.BlockSpec((WIN_SUB, KVA_W), lambda i: (prev_i(i), 0)),
            pl.BlockSpec((WIN_TQ, KVA_W), lambda i: (i, 0)),
            pl.BlockSpec((WIN_SUB, KVA_W), lambda i: (next_i(i), 0)),
            pl.BlockSpec((KVA_W, WIN_SUB), lambda i: (0, prev_i(i))),
            pl.BlockSpec((KVA_W, WIN_TQ), lambda i: (0, i)),
            pl.BlockSpec((KVA_W, WIN_SUB), lambda i: (0, next_i(i))),
            pl.BlockSpec((3 * WIN_SUB, N_HEADS_A * WIN_SUB), const),
            pl.BlockSpec((1, N_HEADS_A * WIN_SUB), const),
        ],
        out_specs=pl.BlockSpec((WIN_TQ, QA_W), lambda i: (i, 0)),
        out_shape=jax.ShapeDtypeStruct((n, QA_W), jnp.bfloat16),
        compiler_params=pltpu.CompilerParams(
            dimension_semantics=("arbitrary",),
            vmem_limit_bytes=_vmem_limit(40 * 1024 * 1024)),
        name="window_attn",
    )(qa_t, ka, ka, ka, va_t, va_t, va_t, bias_t, sink_row)


def _na_kernel(q_ref, kp_ref, kc_ref, kn_ref, vp_ref, vc_ref, vn_ref,
               bias_ref, o_ref):
    zero = jnp.zeros((HEAD_DIM, NA_TQ), jnp.bfloat16)
    for pr in range(N_HEADS_B // 2):
        lanes = slice(pr * LANE, (pr + 1) * LANE)
        kwin = jnp.concatenate(
            [kp_ref[:, lanes], kc_ref[:, lanes], kn_ref[:, lanes]], axis=0)
        vwin = jnp.concatenate(
            [vp_ref[lanes, :], vc_ref[lanes, :], vn_ref[lanes, :]], axis=1)
        q0 = q_ref[(2 * pr) * HEAD_DIM:(2 * pr + 1) * HEAD_DIM, :]
        q1 = q_ref[(2 * pr + 1) * HEAD_DIM:(2 * pr + 2) * HEAD_DIM, :]
        qblk = jnp.concatenate(
            [jnp.concatenate([q0, zero], axis=1),
             jnp.concatenate([zero, q1], axis=1)], axis=0)
        s = jnp.dot(kwin, qblk, preferred_element_type=jnp.float32)
        outs = []
        for t in range(2):
            st = s[:, t * NA_TQ:(t + 1) * NA_TQ] + bias_ref[0, 2 * pr + t]
            m = jnp.max(st, axis=0, keepdims=True)
            p = jnp.exp(st - m)
            denom = jnp.sum(p, axis=0, keepdims=True)
            o_t = jnp.dot(vwin[t * HEAD_DIM:(t + 1) * HEAD_DIM, :],
                          p.astype(jnp.bfloat16),
                          preferred_element_type=jnp.float32)
            outs.append(o_t / denom)
        pair = jnp.concatenate(outs, axis=0)
        o_ref[:, lanes] = pair.T.astype(o_ref.dtype)


def _na_attention(qb_t, kb, vb_t, bias, seq_len):
    n = kb.shape[0]
    ng = n // NA_TQ
    gps = seq_len // NA_TQ
    prev_i = lambda g: jnp.maximum(g - 1, 0)
    next_i = lambda g: jnp.minimum(g + 1, ng - 1)

    def variant(g):
        pos = g % gps
        return jnp.where(pos == 0, 0, jnp.where(pos == gps - 1, 2, 1))

    return pl.pallas_call(
        _na_kernel,
        grid=(ng,),
        in_specs=[
            pl.BlockSpec((QKVB_W, NA_TQ), lambda g: (0, g)),
            pl.BlockSpec((NA_TQ, QKVB_W), lambda g: (prev_i(g), 0)),
            pl.BlockSpec((NA_TQ, QKVB_W), lambda g: (g, 0)),
            pl.BlockSpec((NA_TQ, QKVB_W), lambda g: (next_i(g), 0)),
            pl.BlockSpec((QKVB_W, NA_TQ), lambda g: (0, prev_i(g))),
            pl.BlockSpec((QKVB_W, NA_TQ), lambda g: (0, g)),
            pl.BlockSpec((QKVB_W, NA_TQ), lambda g: (0, next_i(g))),
            pl.BlockSpec((1, N_HEADS_B, NA_KEY_ROWS * GRID_W, NA_TQ),
                         lambda g: (variant(g), 0, 0, 0)),
        ],
        out_specs=pl.BlockSpec((NA_TQ, QKVB_W), lambda g: (g, 0)),
        out_shape=jax.ShapeDtypeStruct((n, QKVB_W), jnp.bfloat16),
        compiler_params=pltpu.CompilerParams(
            dimension_semantics=("arbitrary",),
            vmem_limit_bytes=_vmem_limit(48 * 1024 * 1024)),
        name="na_attn",
    )(qb_t, kb, kb, kb, vb_t, vb_t, vb_t, bias)


def _post_kernel(a_ref, b_ref, x_ref, ga_ref, gb_ref, w_ref, gf_ref,
                 wrh_ref, wrl_ref, x1_ref, h_ref, aff_ref):
    def rms(v, g):
        ms = jnp.mean(v * v, axis=-1, keepdims=True)
        return v * lax.rsqrt(ms + EPS) * g

    an = rms(a_ref[...].astype(jnp.float32), ga_ref[...]).astype(jnp.bfloat16)
    bn = rms(b_ref[...].astype(jnp.float32), gb_ref[...]).astype(jnp.bfloat16)
    y = jnp.dot(an, w_ref[:QA_W, :], preferred_element_type=jnp.float32)
    y = y + jnp.dot(bn, w_ref[QA_W:, :], preferred_element_type=jnp.float32)
    x1 = x_ref[...] + y
    x1_ref[...] = x1
    h = rms(x1, gf_ref[...])
    h_hi = h.astype(jnp.bfloat16)
    h_lo = (h - h_hi.astype(jnp.float32)).astype(jnp.bfloat16)
    h_ref[...] = h_hi
    wrh = wrh_ref[...]
    logits = lax.dot_general(wrh, h_hi, _NT, preferred_element_type=jnp.float32)
    logits = logits + lax.dot_general(wrh, h_lo, _NT, preferred_element_type=jnp.float32)
    logits = logits + lax.dot_general(wrl_ref[...], h_hi, _NT,
                                      preferred_element_type=jnp.float32)
    m = jnp.max(logits, axis=0, keepdims=True)
    e = jnp.exp(logits - m)
    aff_ref[...] = e / jnp.sum(e, axis=0, keepdims=True)


def _post_attn(out_a, out_b, x2d, g_a, g_b, w_out, g_ffn, wr_hi, wr_lo):
    n, d = x2d.shape
    tm = POST_TM
    row = lambda i: (i, 0)
    const = lambda i: (0, 0)
    return pl.pallas_call(
        _post_kernel,
        grid=(n // tm,),
        in_specs=[
            pl.BlockSpec((tm, QA_W), row),
            pl.BlockSpec((tm, QKVB_W), row),
            pl.BlockSpec((tm, d), row),
            pl.BlockSpec((1, QA_W), const),
            pl.BlockSpec((1, QKVB_W), const),
            pl.BlockSpec((QA_W + QKVB_W, d), const),
            pl.BlockSpec((1, d), const),
            pl.BlockSpec((N_EXPERTS, d), const),
            pl.BlockSpec((N_EXPERTS, d), const),
        ],
        out_specs=(
            pl.BlockSpec((tm, d), row),
            pl.BlockSpec((tm, d), row),
            pl.BlockSpec((N_EXPERTS, tm), lambda i: (0, i)),
        ),
        out_shape=(
            jax.ShapeDtypeStruct((n, d), jnp.float32),
            jax.ShapeDtypeStruct((n, d), jnp.bfloat16),
            jax.ShapeDtypeStruct((N_EXPERTS, n), jnp.float32),
        ),
        compiler_params=pltpu.CompilerParams(
            dimension_semantics=("arbitrary",),
            vmem_limit_bytes=_vmem_limit(40 * 1024 * 1024)),
        name="post_attn",
    )(out_a, out_b, x2d, g_a, g_b, w_out, g_ffn, wr_hi, wr_lo)


def _strict_upper(n):
    r = lax.broadcasted_iota(jnp.int32, (n, n), 0)
    c = lax.broadcasted_iota(jnp.int32, (n, n), 1)
    return jnp.where(r < c, 1.0, 0.0).astype(jnp.bfloat16)


def _route_kernel(cap, aff_ref, rel_ref, lo_ref):
    n = aff_ref.shape[1]
    nt = n // RT_T
    w = lo_ref.shape[1]
    cap_f = jnp.float32(cap)

    def count(mask):
        return jnp.sum(jnp.where(mask, 1.0, 0.0), axis=1, keepdims=True)

    def search(b, ans):
        cand = ans | jnp.left_shift(jnp.int32(1), 30 - b)
        bits = pltpu.bitcast(aff_ref[...], jnp.int32)
        return jnp.where(count(bits >= cand) >= cap_f, cand, ans)

    thr = lax.fori_loop(0, 31, search, jnp.zeros((N_EXPERTS, 1), jnp.int32))
    need = cap_f - count(pltpu.bitcast(aff_ref[...], jnp.int32) > thr)

    tri = _strict_upper(RT_T)
    lane = lax.broadcasted_iota(jnp.int32, (N_EXPERTS, w), 1)

    lo_ref[...] = jnp.zeros_like(lo_ref)

    def tile(c, carry):
        run_sel, run_eq = carry
        start = pl.multiple_of(c * RT_T, RT_T)
        bits = pltpu.bitcast(aff_ref[:, pl.ds(start, RT_T)], jnp.int32)
        gt = bits > thr
        eq = bits == thr
        eq_b = jnp.where(eq, 1.0, 0.0).astype(jnp.bfloat16)
        eq_rank = jnp.dot(eq_b, tri, preferred_element_type=jnp.float32)
        sel = gt | (eq & (run_eq + eq_rank < need))
        sel_b = jnp.where(sel, 1.0, 0.0).astype(jnp.bfloat16)
        rank = jnp.dot(sel_b, tri, preferred_element_type=jnp.float32)
        rel_ref[:, pl.ds(start, RT_T)] = jnp.where(sel, rank, -1.0).astype(jnp.int32)
        lo_ref[...] = jnp.where(lane == c, run_sel.astype(jnp.int32), lo_ref[...])
        return run_sel + count(sel), run_eq + count(eq)

    zero = need * 0.0
    run_sel, _ = lax.fori_loop(0, nt, tile, (zero, zero))
    lo_ref[...] = jnp.where(lane >= nt, run_sel.astype(jnp.int32), lo_ref[...])


def _route(aff_t, cap):
    e, n = aff_t.shape
    nt = n // RT_T
    w = nt + LANE
    full = lambda i: (0, 0)
    return pl.pallas_call(
        functools.partial(_route_kernel, cap),
        grid=(1,),
        in_specs=[pl.BlockSpec((e, n), full)],
        out_specs=(pl.BlockSpec((e, n), full), pl.BlockSpec((e, w), full)),
        out_shape=(jax.ShapeDtypeStruct((e, n), jnp.int32),
                   jax.ShapeDtypeStruct((e, w), jnp.int32)),
        compiler_params=pltpu.CompilerParams(
            dimension_semantics=("arbitrary",),
            vmem_limit_bytes=_vmem_limit(40 * 1024 * 1024)),
        name="route",
    )(aff_t)


def _pack_pairs(x):
    w = x.shape[1] // 2
    lo = pltpu.bitcast(x[:, :w], jnp.uint32)
    hi = pltpu.bitcast(x[:, w:], jnp.uint32)
    return lo | (hi >> 16)


def _unpack_pairs(p):
    lo = pltpu.bitcast(p & jnp.uint32(0xFFFF0000), jnp.float32).astype(jnp.bfloat16)
    hi = pltpu.bitcast(p << 16, jnp.float32).astype(jnp.bfloat16)
    return lo, hi


def _one_hot_rows(rel_ref, shift):
    kio = lax.broadcasted_iota(jnp.int32, (RT_CH, RT_T), 0)
    blocks = []
    for e in range(N_EXPERTS):
        hit = (rel_ref[e:e + 1, :] - shift) == kio
        blocks.append(jnp.where(hit, 1.0, 0.0).astype(jnp.bfloat16))
    return jnp.concatenate(blocks, axis=0)


def _dispatch_kernel(cap, cpad, nt, w, lo_ref, h_ref, rel_ref, xe_hbm,
                     stage, stage_x, sem, sem_x):
    j = pl.program_id(0)
    slot = j % 2

    def dst(e, jj, c):
        row = e * cpad + lo_ref[e * w + jj] + c * RT_CH
        return xe_hbm.at[pl.ds(row, RT_CH), 0]

    def chunk_copy(e, jj, sl):
        return pltpu.make_async_copy(
            stage.at[sl, pl.ds(e * RT_CH, RT_CH)], dst(e, jj, 0), sem.at[sl])

    @pl.when(j == 0)
    def _():
        pad = cpad - cap
        stage_x[...] = jnp.zeros_like(stage_x)
        fills = [pltpu.make_async_copy(
            stage_x.at[pl.ds(0, pad)], xe_hbm.at[pl.ds(e * cpad + cap, pad), 0], sem_x)
            for e in range(N_EXPERTS)]
        for f in fills:
            f.start()
        for f in fills:
            f.wait()

    x = jnp.dot(_one_hot_rows(rel_ref, 0), h_ref[...], preferred_element_type=jnp.float32)
    stage[slot] = _pack_pairs(x)

    @pl.when(j > 0)
    def _():
        for e in range(N_EXPERTS):
            chunk_copy(e, j - 1, 1 - slot).wait()

    for e in range(N_EXPERTS):
        chunk_copy(e, j, slot).start()

    cnts = [lo_ref[e * w + j + 1] - lo_ref[e * w + j] for e in range(N_EXPERTS)]
    most = functools.reduce(jnp.maximum, cnts)
    n_pass = jnp.right_shift(most + (RT_CH - 1), RT_CH.bit_length() - 1)

    def extra(c, carry):
        kio = lax.broadcasted_iota(jnp.int32, (RT_CH, RT_T), 0)

        def extra_copy(e):
            return pltpu.make_async_copy(
                stage_x.at[pl.ds(e * RT_CH, RT_CH)], dst(e, j, c), sem_x)

        for e in range(N_EXPERTS):
            @pl.when(cnts[e] > c * RT_CH)
            def _():
                hit = (rel_ref[e:e + 1, :] - c * RT_CH) == kio
                xx = jnp.dot(jnp.where(hit, 1.0, 0.0).astype(jnp.bfloat16), h_ref[...],
                             preferred_element_type=jnp.float32)
                stage_x[e * RT_CH:(e + 1) * RT_CH, :] = _pack_pairs(xx)
                extra_copy(e).start()
        for e in range(N_EXPERTS):
            @pl.when(cnts[e] > c * RT_CH)
            def _():
                extra_copy(e).wait()
        return carry

    lax.fori_loop(1, n_pass, extra, 0)

    @pl.when(j == nt - 1)
    def _():
        for e in range(N_EXPERTS):
            chunk_copy(e, j, slot).wait()


def _dispatch(h, rel_t, lo_flat, cap, cpad):
    n, d = h.shape
    nt = n // RT_T
    w = lo_flat.shape[0] // N_EXPERTS
    rows = N_EXPERTS * RT_CH
    assert cpad - cap <= rows
    return pl.pallas_call(
        functools.partial(_dispatch_kernel, cap, cpad, nt, w),
        grid_spec=pltpu.PrefetchScalarGridSpec(
            num_scalar_prefetch=1,
            grid=(nt,),
            in_specs=[
                pl.BlockSpec((RT_T, d), lambda j, lo: (j, 0)),
                pl.BlockSpec((N_EXPERTS, RT_T), lambda j, lo: (0, j)),
            ],
            out_specs=pl.BlockSpec(memory_space=pl.ANY),
            scratch_shapes=[
                pltpu.VMEM((2, rows, d // 2), jnp.uint32),
                pltpu.VMEM((rows, d // 2), jnp.uint32),
                pltpu.SemaphoreType.DMA((2,)),
                pltpu.SemaphoreType.DMA(()),
            ],
        ),
        out_shape=jax.ShapeDtypeStruct((N_EXPERTS * cpad, 1, d // 2), jnp.uint32),
        compiler_params=pltpu.CompilerParams(
            dimension_semantics=("arbitrary",),
            vmem_limit_bytes=_vmem_limit(40 * 1024 * 1024)),
        name="dispatch",
    )(lo_flat, h, rel_t)


def _ffn_kernel(cpad, tiles, tm, x_hbm, wg_ref, wu_ref, wd_ref, o_ref, xbuf, sem):
    i = pl.program_id(0)
    j = pl.program_id(1)
    step = i * tiles + j
    slot = step % 2

    def x_copy(ii, jj, sl):
        return pltpu.make_async_copy(
            x_hbm.at[pl.ds(ii * cpad + jj * tm, tm), 0], xbuf.at[sl], sem.at[sl])

    @pl.when(step == 0)
    def _():
        x_copy(0, 0, 0).start()

    @pl.when(step + 1 < pl.num_programs(0) * tiles)
    def _():
        wrap = j + 1 == tiles
        x_copy(jnp.where(wrap, i + 1, i), jnp.where(wrap, 0, j + 1), 1 - slot).start()

    x_copy(i, j, slot).wait()
    x_lo, x_hi = _unpack_pairs(xbuf[slot])
    half = x_lo.shape[1]
    d_ff = wg_ref.shape[2]
    acc = None
    for c in range(d_ff // FFN_TF):
        f = slice(c * FFN_TF, (c + 1) * FFN_TF)
        g = jnp.dot(x_lo, wg_ref[0, :half, f], preferred_element_type=jnp.float32)
        g = g + jnp.dot(x_hi, wg_ref[0, half:, f], preferred_element_type=jnp.float32)
        u = jnp.dot(x_lo, wu_ref[0, :half, f], preferred_element_type=jnp.float32)
        u = u + jnp.dot(x_hi, wu_ref[0, half:, f], preferred_element_type=jnp.float32)
        act = (g * jax.nn.sigmoid(g) * u).astype(jnp.bfloat16)
        part = jnp.dot(act, wd_ref[0, f, :], preferred_element_type=jnp.float32)
        acc = part if acc is None else acc + part
    o_ref[...] = acc.astype(o_ref.dtype)


def _expert_ffn(xe, w_gate, w_up, w_down, cap, cpad, tm):
    e, d, d_ff = w_gate.shape
    tiles = cap // tm
    return pl.pallas_call(
        functools.partial(_ffn_kernel, cpad, tiles, tm),
        grid=(e, tiles),
        in_specs=[
            pl.BlockSpec(memory_space=pl.ANY),
            pl.BlockSpec((1, d, d_ff), lambda i, j: (i, 0, 0)),
            pl.BlockSpec((1, d, d_ff), lambda i, j: (i, 0, 0)),
            pl.BlockSpec((1, d_ff, d), lambda i, j: (i, 0, 0)),
        ],
        out_specs=pl.BlockSpec((tm, d), lambda i, j: (i * tiles + j, 0)),
        out_shape=jax.ShapeDtypeStruct((e * cap, d), jnp.bfloat16),
        scratch_shapes=[
            pltpu.VMEM((2, tm, d // 2), jnp.uint32),
            pltpu.SemaphoreType.DMA((2,)),
        ],
        compiler_params=pltpu.CompilerParams(
            dimension_semantics=("arbitrary", "arbitrary"),
            vmem_limit_bytes=_vmem_limit(56 * 1024 * 1024)),
        name="expert_ffn",
    )(xe, w_gate, w_up, w_down)


def _combine_kernel(cap, nt, w, lo_ref, x_ref, rel_ref, aff_ref, ye_hbm, o_ref,
                    ybuf, ybuf_x, sem, sem_x):
    j = pl.program_id(0)
    slot = j % 2
    last_start = N_EXPERTS * cap - RT_CH
    align = 16

    def start_row(e, jj, c):
        lo = lo_ref[e * w + jj]
        a = e * cap + lo - (lo & (align - 1)) + c * RT_CH
        return pl.multiple_of(jnp.minimum(a, last_start), align)

    def fetch(e, jj, sl):
        return pltpu.make_async_copy(
            ye_hbm.at[pl.ds(start_row(e, jj, 0), RT_CH)],
            ybuf.at[sl, pl.ds(e * RT_CH, RT_CH)], sem.at[sl])

    @pl.when(j == 0)
    def _():
        for e in range(N_EXPERTS):
            fetch(e, 0, 0).start()

    @pl.when(j + 1 < nt)
    def _():
        for e in range(N_EXPERTS):
            fetch(e, j + 1, 1 - slot).start()

    los =[lo_ref[e * w + j] for e in range(N_EXPERTS)]
    cnts = [lo_ref[e * w + j + 1] - los[e] for e in range(N_EXPERTS)]
    lead = [los[e] & (align - 1) for e in range(N_EXPERTS)]

    def weight_block(e, c):
        kio = lax.broadcasted_iota(jnp.int32, (RT_CH, RT_T), 0)
        r = rel_ref[e:e + 1, :]
        p = r + lead[e]
        member = (r >= 0) & (p >= c * RT_CH) & (p < (c + 1) * RT_CH)
        off = e * cap + los[e] - start_row(e, j, c)
        hit = member & ((r + off) == kio)
        return jnp.where(hit, aff_ref[e:e + 1, :], 0.0).astype(jnp.bfloat16)

    tn = (((0,), (0,)), ((), ()))
    wt0 = jnp.concatenate([weight_block(e, 0) for e in range(N_EXPERTS)], axis=0)
    for e in range(N_EXPERTS):
        fetch(e, j, slot).wait()
    o_ref[...] = x_ref[...] + lax.dot_general(
        wt0, ybuf[slot], tn, preferred_element_type=jnp.float32)

    spans = [lead[e] + cnts[e] for e in range(N_EXPERTS)]
    most = functools.reduce(jnp.maximum, spans)
    n_pass = jnp.right_shift(most + (RT_CH - 1), RT_CH.bit_length() - 1)

    def extra(c, carry):
        def extra_fetch(e):
            return pltpu.make_async_copy(
                ye_hbm.at[pl.ds(start_row(e, j, c), RT_CH)],
                ybuf_x.at[pl.ds(e * RT_CH, RT_CH)], sem_x)

        for e in range(N_EXPERTS):
            @pl.when(spans[e] > c * RT_CH)
            def _():
                extra_fetch(e).start()
        for e in range(N_EXPERTS):
            @pl.when(spans[e] > c * RT_CH)
            def _():
                extra_fetch(e).wait()
        for e in range(N_EXPERTS):
            @pl.when(spans[e] > c * RT_CH)
            def _():
                wte = weight_block(e, c)
                o_ref[...] += lax.dot_general(
                    wte, ybuf_x[e * RT_CH:(e + 1) * RT_CH, :], tn,
                    preferred_element_type=jnp.float32)
        return carry

    lax.fori_loop(1, n_pass, extra, 0)


def _combine(x1, rel_t, aff_t, ye, lo_flat, cap):
    n, d = x1.shape
    nt = n // RT_T
    w = lo_flat.shape[0] // N_EXPERTS
    rows = N_EXPERTS * RT_CH
    return pl.pallas_call(
        functools.partial(_combine_kernel, cap, nt, w),
        grid_spec=pltpu.PrefetchScalarGridSpec(
            num_scalar_prefetch=1,
            grid=(nt,),
            in_specs=[
                pl.BlockSpec((RT_T, d), lambda j, lo: (j, 0)),
                pl.BlockSpec((N_EXPERTS, RT_T), lambda j, lo: (0, j)),
                pl.BlockSpec((N_EXPERTS, RT_T), lambda j, lo: (0, j)),
                pl.BlockSpec(memory_space=pl.ANY),
            ],
            out_specs=pl.BlockSpec((RT_T, d), lambda j, lo: (j, 0)),
            scratch_shapes=[
                pltpu.VMEM((2, rows, d), jnp.bfloat16),
                pltpu.VMEM((rows, d), jnp.bfloat16),
                pltpu.SemaphoreType.DMA((2,)),
                pltpu.SemaphoreType.DMA(()),
            ],
        ),
        out_shape=jax.ShapeDtypeStruct((n, d), jnp.float32),
        compiler_params=pltpu.CompilerParams(
            dimension_semantics=("arbitrary",),
            vmem_limit_bytes=_vmem_limit(40 * 1024 * 1024)),
        name="combine",
    )(lo_flat, x1, rel_t, aff_t, ye)


def _window_bias_t():
    j = jnp.arange(3 * WIN_SUB)[:, None]
    i = jnp.arange(WIN_SUB)[None, :]
    dist = jnp.abs(i + WIN_SUB - j).astype(jnp.float32)
    slopes = jnp.exp2(-8.0 * (jnp.arange(N_HEADS_A, dtype=jnp.float32) + 1.0) / N_HEADS_A)
    b = jnp.where(dist[None] <= WINDOW, -(slopes[:, None, None] * dist[None]), NEG)
    return jnp.transpose(b, (1, 0, 2)).reshape(3 * WIN_SUB, N_HEADS_A * WIN_SUB)


def _na_bias_t(rpb):
    kk = jnp.arange(NA_KEY_ROWS)[:, None]
    rho = jnp.arange(NA_GROUP_ROWS)[None, :]
    rel = kk - NA_GROUP_ROWS
    r0 = jnp.stack([
        jnp.zeros_like(rho),
        rho - NA_ROWS // 2,
        jnp.full_like(rho, NA_GROUP_ROWS - NA_ROWS),
    ])
    row_ok = (rel[None] >= r0) & (rel[None] < r0 + NA_ROWS)
    dr = jnp.clip(rel - rho + (NA_ROWS - 1), 0, 2 * NA_ROWS - 2)
    ck = jnp.arange(GRID_W)[:, None]
    cq = jnp.arange(GRID_W)[None, :]
    c0 = jnp.clip(cq - NA_COLS // 2, 0, GRID_W - NA_COLS)
    col_ok = (ck >= c0) & (ck < c0 + NA_COLS)
    dc = jnp.clip(ck - cq + (NA_COLS - 1), 0, 2 * NA_COLS - 2)
    hi = lax.Precision.HIGHEST
    oh_r = (dr[:, :, None] == jnp.arange(2 * NA_ROWS - 1)).astype(jnp.float32)
    oh_c = (dc[None] == jnp.arange(2 * NA_COLS - 1)[:, None, None]).astype(jnp.float32)
    rows = jnp.einsum('krs,hsd->hkrd', oh_r, rpb.astype(jnp.float32), precision=hi)
    vals = jnp.einsum('hkrd,dcq->hkcrq', rows, oh_c, precision=hi)
    ok = row_ok[:, :, None, :, None] & col_ok[None, None, :, None, :]
    b = jnp.where(ok[:, None], vals[None], NEG)
    h = rpb.shape[0]
    return b.reshape(3, h, NA_KEY_ROWS * GRID_W, NA_TQ)


def _layer_params(p, l):
    scale = 1.0 / math.sqrt(HEAD_DIM)
    gains = jnp.concatenate([p["qnorm_a"][l] * scale, p["knorm_a"][l],
                             p["qnorm_b"][l] * scale, p["knorm_b"][l]])
    wr = p["w_router"][l].T
    wr_hi = wr.astype(jnp.bfloat16)
    return dict(
        g_mix=p["norm_mix"][l][None, :],
        w_in_t=p["w_in"][l].T.astype(jnp.bfloat16),
        head_gains=jnp.broadcast_to(gains[:, None], (4 * HEAD_DIM, PROJ_TM)),
        sink_row=jnp.repeat(p["sink_a"][l].astype(jnp.float32), WIN_SUB)[None, :],
        na_bias=_na_bias_t(p["rpb_b"][l]),
        g_a=p["onorm_a"][l][None, :],
        g_b=p["onorm_b"][l][None, :],
        w_out=p["w_out"][l].astype(jnp.bfloat16),
        g_ffn=p["norm_ffn"][l][None, :],
        wr_hi=wr_hi,
        wr_lo=(wr - wr_hi.astype(jnp.float32)).astype(jnp.bfloat16),
        w_gate=p["w_gate"][l].astype(jnp.bfloat16),
        w_up=p["w_up"][l].astype(jnp.bfloat16),
        w_down=p["w_down"][l].astype(jnp.bfloat16),
    )


def _trunk(x, layers, win_bias):
    b, s, d = x.shape
    n = b * s
    assert s % WIN_TQ == 0 and s // NA_TQ >= 3 and n % PROJ_TM == 0 and n % RT_T == 0
    cap = EC_CAPACITY * n // N_EXPERTS
    tm = min(FFN_TM, cap)
    assert cap % tm == 0 and tm % RT_CH == 0
    cpad = cap + tm
    x2 = x.reshape(n, d)
    for q in layers:
        qa_t, ka, va_t, qb_t, kb, vb_t = _in_proj(x2, q["g_mix"], q["w_in_t"], q["head_gains"])
        out_a = _window_attention(qa_t, ka, va_t, win_bias, q["sink_row"], s)
        out_b = _na_attention(qb_t, kb, vb_t, q["na_bias"], s)
        x1, h, aff_t = _post_attn(out_a, out_b, x2, q["g_a"], q["g_b"], q["w_out"],
                                  q["g_ffn"], q["wr_hi"], q["wr_lo"])
        rel_t, lo = _route(aff_t, cap)
        lo_flat = lo.reshape(-1)
        xe = _dispatch(h, rel_t, lo_flat, cap, cpad)
        ye = _expert_ffn(xe, q["w_gate"], q["w_up"], q["w_down"], cap, cpad, tm)
        x2 = _combine(x1, rel_t, aff_t, ye, lo_flat, cap)
    return x2.reshape(b, s, d)


def kernel(x_prompt, x_sample, norm_mix, w_in, qnorm_a, knorm_a, sink_a, qnorm_b, knorm_b,
           rpb_b, onorm_a, onorm_b, w_out, norm_ffn, w_router, w_gate, w_up, w_down):
    p = dict(norm_mix=norm_mix, w_in=w_in, qnorm_a=qnorm_a, knorm_a=knorm_a, sink_a=sink_a,
             qnorm_b=qnorm_b, knorm_b=knorm_b, rpb_b=rpb_b, onorm_a=onorm_a, onorm_b=onorm_b,
             w_out=w_out, norm_ffn=norm_ffn, w_router=w_router, w_gate=w_gate, w_up=w_up,
             w_down=w_down)
    layers = [_layer_params(p, l) for l in range(w_in.shape[0])]
    win_bias = _window_bias_t()
    return (_trunk(x_prompt, layers, win_bias), _trunk(x_sample, layers, win_bias))
```

```python
import functools
import math

import jax
import jax.numpy as jnp
from jax import lax
from jax.experimental import pallas as pl
from jax.experimental.pallas import tpu as pltpu

HEAD_DIM = 64
N_HEADS_A = 8
N_KV_HEADS_A = 2
N_HEADS_B = 8
QA_W = N_HEADS_A * HEAD_DIM
KVA_W = N_KV_HEADS_A * HEAD_DIM
QKVB_W = N_HEADS_B * HEAD_DIM
PROJ_W = QA_W + 2 * KVA_W + 3 * QKVB_W
WINDOW = 128
GRID_W = 64
NA_ROWS = 8
NA_COLS = 16
N_EXPERTS = 16
EC_CAPACITY = 2
EPS = 1e-6
NEG = -1e30
LOG2E = 1.4426950408889634

LANE = 128
V7X_VMEM_BYTES = 64 * 1024 * 1024

PROJ_TM = 512
WIN_TQ = 512
WIN_SUB = WINDOW
NA_GROUP_ROWS = 4
NA_TQ = NA_GROUP_ROWS * GRID_W
NA_KEY_ROWS = 3 * NA_GROUP_ROWS
NA_KC = 128
POST_TM = 512
FFN_TM = 512
FFN_TF = 512
RT_T = 256
RT_CH = 64

_NT = (((1,), (1,)), ((), ()))


def _vmem_limit(nbytes):
    return int(min(nbytes, V7X_VMEM_BYTES - 4 * 1024 * 1024))


def _proj_kernel(x_ref, g_ref, w_ref, hg_ref,
                 qa_ref, ka_ref, va_ref, qb_ref, kb_ref, vb_ref):
    x = x_ref[...]
    ms = jnp.mean(x * x, axis=-1, keepdims=True)
    h = (x * lax.rsqrt(ms + EPS) * g_ref[...]).astype(jnp.bfloat16)

    def seg(lo, hi):
        return lax.dot_general(w_ref[lo:hi, :], h, _NT,
                               preferred_element_type=jnp.float32)

    def head_norm(blk, gain):
        ssq = jnp.sum(blk * blk, axis=0, keepdims=True)
        return blk * lax.rsqrt(ssq * (1.0 / HEAD_DIM) + EPS) * gain

    g_qa = hg_ref[0 * HEAD_DIM:1 * HEAD_DIM, :]
    g_ka = hg_ref[1 * HEAD_DIM:2 * HEAD_DIM, :]
    g_qb = hg_ref[2 * HEAD_DIM:3 * HEAD_DIM, :]
    g_kb = hg_ref[3 * HEAD_DIM:4 * HEAD_DIM, :]

    o = 0
    p = seg(o, o + QA_W)
    for hd in range(N_HEADS_A):
        r = slice(hd * HEAD_DIM, (hd + 1) * HEAD_DIM)
        qa_ref[r, :] = head_norm(p[r, :], g_qa).astype(qa_ref.dtype)
    o += QA_W
    p = seg(o, o + 2 * KVA_W)
    kn = jnp.concatenate(
        [head_norm(p[hd * HEAD_DIM:(hd + 1) * HEAD_DIM, :], g_ka)
         for hd in range(N_KV_HEADS_A)], axis=0)
    ka_ref[...] = kn.T.astype(ka_ref.dtype)
    va_ref[...] = p[KVA_W:2 * KVA_W, :].astype(va_ref.dtype)
    o += 2 * KVA_W
    p = seg(o, o + QKVB_W)
    for hd in range(N_HEADS_B):
        r = slice(hd * HEAD_DIM, (hd + 1) * HEAD_DIM)
        qb_ref[r, :] = head_norm(p[r, :], g_qb).astype(qb_ref.dtype)
    o += QKVB_W
    p = seg(o, o + QKVB_W)
    kn = jnp.concatenate(
        [head_norm(p[hd * HEAD_DIM:(hd + 1) * HEAD_DIM, :], g_kb)
         for hd in range(N_HEADS_B)], axis=0)
    kb_ref[...] = kn.T.astype(kb_ref.dtype)
    o += QKVB_W
    vb_ref[...] = seg(o, o + QKVB_W).astype(vb_ref.dtype)


def _in_proj(x2d, g_mix, w_in_t, head_gains):
    n, d = x2d.shape
    tm = PROJ_TM
    bf = jnp.bfloat16
    col = lambda i: (0, i)
    row = lambda i: (i, 0)
    const = lambda i: (0, 0)
    out_shape = (
        jax.ShapeDtypeStruct((QA_W, n), bf),
        jax.ShapeDtypeStruct((n, KVA_W), bf),
        jax.ShapeDtypeStruct((KVA_W, n), bf),
        jax.ShapeDtypeStruct((QKVB_W, n), bf),
        jax.ShapeDtypeStruct((n, QKVB_W), bf),
        jax.ShapeDtypeStruct((QKVB_W, n), bf),
    )
    out_specs = (
        pl.BlockSpec((QA_W, tm), col),
        pl.BlockSpec((tm, KVA_W), row),
        pl.BlockSpec((KVA_W, tm), col),
        pl.BlockSpec((QKVB_W, tm), col),
        pl.BlockSpec((tm, QKVB_W), row),
        pl.BlockSpec((QKVB_W, tm), col),
    )
    return pl.pallas_call(
        _proj_kernel,
        grid=(n // tm,),
        in_specs=[
            pl.BlockSpec((tm, d), row),
            pl.BlockSpec((1, d), const),
            pl.BlockSpec((PROJ_W, d), const),
            pl.BlockSpec((4 * HEAD_DIM, tm), const),
        ],
        out_specs=out_specs,
        out_shape=out_shape,
        compiler_params=pltpu.CompilerParams(
            dimension_semantics=("arbitrary",),
            vmem_limit_bytes=_vmem_limit(48 * 1024 * 1024)),
        name="in_proj",
    )(x2d, g_mix, w_in_t, head_gains)


def _fold(acc, v, op):
    return v if acc is None else op(acc, v)


def _window_kernel(blocks_per_seq, q_ref, kp_ref, kc_ref, kn_ref,
                   vp_ref, vc_ref, vn_ref, bias_ref, sink_ref, o_ref, s_scr, p_scr):
    i = pl.program_id(0)
    pos = i % blocks_per_seq
    pen_prev = jnp.where(pos == 0, NEG, 0.0).astype(jnp.float32)
    pen_next = jnp.where(pos == blocks_per_seq - 1, NEG, 0.0).astype(jnp.float32)

    sink = sink_ref[...]
    n_sub = WIN_TQ // WIN_SUB
    n_chunks = 3
    gq = N_HEADS_A // N_KV_HEADS_A
    zero = jnp.zeros((HEAD_DIM, WIN_SUB), jnp.bfloat16)
    krefs = (kp_ref, kc_ref, kn_ref)

    def key_block(kb):
        if kb == 0:
            return 0, 0
        if kb == n_sub + 1:
            return 2, 0
        return 1, (kb - 1) * WIN_SUB

    def qblock(j):
        cols = slice(j * WIN_SUB, (j + 1) * WIN_SUB)
        halves = []
        for kv in range(N_KV_HEADS_A):
            parts = []
            for hd in range(N_HEADS_A):
                if hd // gq == kv:
                    parts.append(q_ref[hd * HEAD_DIM:(hd + 1) * HEAD_DIM, cols])
                else:
                    parts.append(zero)
            halves.append(jnp.concatenate(parts, axis=1))
        return jnp.concatenate(halves, axis=0)

    def score_chunk(j, c, qblk):
        rows = slice(c * WIN_SUB, (c + 1) * WIN_SUB)
        r, off = key_block(j + c)
        s = jnp.dot(krefs[r][off:off + WIN_SUB, :], qblk,
                    preferred_element_type=jnp.float32)
        s = s + bias_ref[rows, :]
        if j + c == 0:
            s = s + pen_prev
        if j + c == n_sub + 1:
            s = s + pen_next
        s_scr[j % 2, rows, :] = s
        return jnp.max(s, axis=0, keepdims=True)

    def prob_chunk(j, c, m):
        rows = slice(c * WIN_SUB, (c + 1) * WIN_SUB)
        p_scr[j % 2, rows, :] = jnp.exp2(s_scr[j % 2, rows, :] - m).astype(jnp.bfloat16)

    def finish(j, m):
        cols = slice(j * WIN_SUB, (j + 1) * WIN_SUB)
        vparts = []
        for c in range(n_chunks):
            r, off = key_block(j + c)
            vparts.append((vp_ref, vc_ref, vn_ref)[r][:, off:off + WIN_SUB])
        vwin = jnp.concatenate(vparts, axis=1)
        ones = jnp.ones((16, n_chunks * WIN_SUB), jnp.bfloat16)
        sink_term = jnp.exp2(sink - m)
        outs = []
        for kv in range(N_KV_HEADS_A):
            lanes = slice(kv * gq * WIN_SUB, (kv + 1) * gq * WIN_SUB)
            vt = jnp.concatenate([vwin[kv * HEAD_DIM:(kv + 1) * HEAD_DIM, :], ones], axis=0)
            o_t = jnp.dot(vt, p_scr[j % 2, :, lanes],
                          preferred_element_type=jnp.float32)
            o_t = o_t[:HEAD_DIM] / (o_t[HEAD_DIM:HEAD_DIM + 1] + sink_term[:, lanes])
            for g in range(gq):
                outs.append(o_t[:, g * WIN_SUB:(g + 1) * WIN_SUB])
        for a in range(N_HEADS_A // 2):
            pair = jnp.concatenate([outs[2 * a], outs[2 * a + 1]], axis=0)
            o_ref[cols, a * LANE:(a + 1) * LANE] = pair.T.astype(o_ref.dtype)

    qb = qblock(0)
    m = None
    for c in range(n_chunks):
        m = _fold(m, score_chunk(0, c, qb), jnp.maximum)
    m = jnp.maximum(m, sink)
    m_done = None
    for j in range(n_sub):
        m_next = None
        if j + 1 < n_sub:
            qb = qblock(j + 1)
        for c in range(n_chunks):
            prob_chunk(j, c, m)
            if j + 1 < n_sub:
                m_next = _fold(m_next, score_chunk(j + 1, c, qb), jnp.maximum)
            if c == 0 and j >= 1:
                finish(j - 1, m_done)
        m_done = m
        if j + 1 < n_sub:
            m = jnp.maximum(m_next, sink)
    finish(n_sub - 1, m_done)


def _window_attention(qa_t, ka, va_t, bias_t, sink_row, seq_len):
    n = ka.shape[0]
    nblk = n // WIN_TQ
    bps = seq_len // WIN_TQ
    r = WIN_TQ // WIN_SUB
    nsub = n // WIN_SUB
    prev_i = lambda i: jnp.maximum(r * i - 1, 0)
    next_i = lambda i: jnp.minimum(r * i + r, nsub - 1)
    const = lambda i: (0, 0)
    return pl.pallas_call(
        functools.partial(_window_kernel, bps),
        grid=(nblk,),
        in_specs=[
            pl.BlockSpec((QA_W, WIN_TQ), lambda i: (0, i)),
            pl.BlockSpec((WIN_SUB, KVA_W), lambda i: (prev_i(i), 0)),
            pl.BlockSpec((WIN_TQ, KVA_W), lambda i: (i, 0)),
            pl.BlockSpec((WIN_SUB, KVA_W), lambda i: (next_i(i), 0)),
            pl.BlockSpec((KVA_W, WIN_SUB), lambda i: (0, prev_i(i))),
            pl.BlockSpec((KVA_W, WIN_TQ), lambda i: (0, i)),
            pl.BlockSpec((KVA_W, WIN_SUB), lambda i: (0, next_i(i))),
            pl.BlockSpec((3 * WIN_SUB, N_HEADS_A * WIN_SUB), const),
            pl.BlockSpec((1, N_HEADS_A * WIN_SUB), const),
        ],
        out_specs=pl.BlockSpec((WIN_TQ, QA_W), lambda i: (i, 0)),
        out_shape=jax.ShapeDtypeStruct((n, QA_W), jnp.bfloat16),
        scratch_shapes=[
            pltpu.VMEM((2, 3 * WIN_SUB, N_HEADS_A * WIN_SUB), jnp.float32),
            pltpu.VMEM((2, 3 * WIN_SUB, N_HEADS_A * WIN_SUB), jnp.bfloat16),
        ],
        compiler_params=pltpu.CompilerParams(
            dimension_semantics=("arbitrary",),
            vmem_limit_bytes=_vmem_limit(40 * 1024 * 1024)),
        name="window_attn",
    )(qa_t, ka, ka, ka, va_t, va_t, va_t, bias_t, sink_row)


def _na_kernel(q_ref, kp_ref, kc_ref, kn_ref, vp_ref, vc_ref, vn_ref,
               bias_ref, o_ref, s_scr, p_scr):
    zero = jnp.zeros((HEAD_DIM, NA_TQ), jnp.bfloat16)
    n_keys = NA_KEY_ROWS * GRID_W
    n_chunks = n_keys // NA_KC
    n_pairs = N_HEADS_B // 2
    krefs = (kp_ref, kc_ref, kn_ref)

    def qblock(pr):
        q0 = q_ref[(2 * pr) * HEAD_DIM:(2 * pr + 1) * HEAD_DIM, :]
        q1 = q_ref[(2 * pr + 1) * HEAD_DIM:(2 * pr + 2) * HEAD_DIM, :]
        return jnp.concatenate(
            [jnp.concatenate([q0, zero], axis=1),
             jnp.concatenate([zero, q1], axis=1)], axis=0)

    def score_chunk(pr, c, qblk):
        rows = slice(c * NA_KC, (c + 1) * NA_KC)
        blk, off = divmod(c * NA_KC, NA_TQ)
        kchunk = krefs[blk][off:off + NA_KC, pr * LANE:(pr + 1) * LANE]
        s = jnp.dot(kchunk, qblk, preferred_element_type=jnp.float32)
        s = s + bias_ref[0, pr, rows, :]
        s_scr[pr % 2, rows, :] = s
        return jnp.max(s, axis=0, keepdims=True)

    def prob_chunk(pr, c, m):
        rows = slice(c * NA_KC, (c + 1) * NA_KC)
        p_scr[pr % 2, rows, :] = jnp.exp2(s_scr[pr % 2, rows, :] - m).astype(jnp.bfloat16)

    def finish(pr):
        lanes = slice(pr * LANE, (pr + 1) * LANE)
        vwin = jnp.concatenate(
            [vp_ref[lanes, :], vc_ref[lanes, :], vn_ref[lanes, :]], axis=1)
        ones = jnp.ones((16, n_keys), jnp.bfloat16)
        outs = []
        for t in range(2):
            cols = slice(t * NA_TQ, (t + 1) * NA_TQ)
            vt = jnp.concatenate([vwin[t * HEAD_DIM:(t + 1) * HEAD_DIM, :], ones], axis=0)
            o_t = jnp.dot(vt, p_scr[pr % 2, :, cols],
                          preferred_element_type=jnp.float32)
            outs.append(o_t[:HEAD_DIM] / o_t[HEAD_DIM:HEAD_DIM + 1])
        pair = jnp.concatenate(outs, axis=0)
        o_ref[:, lanes] = pair.T.astype(o_ref.dtype)

    qb = qblock(0)
    m = None
    for c in range(n_chunks):
        m = _fold(m, score_chunk(0, c, qb), jnp.maximum)
    for pr in range(n_pairs):
        m_next = None
        if pr + 1 < n_pairs:
            qb = qblock(pr + 1)
        for c in range(n_chunks):
            prob_chunk(pr, c, m)
            if pr + 1 < n_pairs:
                m_next = _fold(m_next, score_chunk(pr + 1, c, qb), jnp.maximum)
            if c == 0 and pr >= 1:
                finish(pr - 1)
        m = m_next
    finish(n_pairs - 1)


def _na_attention(qb_t, kb, vb_t, bias, seq_len):
    n = kb.shape[0]
    ng = n // NA_TQ
    gps = seq_len // NA_TQ
    prev_i = lambda g: jnp.maximum(g - 1, 0)
    next_i = lambda g: jnp.minimum(g + 1, ng - 1)

    def variant(g):
        pos = g % gps
        return jnp.where(pos == 0, 0, jnp.where(pos == gps - 1, 2, 1))

    return pl.pallas_call(
        _na_kernel,
        grid=(ng,),
        in_specs=[
            pl.BlockSpec((QKVB_W, NA_TQ), lambda g: (0, g)),
            pl.BlockSpec((NA_TQ, QKVB_W), lambda g: (prev_i(g), 0)),
            pl.BlockSpec((NA_TQ, QKVB_W), lambda g: (g, 0)),
            pl.BlockSpec((NA_TQ, QKVB_W), lambda g: (next_i(g), 0)),
            pl.BlockSpec((QKVB_W, NA_TQ), lambda g: (0, prev_i(g))),
            pl.BlockSpec((QKVB_W, NA_TQ), lambda g: (0, g)),
            pl.BlockSpec((QKVB_W, NA_TQ), lambda g: (0, next_i(g))),
            pl.BlockSpec((1, N_HEADS_B // 2, NA_KEY_ROWS * GRID_W, 2 * NA_TQ),
                         lambda g: (variant(g), 0, 0, 0)),
        ],
        out_specs=pl.BlockSpec((NA_TQ, QKVB_W), lambda g: (g, 0)),
        out_shape=jax.ShapeDtypeStruct((n, QKVB_W), jnp.bfloat16),
        scratch_shapes=[
            pltpu.VMEM((2, NA_KEY_ROWS * GRID_W, 2 * NA_TQ), jnp.float32),
            pltpu.VMEM((2, NA_KEY_ROWS * GRID_W, 2 * NA_TQ), jnp.bfloat16),
        ],
        compiler_params=pltpu.CompilerParams(
            dimension_semantics=("arbitrary",),
            vmem_limit_bytes=_vmem_limit(48 * 1024 * 1024)),
        name="na_attn",
    )(qb_t, kb, kb, kb, vb_t, vb_t, vb_t, bias)


def _post_kernel(a_ref, b_ref, x_ref, ga_ref, gb_ref, w_ref, gf_ref,
                 wrh_ref, wrl_ref, x1_ref, h_ref, aff_ref):
    def rms(v, g):
        ms = jnp.mean(v * v, axis=-1, keepdims=True)
        return v * lax.rsqrt(ms + EPS) * g

    an = rms(a_ref[...].astype(jnp.float32), ga_ref[...]).astype(jnp.bfloat16)
    bn = rms(b_ref[...].astype(jnp.float32), gb_ref[...]).astype(jnp.bfloat16)
    y = jnp.dot(an, w_ref[:QA_W, :], preferred_element_type=jnp.float32)
    y = y + jnp.dot(bn, w_ref[QA_W:, :], preferred_element_type=jnp.float32)
    x1 = x_ref[...] + y
    x1_ref[...] = x1
    h = rms(x1, gf_ref[...])
    h_hi = h.astype(jnp.bfloat16)
    h_lo = (h - h_hi.astype(jnp.float32)).astype(jnp.bfloat16)
    h_ref[...] = h_hi
    wrh = wrh_ref[...]
    logits = lax.dot_general(wrh, h_hi, _NT, preferred_element_type=jnp.float32)
    logits = logits + lax.dot_general(wrh, h_lo, _NT, preferred_element_type=jnp.float32)
    logits = logits + lax.dot_general(wrl_ref[...], h_hi, _NT,
                                      preferred_element_type=jnp.float32)
    m = jnp.max(logits, axis=0, keepdims=True)
    e = jnp.exp(logits - m)
    aff_ref[...] = e / jnp.sum(e, axis=0, keepdims=True)


def _post_attn(out_a, out_b, x2d, g_a, g_b, w_out, g_ffn, wr_hi, wr_lo):
    n, d = x2d.shape
    tm = POST_TM
    row = lambda i: (i, 0)
    const = lambda i: (0, 0)
    return pl.pallas_call(
        _post_kernel,
        grid=(n // tm,),
        in_specs=[
            pl.BlockSpec((tm, QA_W), row),
            pl.BlockSpec((tm, QKVB_W), row),
            pl.BlockSpec((tm, d), row),
            pl.BlockSpec((1, QA_W), const),
            pl.BlockSpec((1, QKVB_W), const),
            pl.BlockSpec((QA_W + QKVB_W, d), const),
            pl.BlockSpec((1, d), const),
            pl.BlockSpec((N_EXPERTS, d), const),
            pl.BlockSpec((N_EXPERTS, d), const),
        ],
        out_specs=(
            pl.BlockSpec((tm, d), row),
            pl.BlockSpec((tm, d), row),
            pl.BlockSpec((N_EXPERTS, tm), lambda i: (0, i)),
        ),
        out_shape=(
            jax.ShapeDtypeStruct((n, d), jnp.float32),
            jax.ShapeDtypeStruct((n, d), jnp.bfloat16),
            jax.ShapeDtypeStruct((N_EXPERTS, n), jnp.float32),
        ),
        compiler_params=pltpu.CompilerParams(
            dimension_semantics=("arbitrary",),
            vmem_limit_bytes=_vmem_limit(40 * 1024 * 1024)),
        name="post_attn",
    )(out_a, out_b, x2d, g_a, g_b, w_out, g_ffn, wr_hi, wr_lo)


def _strict_upper(n):
    r = lax.broadcasted_iota(jnp.int32, (n, n), 0)
    c = lax.broadcasted_iota(jnp.int32, (n, n), 1)
    return jnp.where(r < c, 1.0, 0.0).astype(jnp.bfloat16)


def _route_kernel(cap, aff_ref, rel_ref, lo_ref):
    n = aff_ref.shape[1]
    nt = n // RT_T
    w = lo_ref.shape[1]
    cap_f = jnp.float32(cap)

    def count(mask):
        return jnp.sum(jnp.where(mask, 1.0, 0.0), axis=1, keepdims=True)

    def search(b, ans):
        cand = ans | jnp.left_shift(jnp.int32(1), 30 - b)
        bits = pltpu.bitcast(aff_ref[...], jnp.int32)
        return jnp.where(count(bits >= cand) >= cap_f, cand, ans)

    thr = lax.fori_loop(0, 31, search, jnp.zeros((N_EXPERTS, 1), jnp.int32))
    need = cap_f - count(pltpu.bitcast(aff_ref[...], jnp.int32) > thr)

    tri = _strict_upper(RT_T)
    lane = lax.broadcasted_iota(jnp.int32, (N_EXPERTS, w), 1)

    lo_ref[...] = jnp.zeros_like(lo_ref)

    def tile(c, carry):
        run_sel, run_eq = carry
        start = pl.multiple_of(c * RT_T, RT_T)
        bits = pltpu.bitcast(aff_ref[:, pl.ds(start, RT_T)], jnp.int32)
        gt = bits > thr
        eq = bits == thr
        eq_b = jnp.where(eq, 1.0, 0.0).astype(jnp.bfloat16)
        eq_rank = jnp.dot(eq_b, tri, preferred_element_type=jnp.float32)
        sel = gt | (eq & (run_eq + eq_rank < need))
        sel_b = jnp.where(sel, 1.0, 0.0).astype(jnp.bfloat16)
        rank = jnp.dot(sel_b, tri, preferred_element_type=jnp.float32)
        rel_ref[:, pl.ds(start, RT_T)] = jnp.where(sel, rank, -1.0).astype(jnp.int32)
        lo_ref[...] = jnp.where(lane == c, run_sel.astype(jnp.int32), lo_ref[...])
        return run_sel + count(sel), run_eq + count(eq)

    zero = need * 0.0
    run_sel, _ = lax.fori_loop(0, nt, tile, (zero, zero))
    lo_ref[...] = jnp.where(lane >= nt, run_sel.astype(jnp.int32), lo_ref[...])


def _route(aff_t, cap):
    e, n = aff_t.shape
    nt = n // RT_T
    w = nt + LANE
    full = lambda i: (0, 0)
    return pl.pallas_call(
        functools.partial(_route_kernel, cap),
        grid=(1,),
        in_specs=[pl.BlockSpec((e, n), full)],
        out_specs=(pl.BlockSpec((e, n), full), pl.BlockSpec((e, w), full)),
        out_shape=(jax.ShapeDtypeStruct((e, n), jnp.int32),
                   jax.ShapeDtypeStruct((e, w), jnp.int32)),
        compiler_params=pltpu.CompilerParams(
            dimension_semantics=("arbitrary",),
            vmem_limit_bytes=_vmem_limit(40 * 1024 * 1024)),
        name="route",
    )(aff_t)


def _pack_pairs(x):
    w = x.shape[1] // 2
    lo = pltpu.bitcast(x[:, :w], jnp.uint32)
    hi = pltpu.bitcast(x[:, w:], jnp.uint32)
    return lo | (hi >> 16)


def _unpack_pairs(p):
    lo = pltpu.bitcast(p & jnp.uint32(0xFFFF0000), jnp.float32).astype(jnp.bfloat16)
    hi = pltpu.bitcast(p << 16, jnp.float32).astype(jnp.bfloat16)
    return lo, hi


def _one_hot_rows(rel_ref, shift):
    kio = lax.broadcasted_iota(jnp.int32, (RT_CH, RT_T), 0)
    blocks = []
    for e in range(N_EXPERTS):
        hit = (rel_ref[e:e + 1, :] - shift) == kio
        blocks.append(jnp.where(hit, 1.0, 0.0).astype(jnp.bfloat16))
    return jnp.concatenate(blocks, axis=0)


def _dispatch_kernel(cap, cpad, nt, w, lo_ref, h_ref, rel_ref, xe_hbm,
                     stage, stage_x, sem, sem_x):
    j = pl.program_id(0)
    slot = j % 2

    def dst(e, jj, c):
        row = e * cpad + lo_ref[e * w + jj] + c * RT_CH
        return xe_hbm.at[pl.ds(row, RT_CH), 0]

    def chunk_copy(e, jj, sl):
        return pltpu.make_async_copy(
            stage.at[sl, pl.ds(e * RT_CH, RT_CH)], dst(e, jj, 0), sem.at[sl])

    @pl.when(j == 0)
    def _():
        pad = cpad - cap
        stage_x[...] = jnp.zeros_like(stage_x)
        fills = [pltpu.make_async_copy(
            stage_x.at[pl.ds(0, pad)], xe_hbm.at[pl.ds(e * cpad + cap, pad), 0], sem_x)
            for e in range(N_EXPERTS)]
        for f in fills:
            f.start()
        for f in fills:
            f.wait()

    x = jnp.dot(_one_hot_rows(rel_ref, 0), h_ref[...], preferred_element_type=jnp.float32)
    stage[slot] = _pack_pairs(x)

    @pl.when(j > 0)
    def _():
        for e in range(N_EXPERTS):
            chunk_copy(e, j - 1, 1 - slot).wait()

    for e in range(N_EXPERTS):
        chunk_copy(e, j, slot).start()

    cnts = [lo_ref[e * w + j + 1] - lo_ref[e * w + j] for e in range(N_EXPERTS)]
    most = functools.reduce(jnp.maximum, cnts)
    n_pass = jnp.right_shift(most + (RT_CH - 1), RT_CH.bit_length() - 1)

    def extra(c, carry):
        kio = lax.broadcasted_iota(jnp.int32, (RT_CH, RT_T), 0)

        def extra_copy(e):
            return pltpu.make_async_copy(
                stage_x.at[pl.ds(e * RT_CH, RT_CH)], dst(e, j, c), sem_x)

        for e in range(N_EXPERTS):
            @pl.when(cnts[e] > c * RT_CH)
            def _():
                hit = (rel_ref[e:e + 1, :] - c * RT_CH) == kio
                xx = jnp.dot(jnp.where(hit, 1.0, 0.0).astype(jnp.bfloat16), h_ref[...],
                             preferred_element_type=jnp.float32)
                stage_x[e * RT_CH:(e + 1) * RT_CH, :] = _pack_pairs(xx)
                extra_copy(e).start()
        for e in range(N_EXPERTS):
            @pl.when(cnts[e] > c * RT_CH)
            def _():
                extra_copy(e).wait()
        return carry

    lax.fori_loop(1, n_pass, extra, 0)

    @pl.when(j == nt - 1)
    def _():
        for e in range(N_EXPERTS):
            chunk_copy(e, j, slot).wait()


def _dispatch(h, rel_t, lo_flat, cap, cpad):
    n, d = h.shape
    nt = n // RT_T
    w = lo_flat.shape[0] // N_EXPERTS
    rows = N_EXPERTS * RT_CH
    assert cpad - cap <= rows
    return pl.pallas_call(
        functools.partial(_dispatch_kernel, cap, cpad, nt, w),
        grid_spec=pltpu.PrefetchScalarGridSpec(
            num_scalar_prefetch=1,
            grid=(nt,),
            in_specs=[
                pl.BlockSpec((RT_T, d), lambda j, lo: (j, 0)),
                pl.BlockSpec((N_EXPERTS, RT_T), lambda j, lo: (0, j)),
            ],
            out_specs=pl.BlockSpec(memory_space=pl.ANY),
            scratch_shapes=[
                pltpu.VMEM((2, rows, d // 2), jnp.uint32),
                pltpu.VMEM((rows, d // 2), jnp.uint32),
                pltpu.SemaphoreType.DMA((2,)),
                pltpu.SemaphoreType.DMA(()),
            ],
        ),
        out_shape=jax.ShapeDtypeStruct((N_EXPERTS * cpad, 1, d // 2), jnp.uint32),
        compiler_params=pltpu.CompilerParams(
            dimension_semantics=("arbitrary",),
            vmem_limit_bytes=_vmem_limit(40 * 1024 * 1024)),
        name="dispatch",
    )(lo_flat, h, rel_t)


def _ffn_kernel(cpad, tiles, tm, x_hbm, wg_ref, wu_ref, wd_ref, o_ref, xbuf, sem):
    i = pl.program_id(0)
    j = pl.program_id(1)
    step = i * tiles + j
    slot = step % 2

    def x_copy(ii, jj, sl):
        return pltpu.make_async_copy(
            x_hbm.at[pl.ds(ii * cpad + jj * tm, tm), 0], xbuf.at[sl], sem.at[sl])

    @pl.when(step == 0)
    def _():
        x_copy(0, 0, 0).start()

    @pl.when(step + 1 < pl.num_programs(0) * tiles)
    def _():
        wrap = j + 1 == tiles
        x_copy(jnp.where(wrap, i + 1, i), jnp.where(wrap, 0, j + 1), 1 - slot).start()

    x_copy(i, j, slot).wait()
    x_lo, x_hi = _unpack_pairs(xbuf[slot])
    half = x_lo.shape[1]
    d_ff = wg_ref.shape[2]
    acc = None
    for c in range(d_ff // FFN_TF):
        f = slice(c * FFN_TF, (c + 1) * FFN_TF)
        g = jnp.dot(x_lo, wg_ref[0, :half, f], preferred_element_type=jnp.float32)
        g = g + jnp.dot(x_hi, wg_ref[0, half:, f], preferred_element_type=jnp.float32)
        u = jnp.dot(x_lo, wu_ref[0, :half, f], preferred_element_type=jnp.float32)
        u = u + jnp.dot(x_hi, wu_ref[0, half:, f], preferred_element_type=jnp.float32)
        act = (g * jax.nn.sigmoid(g) * u).astype(jnp.bfloat16)
        part = jnp.dot(act, wd_ref[0, f, :], preferred_element_type=jnp.float32)
        acc = part if acc is None else acc + part
    o_ref[...] = acc.astype(o_ref.dtype)


def _expert_ffn(xe, w_gate, w_up, w_down, cap, cpad, tm):
    e, d, d_ff = w_gate.shape
    tiles = cap // tm
    return pl.pallas_call(
        functools.partial(_ffn_kernel, cpad, tiles, tm),
        grid=(e, tiles),
        in_specs=[
            pl.BlockSpec(memory_space=pl.ANY),
            pl.BlockSpec((1, d, d_ff), lambda i, j: (i, 0, 0)),
            pl.BlockSpec((1, d, d_ff), lambda i, j: (i, 0, 0)),
            pl.BlockSpec((1, d_ff, d), lambda i, j: (i, 0, 0)),
        ],
        out_specs=pl.BlockSpec((tm, d), lambda i, j: (i * tiles + j, 0)),
        out_shape=jax.ShapeDtypeStruct((e * cap, d), jnp.bfloat16),
        scratch_shapes=[
            pltpu.VMEM((2, tm, d // 2), jnp.uint32),
            pltpu.SemaphoreType.DMA((2,)),
        ],
        compiler_params=pltpu.CompilerParams(
            dimension_semantics=("arbitrary", "arbitrary"),
            vmem_limit_bytes=_vmem_limit(56 * 1024 * 1024)),
        name="expert_ffn",
    )(xe, w_gate, w_up, w_down)


def _combine_kernel(cap, nt, w, lo_ref, x_ref, rel_ref, aff_ref, ye_hbm, o_ref,
                    ybuf, ybuf_x, sem, sem_x):
    j = pl.program_id(0)
    slot = j % 2
    last_start = N_EXPERTS * cap - RT_CH
    align = 16

    def start_row(e, jj, c):
        lo = lo_ref[e * w + jj]
        a = e * cap + lo - (lo & (align - 1)) + c * RT_CH
        return pl.multiple_of(jnp.minimum(a, last_start), align)

    def fetch(e, jj, sl):
        return pltpu.make_async_copy(
            ye_hbm.at[pl.ds(start_row(e, jj, 0), RT_CH)],
            ybuf.at[sl, pl.ds(e * RT_CH, RT_CH)], sem.at[sl])

    @pl.when(j == 0)
    def _():
        for e in range(N_EXPERTS):
            fetch(e, 0, 0).start()

    @pl.when(j + 1 < nt)
    def _():
        for e in range(N_EXPERTS):
            fetch(e, j + 1, 1 - slot).start()

    los =[lo_ref[e * w + j] for e in range(N_EXPERTS)]
    cnts = [lo_ref[e * w + j + 1] - los[e] for e in range(N_EXPERTS)]
    lead = [los[e] & (align - 1) for e in range(N_EXPERTS)]

    def weight_block(e, c):
        kio = lax.broadcasted_iota(jnp.int32, (RT_CH, RT_T), 0)
        r = rel_ref[e:e + 1, :]
        p = r + lead[e]
        member = (r >= 0) & (p >= c * RT_CH) & (p < (c + 1) * RT_CH)
        off = e * cap + los[e] - start_row(e, j, c)
        hit = member & ((r + off) == kio)
        return jnp.where(hit, aff_ref[e:e + 1, :], 0.0).astype(jnp.bfloat16)

    tn = (((0,), (0,)), ((), ()))
    wt0 = jnp.concatenate([weight_block(e, 0) for e in range(N_EXPERTS)], axis=0)
    for e in range(N_EXPERTS):
        fetch(e, j, slot).wait()
    o_ref[...] = x_ref[...] + lax.dot_general(
        wt0, ybuf[slot], tn, preferred_element_type=jnp.float32)

    spans = [lead[e] + cnts[e] for e in range(N_EXPERTS)]
    most = functools.reduce(jnp.maximum, spans)
    n_pass = jnp.right_shift(most + (RT_CH - 1), RT_CH.bit_length() - 1)

    def extra(c, carry):
        def extra_fetch(e):
            return pltpu.make_async_copy(
                ye_hbm.at[pl.ds(start_row(e, j, c), RT_CH)],
                ybuf_x.at[pl.ds(e * RT_CH, RT_CH)], sem_x)

        for e in range(N_EXPERTS):
            @pl.when(spans[e] > c * RT_CH)
            def _():
                extra_fetch(e).start()
        for e in range(N_EXPERTS):
            @pl.when(spans[e] > c * RT_CH)
            def _():
                extra_fetch(e).wait()
        for e in range(N_EXPERTS):
            @pl.when(spans[e] > c * RT_CH)
            def _():
                wte = weight_block(e, c)
                o_ref[...] += lax.dot_general(
                    wte, ybuf_x[e * RT_CH:(e + 1) * RT_CH, :], tn,
                    preferred_element_type=jnp.float32)
        return carry

    lax.fori_loop(1, n_pass, extra, 0)


def _combine(x1, rel_t, aff_t, ye, lo_flat, cap):
    n, d = x1.shape
    nt = n // RT_T
    w = lo_flat.shape[0] // N_EXPERTS
    rows = N_EXPERTS * RT_CH
    return pl.pallas_call(
        functools.partial(_combine_kernel, cap, nt, w),
        grid_spec=pltpu.PrefetchScalarGridSpec(
            num_scalar_prefetch=1,
            grid=(nt,),
            in_specs=[
                pl.BlockSpec((RT_T, d), lambda j, lo: (j, 0)),
                pl.BlockSpec((N_EXPERTS, RT_T), lambda j, lo: (0, j)),
                pl.BlockSpec((N_EXPERTS, RT_T), lambda j, lo: (0, j)),
                pl.BlockSpec(memory_space=pl.ANY),
            ],
            out_specs=pl.BlockSpec((RT_T, d), lambda j, lo: (j, 0)),
            scratch_shapes=[
                pltpu.VMEM((2, rows, d), jnp.bfloat16),
                pltpu.VMEM((rows, d), jnp.bfloat16),
                pltpu.SemaphoreType.DMA((2,)),
                pltpu.SemaphoreType.DMA(()),
            ],
        ),
        out_shape=jax.ShapeDtypeStruct((n, d), jnp.float32),
        compiler_params=pltpu.CompilerParams(
            dimension_semantics=("arbitrary",),
            vmem_limit_bytes=_vmem_limit(40 * 1024 * 1024)),
        name="combine",
    )(lo_flat, x1, rel_t, aff_t, ye)


def _window_bias_t():
    j = jnp.arange(3 * WIN_SUB)[:, None]
    i = jnp.arange(WIN_SUB)[None, :]
    dist = jnp.abs(i + WIN_SUB - j).astype(jnp.float32)
    slopes = jnp.exp2(-8.0 * (jnp.arange(N_HEADS_A, dtype=jnp.float32) + 1.0) / N_HEADS_A)
    b = jnp.where(dist[None] <= WINDOW, -(slopes[:, None, None] * dist[None]), NEG)
    return jnp.transpose(b, (1, 0, 2)).reshape(3 * WIN_SUB, N_HEADS_A * WIN_SUB) * LOG2E


def _na_bias_t(rpb):
    kk = jnp.arange(NA_KEY_ROWS)[:, None]
    rho = jnp.arange(NA_GROUP_ROWS)[None, :]
    rel = kk - NA_GROUP_ROWS
    r0 = jnp.stack([
        jnp.zeros_like(rho),
        rho - NA_ROWS // 2,
        jnp.full_like(rho, NA_GROUP_ROWS - NA_ROWS),
    ])
    row_ok = (rel[None] >= r0) & (rel[None] < r0 + NA_ROWS)
    dr = jnp.clip(rel - rho + (NA_ROWS - 1), 0, 2 * NA_ROWS - 2)
    ck = jnp.arange(GRID_W)[:, None]
    cq = jnp.arange(GRID_W)[None, :]
    c0 = jnp.clip(cq - NA_COLS // 2, 0, GRID_W - NA_COLS)
    col_ok = (ck >= c0) & (ck < c0 + NA_COLS)
    dc = jnp.clip(ck - cq + (NA_COLS - 1), 0, 2 * NA_COLS - 2)
    hi = lax.Precision.HIGHEST
    oh_r = (dr[:, :, None] == jnp.arange(2 * NA_ROWS - 1)).astype(jnp.float32)
    oh_c = (dc[None] == jnp.arange(2 * NA_COLS - 1)[:, None, None]).astype(jnp.float32)
    rows = jnp.einsum('krs,hsd->hkrd', oh_r, rpb.astype(jnp.float32), precision=hi)
    vals = jnp.einsum('hkrd,dcq->hkcrq', rows, oh_c, precision=hi)
    ok = row_ok[:, :, None, :, None] & col_ok[None, None, :, None, :]
    b = jnp.where(ok[:, None], vals[None], NEG)
    h = rpb.shape[0]
    b = b.reshape(3, h // 2, 2, NA_KEY_ROWS * GRID_W, NA_TQ)
    b = jnp.transpose(b, (0, 1, 3, 2, 4)).reshape(3, h // 2, NA_KEY_ROWS * GRID_W, 2 * NA_TQ)
    return b * LOG2E


def _layer_params(p, l):
    scale = LOG2E / math.sqrt(HEAD_DIM)
    gains = jnp.concatenate([p["qnorm_a"][l] * scale, p["knorm_a"][l],
                             p["qnorm_b"][l] * scale, p["knorm_b"][l]])
    wr = p["w_router"][l].T
    wr_hi = wr.astype(jnp.bfloat16)
    return dict(
        g_mix=p["norm_mix"][l][None, :],
        w_in_t=p["w_in"][l].T.astype(jnp.bfloat16),
        head_gains=jnp.broadcast_to(gains[:, None], (4 * HEAD_DIM, PROJ_TM)),
        sink_row=jnp.repeat(p["sink_a"][l].astype(jnp.float32) * LOG2E, WIN_SUB)[None, :],
        na_bias=_na_bias_t(p["rpb_b"][l]),
        g_a=p["onorm_a"][l][None, :],
        g_b=p["onorm_b"][l][None, :],
        w_out=p["w_out"][l].astype(jnp.bfloat16),
        g_ffn=p["norm_ffn"][l][None, :],
        wr_hi=wr_hi,
        wr_lo=(wr - wr_hi.astype(jnp.float32)).astype(jnp.bfloat16),
        w_gate=p["w_gate"][l].astype(jnp.bfloat16),
        w_up=p["w_up"][l].astype(jnp.bfloat16),
        w_down=p["w_down"][l].astype(jnp.bfloat16),
    )


def _trunk(x, layers, win_bias):
    b, s, d = x.shape
    n = b * s
    assert s % WIN_TQ == 0 and s // NA_TQ >= 3 and n % PROJ_TM == 0 and n % RT_T == 0
    cap = EC_CAPACITY * n // N_EXPERTS
    tm = min(FFN_TM, cap)
    assert cap % tm == 0 and tm % RT_CH == 0
    cpad = cap + tm
    x2 = x.reshape(n, d)
    for q in layers:
        qa_t, ka, va_t, qb_t, kb, vb_t = _in_proj(x2, q["g_mix"], q["w_in_t"], q["head_gains"])
        out_a = _window_attention(qa_t, ka, va_t, win_bias, q["sink_row"], s)
        out_b = _na_attention(qb_t, kb, vb_t, q["na_bias"], s)
        x1, h, aff_t = _post_attn(out_a, out_b, x2, q["g_a"], q["g_b"], q["w_out"],
                                  q["g_ffn"], q["wr_hi"], q["wr_lo"])
        rel_t, lo = _route(aff_t, cap)
        lo_flat = lo.reshape(-1)
        xe = _dispatch(h, rel_t, lo_flat, cap, cpad)
        ye = _expert_ffn(xe, q["w_gate"], q["w_up"], q["w_down"], cap, cpad, tm)
        x2 = _combine(x1, rel_t, aff_t, ye, lo_flat, cap)
    return x2.reshape(b, s, d)


def kernel(x_prompt, x_sample, norm_mix, w_in, qnorm_a, knorm_a, sink_a, qnorm_b, knorm_b,
           rpb_b, onorm_a, onorm_b, w_out, norm_ffn, w_router, w_gate, w_up, w_down):
    p = dict(norm_mix=norm_mix, w_in=w_in, qnorm_a=qnorm_a, knorm_a=knorm_a, sink_a=sink_a,
             qnorm_b=qnorm_b, knorm_b=knorm_b, rpb_b=rpb_b, onorm_a=onorm_a, onorm_b=onorm_b,
             w_out=w_out, norm_ffn=norm_ffn, w_router=w_router, w_gate=w_gate, w_up=w_up,
             w_down=w_down)
    layers = [_layer_params(p, l) for l in range(w_in.shape[0])]
    win_bias = _window_bias_t()
    return (_trunk(x_prompt, layers, win_bias), _trunk(x_sample, layers, win_bias))
```

```python
import functools
import math

import jax
import jax.numpy as jnp
from jax import lax
from jax.experimental import pallas as pl
from jax.experimental.pallas import tpu as pltpu

HEAD_DIM = 64
N_HEADS_A = 8
N_KV_HEADS_A = 2
N_HEADS_B = 8
QA_W = N_HEADS_A * HEAD_DIM
KVA_W = N_KV_HEADS_A * HEAD_DIM
QKVB_W = N_HEADS_B * HEAD_DIM
PROJ_W = QA_W + 2 * KVA_W + 3 * QKVB_W
WINDOW = 128
GRID_W = 64
NA_ROWS = 8
NA_COLS = 16
N_EXPERTS = 16
EC_CAPACITY = 2
EPS = 1e-6
NEG = -1e30
LOG2E = 1.4426950408889634

LANE = 128
V7X_VMEM_BYTES = 64 * 1024 * 1024

PROJ_TM = 512
WIN_TQ = 512
WIN_SUB = WINDOW
NA_GROUP_ROWS = 4
NA_TQ = NA_GROUP_ROWS * GRID_W
NA_KEY_ROWS = 3 * NA_GROUP_ROWS
NA_KC = 128
POST_TM = 512
FFN_TM = 512
FFN_TF = 512
RT_T = 256
RT_CH = 64

_NT = (((1,), (1,)), ((), ()))


def _vmem_limit(nbytes):
    return int(min(nbytes, V7X_VMEM_BYTES - 4 * 1024 * 1024))


def _proj_kernel(x_ref, g_ref, w_ref, hg_ref,
                 qa_ref, ka_ref, va_ref, qb_ref, kb_ref, vb_ref):
    x = x_ref[...]
    ms = jnp.mean(x * x, axis=-1, keepdims=True)
    h = (x * lax.rsqrt(ms + EPS) * g_ref[...]).astype(jnp.bfloat16)

    def seg(lo, hi):
        return lax.dot_general(w_ref[lo:hi, :], h, _NT,
                               preferred_element_type=jnp.float32)

    def head_norm(blk, gain):
        ssq = jnp.sum(blk * blk, axis=0, keepdims=True)
        return blk * lax.rsqrt(ssq * (1.0 / HEAD_DIM) + EPS) * gain

    g_qa = hg_ref[0 * HEAD_DIM:1 * HEAD_DIM, :]
    g_ka = hg_ref[1 * HEAD_DIM:2 * HEAD_DIM, :]
    g_qb = hg_ref[2 * HEAD_DIM:3 * HEAD_DIM, :]
    g_kb = hg_ref[3 * HEAD_DIM:4 * HEAD_DIM, :]

    o = 0
    p = seg(o, o + QA_W)
    for hd in range(N_HEADS_A):
        r = slice(hd * HEAD_DIM, (hd + 1) * HEAD_DIM)
        qa_ref[r, :] = head_norm(p[r, :], g_qa).astype(qa_ref.dtype)
    o += QA_W
    p = seg(o, o + 2 * KVA_W)
    kn = jnp.concatenate(
        [head_norm(p[hd * HEAD_DIM:(hd + 1) * HEAD_DIM, :], g_ka)
         for hd in range(N_KV_HEADS_A)], axis=0)
    ka_ref[...] = kn.T.astype(ka_ref.dtype)
    va_ref[...] = p[KVA_W:2 * KVA_W, :].astype(va_ref.dtype)
    o += 2 * KVA_W
    p = seg(o, o + QKVB_W)
    for hd in range(N_HEADS_B):
        r = slice(hd * HEAD_DIM, (hd + 1) * HEAD_DIM)
        qb_ref[r, :] = head_norm(p[r, :], g_qb).astype(qb_ref.dtype)
    o += QKVB_W
    p = seg(o, o + QKVB_W)
    kn = jnp.concatenate(
        [head_norm(p[hd * HEAD_DIM:(hd + 1) * HEAD_DIM, :], g_kb)
         for hd in range(N_HEADS_B)], axis=0)
    kb_ref[...] = kn.T.astype(kb_ref.dtype)
    o += QKVB_W
    vb_ref[...] = seg(o, o + QKVB_W).astype(vb_ref.dtype)


def _in_proj(x2d, g_mix, w_in_t, head_gains):
    n, d = x2d.shape
    tm = PROJ_TM
    bf = jnp.bfloat16
    col = lambda i: (0, i)
    row = lambda i: (i, 0)
    const = lambda i: (0, 0)
    out_shape = (
        jax.ShapeDtypeStruct((QA_W, n), bf),
        jax.ShapeDtypeStruct((n, KVA_W), bf),
        jax.ShapeDtypeStruct((KVA_W, n), bf),
        jax.ShapeDtypeStruct((QKVB_W, n), bf),
        jax.ShapeDtypeStruct((n, QKVB_W), bf),
        jax.ShapeDtypeStruct((QKVB_W, n), bf),
    )
    out_specs = (
        pl.BlockSpec((QA_W, tm), col),
        pl.BlockSpec((tm, KVA_W), row),
        pl.BlockSpec((KVA_W, tm), col),
        pl.BlockSpec((QKVB_W, tm), col),
        pl.BlockSpec((tm, QKVB_W), row),
        pl.BlockSpec((QKVB_W, tm), col),
    )
    return pl.pallas_call(
        _proj_kernel,
        grid=(n // tm,),
        in_specs=[
            pl.BlockSpec((tm, d), row),
            pl.BlockSpec((1, d), const),
            pl.BlockSpec((PROJ_W, d), const),
            pl.BlockSpec((4 * HEAD_DIM, tm), const),
        ],
        out_specs=out_specs,
        out_shape=out_shape,
        compiler_params=pltpu.CompilerParams(
            dimension_semantics=("arbitrary",),
            vmem_limit_bytes=_vmem_limit(48 * 1024 * 1024)),
        name="in_proj",
    )(x2d, g_mix, w_in_t, head_gains)


def _fold(acc, v, op):
    return v if acc is None else op(acc, v)


def _window_kernel(blocks_per_seq, q_ref, kp_ref, kc_ref, kn_ref,
                   vp_ref, vc_ref, vn_ref, bias_ref, sink_ref, o_ref, s_scr, p_scr):
    i = pl.program_id(0)
    pos = i % blocks_per_seq
    pen_prev = jnp.where(pos == 0, NEG, 0.0).astype(jnp.float32)
    pen_next = jnp.where(pos == blocks_per_seq - 1, NEG, 0.0).astype(jnp.float32)

    sink = sink_ref[...]
    n_sub = WIN_TQ // WIN_SUB
    n_chunks = 3
    gq = N_HEADS_A // N_KV_HEADS_A
    zero = jnp.zeros((HEAD_DIM, WIN_SUB), jnp.bfloat16)
    krefs = (kp_ref, kc_ref, kn_ref)

    def key_block(kb):
        if kb == 0:
            return 0, 0
        if kb == n_sub + 1:
            return 2, 0
        return 1, (kb - 1) * WIN_SUB

    def qblock(j):
        cols = slice(j * WIN_SUB, (j + 1) * WIN_SUB)
        halves = []
        for kv in range(N_KV_HEADS_A):
            parts = []
            for hd in range(N_HEADS_A):
                if hd // gq == kv:
                    parts.append(q_ref[hd * HEAD_DIM:(hd + 1) * HEAD_DIM, cols])
                else:
                    parts.append(zero)
            halves.append(jnp.concatenate(parts, axis=1))
        return jnp.concatenate(halves, axis=0)

    def score_chunk(j, c, qblk):
        rows = slice(c * WIN_SUB, (c + 1) * WIN_SUB)
        r, off = key_block(j + c)
        s = jnp.dot(krefs[r][off:off + WIN_SUB, :], qblk,
                    preferred_element_type=jnp.float32)
        s = s + bias_ref[rows, :]
        if j + c == 0:
            s = s + pen_prev
        if j + c == n_sub + 1:
            s = s + pen_next
        s_scr[j % 2, rows, :] = s
        return jnp.max(s, axis=0, keepdims=True)

    def prob_chunk(j, c, m):
        rows = slice(c * WIN_SUB, (c + 1) * WIN_SUB)
        p_scr[j % 2, rows, :] = jnp.exp2(s_scr[j % 2, rows, :] - m).astype(jnp.bfloat16)

    def finish(j, m):
        cols = slice(j * WIN_SUB, (j + 1) * WIN_SUB)
        vparts = []
        for c in range(n_chunks):
            r, off = key_block(j + c)
            vparts.append((vp_ref, vc_ref, vn_ref)[r][:, off:off + WIN_SUB])
        vwin = jnp.concatenate(vparts, axis=1)
        ones = jnp.ones((16, n_chunks * WIN_SUB), jnp.bfloat16)
        sink_term = jnp.exp2(sink - m)
        outs = []
        for kv in range(N_KV_HEADS_A):
            lanes = slice(kv * gq * WIN_SUB, (kv + 1) * gq * WIN_SUB)
            vt = jnp.concatenate([vwin[kv * HEAD_DIM:(kv + 1) * HEAD_DIM, :], ones], axis=0)
            o_t = jnp.dot(vt, p_scr[j % 2, :, lanes],
                          preferred_element_type=jnp.float32)
            o_t = o_t[:HEAD_DIM] / (o_t[HEAD_DIM:HEAD_DIM + 1] + sink_term[:, lanes])
            for g in range(gq):
                outs.append(o_t[:, g * WIN_SUB:(g + 1) * WIN_SUB])
        for a in range(N_HEADS_A // 2):
            pair = jnp.concatenate([outs[2 * a], outs[2 * a + 1]], axis=0)
            o_ref[cols, a * LANE:(a + 1) * LANE] = pair.T.astype(o_ref.dtype)

    qb = qblock(0)
    m = None
    for c in range(n_chunks):
        m = _fold(m, score_chunk(0, c, qb), jnp.maximum)
    m = jnp.maximum(m, sink)
    m_done = None
    for j in range(n_sub):
        m_next = None
        if j + 1 < n_sub:
            qb = qblock(j + 1)
        for c in range(n_chunks):
            prob_chunk(j, c, m)
            if j + 1 < n_sub:
                m_next = _fold(m_next, score_chunk(j + 1, c, qb), jnp.maximum)
            if c == 0 and j >= 1:
                finish(j - 1, m_done)
        m_done = m
        if j + 1 < n_sub:
            m = jnp.maximum(m_next, sink)
    finish(n_sub - 1, m_done)


def _window_attention(qa_t, ka, va_t, bias_t, sink_row, seq_len):
    n = ka.shape[0]
    nblk = n // WIN_TQ
    bps = seq_len // WIN_TQ
    r = WIN_TQ // WIN_SUB
    nsub = n // WIN_SUB
    prev_i = lambda i: jnp.maximum(r * i - 1, 0)
    next_i = lambda i: jnp.minimum(r * i + r, nsub - 1)
    const = lambda i: (0, 0)
    return pl.pallas_call(
        functools.partial(_window_kernel, bps),
        grid=(nblk,),
        in_specs=[
            pl.BlockSpec((QA_W, WIN_TQ), lambda i: (0, i)),
            pl.BlockSpec((WIN_SUB, KVA_W), lambda i: (prev_i(i), 0)),
            pl.BlockSpec((WIN_TQ, KVA_W), lambda i: (i, 0)),
            pl.BlockSpec((WIN_SUB, KVA_W), lambda i: (next_i(i), 0)),
            pl.BlockSpec((KVA_W, WIN_SUB), lambda i: (0, prev_i(i))),
            pl.BlockSpec((KVA_W, WIN_TQ), lambda i: (0, i)),
            pl.BlockSpec((KVA_W, WIN_SUB), lambda i: (0, next_i(i))),
            pl.BlockSpec((3 * WIN_SUB, N_HEADS_A * WIN_SUB), const),
            pl.BlockSpec((1, N_HEADS_A * WIN_SUB), const),
        ],
        out_specs=pl.BlockSpec((WIN_TQ, QA_W), lambda i: (i, 0)),
        out_shape=jax.ShapeDtypeStruct((n, QA_W), jnp.bfloat16),
        scratch_shapes=[
            pltpu.VMEM((2, 3 * WIN_SUB, N_HEADS_A * WIN_SUB), jnp.float32),
            pltpu.VMEM((2, 3 * WIN_SUB, N_HEADS_A * WIN_SUB), jnp.bfloat16),
        ],
        compiler_params=pltpu.CompilerParams(
            dimension_semantics=("arbitrary",),
            vmem_limit_bytes=_vmem_limit(40 * 1024 * 1024)),
        name="window_attn",
    )(qa_t, ka, ka, ka, va_t, va_t, va_t, bias_t, sink_row)


def _na_kernel(q_ref, kp_ref, kc_ref, kn_ref, vp_ref, vc_ref, vn_ref,
               bias_ref, o_ref, s_scr, p_scr):
    zero = jnp.zeros((HEAD_DIM, NA_TQ), jnp.bfloat16)
    n_keys = NA_KEY_ROWS * GRID_W
    n_chunks = n_keys // NA_KC
    n_pairs = N_HEADS_B // 2
    krefs = (kp_ref, kc_ref, kn_ref)

    def qblock(pr):
        q0 = q_ref[(2 * pr) * HEAD_DIM:(2 * pr + 1) * HEAD_DIM, :]
        q1 = q_ref[(2 * pr + 1) * HEAD_DIM:(2 * pr + 2) * HEAD_DIM, :]
        return jnp.concatenate(
            [jnp.concatenate([q0, zero], axis=1),
             jnp.concatenate([zero, q1], axis=1)], axis=0)

    def score_chunk(pr, c, qblk):
        rows = slice(c * NA_KC, (c + 1) * NA_KC)
        blk, off = divmod(c * NA_KC, NA_TQ)
        kchunk = krefs[blk][off:off + NA_KC, pr * LANE:(pr + 1) * LANE]
        s = jnp.dot(kchunk, qblk, preferred_element_type=jnp.float32)
        s = s + bias_ref[0, pr, rows, :]
        s_scr[pr % 2, rows, :] = s
        return jnp.max(s, axis=0, keepdims=True)

    def prob_chunk(pr, c, m):
        rows = slice(c * NA_KC, (c + 1) * NA_KC)
        p_scr[pr % 2, rows, :] = jnp.exp2(s_scr[pr % 2, rows, :] - m).astype(jnp.bfloat16)

    def finish(pr):
        lanes = slice(pr * LANE, (pr + 1) * LANE)
        vwin = jnp.concatenate(
            [vp_ref[lanes, :], vc_ref[lanes, :], vn_ref[lanes, :]], axis=1)
        ones = jnp.ones((16, n_keys), jnp.bfloat16)
        outs = []
        for t in range(2):
            cols = slice(t * NA_TQ, (t + 1) * NA_TQ)
            vt = jnp.concatenate([vwin[t * HEAD_DIM:(t + 1) * HEAD_DIM, :], ones], axis=0)
            o_t = jnp.dot(vt, p_scr[pr % 2, :, cols],
                          preferred_element_type=jnp.float32)
            outs.append(o_t[:HEAD_DIM] / o_t[HEAD_DIM:HEAD_DIM + 1])
        pair = jnp.concatenate(outs, axis=0)
        o_ref[:, lanes] = pair.T.astype(o_ref.dtype)

    qb = qblock(0)
    m = None
    for c in range(n_chunks):
        m = _fold(m, score_chunk(0, c, qb), jnp.maximum)
    for pr in range(n_pairs):
        m_next = None
        if pr + 1 < n_pairs:
            qb = qblock(pr + 1)
        for c in range(n_chunks):
            prob_chunk(pr, c, m)
            if pr + 1 < n_pairs:
                m_next = _fold(m_next, score_chunk(pr + 1, c, qb), jnp.maximum)
            if c == 0 and pr >= 1:
                finish(pr - 1)
        m = m_next
    finish(n_pairs - 1)


def _na_attention(qb_t, kb, vb_t, bias, seq_len):
    n = kb.shape[0]
    ng = n // NA_TQ
    gps = seq_len // NA_TQ
    prev_i = lambda g: jnp.maximum(g - 1, 0)
    next_i = lambda g: jnp.minimum(g + 1, ng - 1)

    def variant(g):
        pos = g % gps
        return jnp.where(pos == 0, 0, jnp.where(pos == gps - 1, 2, 1))

    return pl.pallas_call(
        _na_kernel,
        grid=(ng,),
        in_specs=[
            pl.BlockSpec((QKVB_W, NA_TQ), lambda g: (0, g)),
            pl.BlockSpec((NA_TQ, QKVB_W), lambda g: (prev_i(g), 0)),
            pl.BlockSpec((NA_TQ, QKVB_W), lambda g: (g, 0)),
            pl.BlockSpec((NA_TQ, QKVB_W), lambda g: (next_i(g), 0)),
            pl.BlockSpec((QKVB_W, NA_TQ), lambda g: (0, prev_i(g))),
            pl.BlockSpec((QKVB_W, NA_TQ), lambda g: (0, g)),
            pl.BlockSpec((QKVB_W, NA_TQ), lambda g: (0, next_i(g))),
            pl.BlockSpec((1, N_HEADS_B // 2, NA_KEY_ROWS * GRID_W, 2 * NA_TQ),
                         lambda g: (variant(g), 0, 0, 0)),
        ],
        out_specs=pl.BlockSpec((NA_TQ, QKVB_W), lambda g: (g, 0)),
        out_shape=jax.ShapeDtypeStruct((n, QKVB_W), jnp.bfloat16),
        scratch_shapes=[
            pltpu.VMEM((2, NA_KEY_ROWS * GRID_W, 2 * NA_TQ), jnp.float32),
            pltpu.VMEM((2, NA_KEY_ROWS * GRID_W, 2 * NA_TQ), jnp.bfloat16),
        ],
        compiler_params=pltpu.CompilerParams(
            dimension_semantics=("arbitrary",),
            vmem_limit_bytes=_vmem_limit(48 * 1024 * 1024)),
        name="na_attn",
    )(qb_t, kb, kb, kb, vb_t, vb_t, vb_t, bias)


def _post_kernel(a_ref, b_ref, x_ref, ga_ref, gb_ref, w_ref, gf_ref,
                 wrh_ref, wrl_ref, x1_ref, h_ref, aff_ref):
    def rms(v, g):
        ms = jnp.mean(v * v, axis=-1, keepdims=True)
        return v * lax.rsqrt(ms + EPS) * g

    an = rms(a_ref[...].astype(jnp.float32), ga_ref[...]).astype(jnp.bfloat16)
    bn = rms(b_ref[...].astype(jnp.float32), gb_ref[...]).astype(jnp.bfloat16)
    y = jnp.dot(an, w_ref[:QA_W, :], preferred_element_type=jnp.float32)
    y = y + jnp.dot(bn, w_ref[QA_W:, :], preferred_element_type=jnp.float32)
    x1 = x_ref[...] + y
    x1_ref[...] = x1
    h = rms(x1, gf_ref[...])
    h_hi = h.astype(jnp.bfloat16)
    h_lo = (h - h_hi.astype(jnp.float32)).astype(jnp.bfloat16)
    h_ref[...] = h_hi
    wrh = wrh_ref[...]
    logits = lax.dot_general(wrh, h_hi, _NT, preferred_element_type=jnp.float32)
    logits = logits + lax.dot_general(wrh, h_lo, _NT, preferred_element_type=jnp.float32)
    logits = logits + lax.dot_general(wrl_ref[...], h_hi, _NT,
                                      preferred_element_type=jnp.float32)
    m = jnp.max(logits, axis=0, keepdims=True)
    e = jnp.exp(logits - m)
    aff_ref[...] = e / jnp.sum(e, axis=0, keepdims=True)


def _post_attn(out_a, out_b, x2d, g_a, g_b, w_out, g_ffn, wr_hi, wr_lo):
    n, d = x2d.shape
    tm = POST_TM
    row = lambda i: (i, 0)
    const = lambda i: (0, 0)
    return pl.pallas_call(
        _post_kernel,
        grid=(n // tm,),
        in_specs=[
            pl.BlockSpec((tm, QA_W), row),
            pl.BlockSpec((tm, QKVB_W), row),
            pl.BlockSpec((tm, d), row),
            pl.BlockSpec((1, QA_W), const),
            pl.BlockSpec((1, QKVB_W), const),
            pl.BlockSpec((QA_W + QKVB_W, d), const),
            pl.BlockSpec((1, d), const),
            pl.BlockSpec((N_EXPERTS, d), const),
            pl.BlockSpec((N_EXPERTS, d), const),
        ],
        out_specs=(
            pl.BlockSpec((tm, d), row),
            pl.BlockSpec((tm, d), row),
            pl.BlockSpec((N_EXPERTS, tm), lambda i: (0, i)),
        ),
        out_shape=(
            jax.ShapeDtypeStruct((n, d), jnp.float32),
            jax.ShapeDtypeStruct((n, d), jnp.bfloat16),
            jax.ShapeDtypeStruct((N_EXPERTS, n), jnp.float32),
        ),
        compiler_params=pltpu.CompilerParams(
            dimension_semantics=("arbitrary",),
            vmem_limit_bytes=_vmem_limit(40 * 1024 * 1024)),
        name="post_attn",
    )(out_a, out_b, x2d, g_a, g_b, w_out, g_ffn, wr_hi, wr_lo)


def _strict_upper(n):
    r = lax.broadcasted_iota(jnp.int32, (n, n), 0)
    c = lax.broadcasted_iota(jnp.int32, (n, n), 1)
    return jnp.where(r < c, 1.0, 0.0).astype(jnp.bfloat16)


def _route_kernel(cap, aff_ref, rel_ref, lo_ref):
    n = aff_ref.shape[1]
    nt = n // RT_T
    w = lo_ref.shape[1]
    cap_f = jnp.float32(cap)

    def count(mask):
        return jnp.sum(jnp.where(mask, 1.0, 0.0), axis=1, keepdims=True)

    def search(b, ans):
        cand = ans | jnp.left_shift(jnp.int32(1), 30 - b)
        bits = pltpu.bitcast(aff_ref[...], jnp.int32)
        return jnp.where(count(bits >= cand) >= cap_f, cand, ans)

    thr = lax.fori_loop(0, 31, search, jnp.zeros((N_EXPERTS, 1), jnp.int32))
    need = cap_f - count(pltpu.bitcast(aff_ref[...], jnp.int32) > thr)

    tri = _strict_upper(RT_T)
    lane = lax.broadcasted_iota(jnp.int32, (N_EXPERTS, w), 1)

    lo_ref[...] = jnp.zeros_like(lo_ref)

    def tile(c, carry):
        run_sel, run_eq = carry
        start = pl.multiple_of(c * RT_T, RT_T)
        bits = pltpu.bitcast(aff_ref[:, pl.ds(start, RT_T)], jnp.int32)
        gt = bits > thr
        eq = bits == thr
        eq_b = jnp.where(eq, 1.0, 0.0).astype(jnp.bfloat16)
        eq_rank = jnp.dot(eq_b, tri, preferred_element_type=jnp.float32)
        sel = gt | (eq & (run_eq + eq_rank < need))
        sel_b = jnp.where(sel, 1.0, 0.0).astype(jnp.bfloat16)
        rank = jnp.dot(sel_b, tri, preferred_element_type=jnp.float32)
        rel_ref[:, pl.ds(start, RT_T)] = jnp.where(sel, rank, -1.0).astype(jnp.int32)
        lo_ref[...] = jnp.where(lane == c, run_sel.astype(jnp.int32), lo_ref[...])
        return run_sel + count(sel), run_eq + count(eq)

    zero = need * 0.0
    run_sel, _ = lax.fori_loop(0, nt, tile, (zero, zero))
    lo_ref[...] = jnp.where(lane >= nt, run_sel.astype(jnp.int32), lo_ref[...])


def _route(aff_t, cap):
    e, n = aff_t.shape
    nt = n // RT_T
    w = nt + LANE
    full = lambda i: (0, 0)
    return pl.pallas_call(
        functools.partial(_route_kernel, cap),
        grid=(1,),
        in_specs=[pl.BlockSpec((e, n), full)],
        out_specs=(pl.BlockSpec((e, n), full), pl.BlockSpec((e, w), full)),
        out_shape=(jax.ShapeDtypeStruct((e, n), jnp.int32),
                   jax.ShapeDtypeStruct((e, w), jnp.int32)),
        compiler_params=pltpu.CompilerParams(
            dimension_semantics=("arbitrary",),
            vmem_limit_bytes=_vmem_limit(40 * 1024 * 1024)),
        name="route",
    )(aff_t)


def _pack_pairs(x):
    w = x.shape[1] // 2
    lo = pltpu.bitcast(x[:, :w], jnp.uint32)
    hi = pltpu.bitcast(x[:, w:], jnp.uint32)
    return lo | (hi >> 16)


def _unpack_pairs(p):
    lo = pltpu.bitcast(p & jnp.uint32(0xFFFF0000), jnp.float32).astype(jnp.bfloat16)
    hi = pltpu.bitcast(p << 16, jnp.float32).astype(jnp.bfloat16)
    return lo, hi


def _one_hot_rows(rel_ref, shift):
    kio = lax.broadcasted_iota(jnp.int32, (RT_CH, RT_T), 0)
    blocks = []
    for e in range(N_EXPERTS):
        hit = (rel_ref[e:e + 1, :] - shift) == kio
        blocks.append(jnp.where(hit, 1.0, 0.0).astype(jnp.bfloat16))
    return jnp.concatenate(blocks, axis=0)


def _dispatch_kernel(cap, cpad, nt, w, lo_ref, h_ref, rel_ref, xe_hbm,
                     stage, stage_x, sem, sem_x):
    j = pl.program_id(0)
    slot = j % 2

    def dst(e, jj, c):
        row = e * cpad + lo_ref[e * w + jj] + c * RT_CH
        return xe_hbm.at[pl.ds(row, RT_CH), 0]

    def chunk_copy(e, jj, sl):
        return pltpu.make_async_copy(
            stage.at[sl, pl.ds(e * RT_CH, RT_CH)], dst(e, jj, 0), sem.at[sl])

    @pl.when(j == 0)
    def _():
        pad = cpad - cap
        stage_x[...] = jnp.zeros_like(stage_x)
        fills = [pltpu.make_async_copy(
            stage_x.at[pl.ds(0, pad)], xe_hbm.at[pl.ds(e * cpad + cap, pad), 0], sem_x)
            for e in range(N_EXPERTS)]
        for f in fills:
            f.start()
        for f in fills:
            f.wait()

    x = jnp.dot(_one_hot_rows(rel_ref, 0), h_ref[...], preferred_element_type=jnp.float32)
    stage[slot] = _pack_pairs(x)

    @pl.when(j > 0)
    def _():
        for e in range(N_EXPERTS):
            chunk_copy(e, j - 1, 1 - slot).wait()

    for e in range(N_EXPERTS):
        chunk_copy(e, j, slot).start()

    cnts = [lo_ref[e * w + j + 1] - lo_ref[e * w + j] for e in range(N_EXPERTS)]
    most = functools.reduce(jnp.maximum, cnts)
    n_pass = jnp.right_shift(most + (RT_CH - 1), RT_CH.bit_length() - 1)

    def extra(c, carry):
        kio = lax.broadcasted_iota(jnp.int32, (RT_CH, RT_T), 0)

        def extra_copy(e):
            return pltpu.make_async_copy(
                stage_x.at[pl.ds(e * RT_CH, RT_CH)], dst(e, j, c), sem_x)

        for e in range(N_EXPERTS):
            @pl.when(cnts[e] > c * RT_CH)
            def _():
                hit = (rel_ref[e:e + 1, :] - c * RT_CH) == kio
                xx = jnp.dot(jnp.where(hit, 1.0, 0.0).astype(jnp.bfloat16), h_ref[...],
                             preferred_element_type=jnp.float32)
                stage_x[e * RT_CH:(e + 1) * RT_CH, :] = _pack_pairs(xx)
                extra_copy(e).start()
        for e in range(N_EXPERTS):
            @pl.when(cnts[e] > c * RT_CH)
            def _():
                extra_copy(e).wait()
        return carry

    lax.fori_loop(1, n_pass, extra, 0)

    @pl.when(j == nt - 1)
    def _():
        for e in range(N_EXPERTS):
            chunk_copy(e, j, slot).wait()


def _dispatch(h, rel_t, lo_flat, cap, cpad):
    n, d = h.shape
    nt = n // RT_T
    w = lo_flat.shape[0] // N_EXPERTS
    rows = N_EXPERTS * RT_CH
    assert cpad - cap <= rows
    return pl.pallas_call(
        functools.partial(_dispatch_kernel, cap, cpad, nt, w),
        grid_spec=pltpu.PrefetchScalarGridSpec(
            num_scalar_prefetch=1,
            grid=(nt,),
            in_specs=[
                pl.BlockSpec((RT_T, d), lambda j, lo: (j, 0)),
                pl.BlockSpec((N_EXPERTS, RT_T), lambda j, lo: (0, j)),
            ],
            out_specs=pl.BlockSpec(memory_space=pl.ANY),
            scratch_shapes=[
                pltpu.VMEM((2, rows, d // 2), jnp.uint32),
                pltpu.VMEM((rows, d // 2), jnp.uint32),
                pltpu.SemaphoreType.DMA((2,)),
                pltpu.SemaphoreType.DMA(()),
            ],
        ),
        out_shape=jax.ShapeDtypeStruct((N_EXPERTS * cpad, 1, d // 2), jnp.uint32),
        compiler_params=pltpu.CompilerParams(
            dimension_semantics=("arbitrary",),
            vmem_limit_bytes=_vmem_limit(40 * 1024 * 1024)),
        name="dispatch",
    )(lo_flat, h, rel_t)


def _ffn_kernel(layer, cpad, tiles, tm, x_hbm, wg_hbm, wu_hbm, wd_hbm, o_ref,
                xbuf, wg_b, wu_b, wd_b, stg_g, stg_u, stg_d, xsem, wsem):
    i = pl.program_id(0)
    j = pl.program_id(1)
    n_exp = pl.num_programs(0)
    step = i * tiles + j
    slot = step % 2
    prev_slot = (step + 1) % 2
    rg = wg_b.shape[1] // tiles
    rd = wd_b.shape[1] // tiles

    def x_copy(ii, jj, sl):
        return pltpu.make_async_copy(
            x_hbm.at[pl.ds(ii * cpad + jj * tm, tm), 0], xbuf.at[sl], xsem.at[sl])

    def slab_copies(e, k, sl):
        r_g = pl.multiple_of(k * rg, rg)
        r_d = pl.multiple_of(k * rd, rd)
        return (
            pltpu.make_async_copy(wg_hbm.at[layer, e, pl.ds(r_g, rg)], stg_g.at[sl], wsem.at[sl]),
            pltpu.make_async_copy(wu_hbm.at[layer, e, pl.ds(r_g, rg)], stg_u.at[sl], wsem.at[sl]),
            pltpu.make_async_copy(wd_hbm.at[layer, e, pl.ds(r_d, rd)], stg_d.at[sl], wsem.at[sl]),
        )

    def cast_slab(wslot, k, sl):
        r_g = pl.multiple_of(k * rg, rg)
        r_d = pl.multiple_of(k * rd, rd)
        wg_b[wslot, pl.ds(r_g, rg), :] = stg_g[sl].astype(jnp.bfloat16)
        wu_b[wslot, pl.ds(r_g, rg), :] = stg_u[sl].astype(jnp.bfloat16)
        wd_b[wslot, pl.ds(r_d, rd), :] = stg_d[sl].astype(jnp.bfloat16)

    @pl.when(step == 0)
    def _():
        x_copy(0, 0, 0).start()

        def load(k, carry):
            sl = (k + tiles) % 2
            for cp in slab_copies(0, k, sl):
                cp.start()
            for cp in slab_copies(0, k, sl):
                cp.wait()
            cast_slab(0, k, sl)
            return carry

        lax.fori_loop(0, tiles, load, 0)

    j_prev = jnp.where(j > 0, j - 1, tiles - 1)
    e_prev = jnp.where(j > 0, i + 1, i)
    pending = jnp.logical_and(step > 0, e_prev < n_exp)

    @pl.when(pending)
    def _():
        for cp in slab_copies(e_prev, j_prev, prev_slot):
            cp.wait()

    @pl.when(i + 1 < n_exp)
    def _():
        for cp in slab_copies(i + 1, j, slot):
            cp.start()

    cast_slab(e_prev % 2, j_prev, prev_slot)

    @pl.when(step + 1 < n_exp * tiles)
    def _():
        wrap = j + 1 == tiles
        x_copy(jnp.where(wrap, i + 1, i), jnp.where(wrap, 0, j + 1), 1 - slot).start()

    x_copy(i, j, slot).wait()
    x_lo, x_hi = _unpack_pairs(xbuf[slot])
    half = x_lo.shape[1]
    d_ff = wg_b.shape[2]
    wg, wu, wd = wg_b.at[i % 2], wu_b.at[i % 2], wd_b.at[i % 2]
    acc = None
    for c in range(d_ff // FFN_TF):
        f = slice(c * FFN_TF, (c + 1) * FFN_TF)
        g = jnp.dot(x_lo, wg[:half, f], preferred_element_type=jnp.float32)
        g = g + jnp.dot(x_hi, wg[half:, f], preferred_element_type=jnp.float32)
        u = jnp.dot(x_lo, wu[:half, f], preferred_element_type=jnp.float32)
        u = u + jnp.dot(x_hi, wu[half:, f], preferred_element_type=jnp.float32)
        act = (g * jax.nn.sigmoid(g) * u).astype(jnp.bfloat16)
        part = jnp.dot(act, wd[f, :], preferred_element_type=jnp.float32)
        acc = part if acc is None else acc + part
    o_ref[...] = acc.astype(o_ref.dtype)


def _expert_ffn(xe, w_gate, w_up, w_down, layer, cap, cpad, tm):
    _, e, d, d_ff = w_gate.shape
    tiles = cap // tm
    assert d % tiles == 0 and (d // tiles) % 16 == 0
    any_spec = pl.BlockSpec(memory_space=pl.ANY)
    return pl.pallas_call(
        functools.partial(_ffn_kernel, layer, cpad, tiles, tm),
        grid=(e, tiles),
        in_specs=[any_spec, any_spec, any_spec, any_spec],
        out_specs=pl.BlockSpec((tm, d), lambda i, j: (i * tiles + j, 0)),
        out_shape=jax.ShapeDtypeStruct((e * cap, d), jnp.bfloat16),
        scratch_shapes=[
            pltpu.VMEM((2, tm, d // 2), jnp.uint32),
            pltpu.VMEM((2, d, d_ff), jnp.bfloat16),
            pltpu.VMEM((2, d, d_ff), jnp.bfloat16),
            pltpu.VMEM((2, d_ff, d), jnp.bfloat16),
            pltpu.VMEM((2, d // tiles, d_ff), jnp.float32),
            pltpu.VMEM((2, d // tiles, d_ff), jnp.float32),
            pltpu.VMEM((2, d_ff // tiles, d), jnp.float32),
            pltpu.SemaphoreType.DMA((2,)),
            pltpu.SemaphoreType.DMA((2,)),
        ],
        compiler_params=pltpu.CompilerParams(
            dimension_semantics=("arbitrary", "arbitrary"),
            vmem_limit_bytes=_vmem_limit(58 * 1024 * 1024)),
        name="expert_ffn",
    )(xe, w_gate, w_up, w_down)


def _combine_kernel(cap, nt, w, lo_ref, x_ref, rel_ref, aff_ref, ye_hbm, o_ref,
                    ybuf, ybuf_x, sem, sem_x):
    j = pl.program_id(0)
    slot = j % 2
    last_start = N_EXPERTS * cap - RT_CH
    align = 16

    def start_row(e, jj, c):
        lo = lo_ref[e * w + jj]
        a = e * cap + lo - (lo & (align - 1)) + c * RT_CH
        return pl.multiple_of(jnp.minimum(a, last_start), align)

    def fetch(e, jj, sl):
        return pltpu.make_async_copy(
            ye_hbm.at[pl.ds(start_row(e, jj, 0), RT_CH)],
            ybuf.at[sl, pl.ds(e * RT_CH, RT_CH)], sem.at[sl])

    @pl.when(j == 0)
    def _():
        for e in range(N_EXPERTS):
            fetch(e, 0, 0).start()

    @pl.when(j + 1 < nt)
    def _():
        for e in range(N_EXPERTS):
            fetch(e, j + 1, 1 - slot).start()

    los =[lo_ref[e * w + j] for e in range(N_EXPERTS)]
    cnts = [lo_ref[e * w + j + 1] - los[e] for e in range(N_EXPERTS)]
    lead = [los[e] & (align - 1) for e in range(N_EXPERTS)]

    def weight_block(e, c):
        kio = lax.broadcasted_iota(jnp.int32, (RT_CH, RT_T), 0)
        r = rel_ref[e:e + 1, :]
        p = r + lead[e]
        member = (r >= 0) & (p >= c * RT_CH) & (p < (c + 1) * RT_CH)
        off = e * cap + los[e] - start_row(e, j, c)
        hit = member & ((r + off) == kio)
        return jnp.where(hit, aff_ref[e:e + 1, :], 0.0).astype(jnp.bfloat16)

    tn = (((0,), (0,)), ((), ()))
    wt0 = jnp.concatenate([weight_block(e, 0) for e in range(N_EXPERTS)], axis=0)
    for e in range(N_EXPERTS):
        fetch(e, j, slot).wait()
    o_ref[...] = x_ref[...] + lax.dot_general(
        wt0, ybuf[slot], tn, preferred_element_type=jnp.float32)

    spans = [lead[e] + cnts[e] for e in range(N_EXPERTS)]
    most = functools.reduce(jnp.maximum, spans)
    n_pass = jnp.right_shift(most + (RT_CH - 1), RT_CH.bit_length() - 1)

    def extra(c, carry):
        def extra_fetch(e):
            return pltpu.make_async_copy(
                ye_hbm.at[pl.ds(start_row(e, j, c), RT_CH)],
                ybuf_x.at[pl.ds(e * RT_CH, RT_CH)], sem_x)

        for e in range(N_EXPERTS):
            @pl.when(spans[e] > c * RT_CH)
            def _():
                extra_fetch(e).start()
        for e in range(N_EXPERTS):
            @pl.when(spans[e] > c * RT_CH)
            def _():
                extra_fetch(e).wait()
        for e in range(N_EXPERTS):
            @pl.when(spans[e] > c * RT_CH)
            def _():
                wte = weight_block(e, c)
                o_ref[...] += lax.dot_general(
                    wte, ybuf_x[e * RT_CH:(e + 1) * RT_CH, :], tn,
                    preferred_element_type=jnp.float32)
        return carry

    lax.fori_loop(1, n_pass, extra, 0)


def _combine(x1, rel_t, aff_t, ye, lo_flat, cap):
    n, d = x1.shape
    nt = n // RT_T
    w = lo_flat.shape[0] // N_EXPERTS
    rows = N_EXPERTS * RT_CH
    return pl.pallas_call(
        functools.partial(_combine_kernel, cap, nt, w),
        grid_spec=pltpu.PrefetchScalarGridSpec(
            num_scalar_prefetch=1,
            grid=(nt,),
            in_specs=[
                pl.BlockSpec((RT_T, d), lambda j, lo: (j, 0)),
                pl.BlockSpec((N_EXPERTS, RT_T), lambda j, lo: (0, j)),
                pl.BlockSpec((N_EXPERTS, RT_T), lambda j, lo: (0, j)),
                pl.BlockSpec(memory_space=pl.ANY),
            ],
            out_specs=pl.BlockSpec((RT_T, d), lambda j, lo: (j, 0)),
            scratch_shapes=[
                pltpu.VMEM((2, rows, d), jnp.bfloat16),
                pltpu.VMEM((rows, d), jnp.bfloat16),
                pltpu.SemaphoreType.DMA((2,)),
                pltpu.SemaphoreType.DMA(()),
            ],
        ),
        out_shape=jax.ShapeDtypeStruct((n, d), jnp.float32),
        compiler_params=pltpu.CompilerParams(
            dimension_semantics=("arbitrary",),
            vmem_limit_bytes=_vmem_limit(40 * 1024 * 1024)),
        name="combine",
    )(lo_flat, x1, rel_t, aff_t, ye)


def _window_bias_t():
    j = jnp.arange(3 * WIN_SUB)[:, None]
    i = jnp.arange(WIN_SUB)[None, :]
    dist = jnp.abs(i + WIN_SUB - j).astype(jnp.float32)
    slopes = jnp.exp2(-8.0 * (jnp.arange(N_HEADS_A, dtype=jnp.float32) + 1.0) / N_HEADS_A)
    b = jnp.where(dist[None] <= WINDOW, -(slopes[:, None, None] * dist[None]), NEG)
    return jnp.transpose(b, (1, 0, 2)).reshape(3 * WIN_SUB, N_HEADS_A * WIN_SUB) * LOG2E


def _na_bias_t(rpb):
    kk = jnp.arange(NA_KEY_ROWS)[:, None]
    rho = jnp.arange(NA_GROUP_ROWS)[None, :]
    rel = kk - NA_GROUP_ROWS
    r0 = jnp.stack([
        jnp.zeros_like(rho),
        rho - NA_ROWS // 2,
        jnp.full_like(rho, NA_GROUP_ROWS - NA_ROWS),
    ])
    row_ok = (rel[None] >= r0) & (rel[None] < r0 + NA_ROWS)
    dr = jnp.clip(rel - rho + (NA_ROWS - 1), 0, 2 * NA_ROWS - 2)
    ck = jnp.arange(GRID_W)[:, None]
    cq = jnp.arange(GRID_W)[None, :]
    c0 = jnp.clip(cq - NA_COLS // 2, 0, GRID_W - NA_COLS)
    col_ok = (ck >= c0) & (ck < c0 + NA_COLS)
    dc = jnp.clip(ck - cq + (NA_COLS - 1), 0, 2 * NA_COLS - 2)
    hi = lax.Precision.HIGHEST
    oh_r = (dr[:, :, None] == jnp.arange(2 * NA_ROWS - 1)).astype(jnp.float32)
    oh_c = (dc[None] == jnp.arange(2 * NA_COLS - 1)[:, None, None]).astype(jnp.float32)
    rows = jnp.einsum('krs,hsd->hkrd', oh_r, rpb.astype(jnp.float32), precision=hi)
    vals = jnp.einsum('hkrd,dcq->hkcrq', rows, oh_c, precision=hi)
    ok = row_ok[:, :, None, :, None] & col_ok[None, None, :, None, :]
    b = jnp.where(ok[:, None], vals[None], NEG)
    h = rpb.shape[0]
    b = b.reshape(3, h // 2, 2, NA_KEY_ROWS * GRID_W, NA_TQ)
    b = jnp.transpose(b, (0, 1, 3, 2, 4)).reshape(3, h // 2, NA_KEY_ROWS * GRID_W, 2 * NA_TQ)
    return b * LOG2E


def _layer_params(p, l):
    scale = LOG2E / math.sqrt(HEAD_DIM)
    gains = jnp.concatenate([p["qnorm_a"][l] * scale, p["knorm_a"][l],
                             p["qnorm_b"][l] * scale, p["knorm_b"][l]])
    wr = p["w_router"][l].T
    wr_hi = wr.astype(jnp.bfloat16)
    return dict(
        g_mix=p["norm_mix"][l][None, :],
        w_in_t=p["w_in"][l].T.astype(jnp.bfloat16),
        head_gains=jnp.broadcast_to(gains[:, None], (4 * HEAD_DIM, PROJ_TM)),
        sink_row=jnp.repeat(p["sink_a"][l].astype(jnp.float32) * LOG2E, WIN_SUB)[None, :],
        na_bias=_na_bias_t(p["rpb_b"][l]),
        g_a=p["onorm_a"][l][None, :],
        g_b=p["onorm_b"][l][None, :],
        w_out=p["w_out"][l].astype(jnp.bfloat16),
        g_ffn=p["norm_ffn"][l][None, :],
        wr_hi=wr_hi,
        wr_lo=(wr - wr_hi.astype(jnp.float32)).astype(jnp.bfloat16),
        layer=l,
        w_gate=p["w_gate"],
        w_up=p["w_up"],
        w_down=p["w_down"],
    )


def _trunk(x, layers, win_bias):
    b, s, d = x.shape
    n = b * s
    assert s % WIN_TQ == 0 and s // NA_TQ >= 3 and n % PROJ_TM == 0 and n % RT_T == 0
    cap = EC_CAPACITY * n // N_EXPERTS
    tm = min(FFN_TM, cap)
    assert cap % tm == 0 and tm % RT_CH == 0
    cpad = cap + tm
    x2 = x.reshape(n, d)
    for q in layers:
        qa_t, ka, va_t, qb_t, kb, vb_t = _in_proj(x2, q["g_mix"], q["w_in_t"], q["head_gains"])
        out_a = _window_attention(qa_t, ka, va_t, win_bias, q["sink_row"], s)
        out_b = _na_attention(qb_t, kb, vb_t, q["na_bias"], s)
        x1, h, aff_t = _post_attn(out_a, out_b, x2, q["g_a"], q["g_b"], q["w_out"],
                                  q["g_ffn"], q["wr_hi"], q["wr_lo"])
        rel_t, lo = _route(aff_t, cap)
        lo_flat = lo.reshape(-1)
        xe = _dispatch(h, rel_t, lo_flat, cap, cpad)
        ye = _expert_ffn(xe, q["w_gate"], q["w_up"], q["w_down"], q["layer"], cap, cpad, tm)
        x2 = _combine(x1, rel_t, aff_t, ye, lo_flat, cap)
    return x2.reshape(b, s, d)


def kernel(x_prompt, x_sample, norm_mix, w_in, qnorm_a, knorm_a, sink_a, qnorm_b, knorm_b,
           rpb_b, onorm_a, onorm_b, w_out, norm_ffn, w_router, w_gate, w_up, w_down):
    p = dict(norm_mix=norm_mix, w_in=w_in, qnorm_a=qnorm_a, knorm_a=knorm_a, sink_a=sink_a,
             qnorm_b=qnorm_b, knorm_b=knorm_b, rpb_b=rpb_b, onorm_a=onorm_a, onorm_b=onorm_b,
             w_out=w_out, norm_ffn=norm_ffn, w_router=w_router, w_gate=w_gate, w_up=w_up,
             w_down=w_down)
    layers = [_layer_params(p, l) for l in range(w_in.shape[0])]
    win_bias = _window_bias_t()
    return (_trunk(x_prompt, layers, win_bias), _trunk(x_sample, layers, win_bias))
```

```python
import functools
import math

import jax
import jax.numpy as jnp
from jax import lax
from jax.experimental import pallas as pl
from jax.experimental.pallas import tpu as pltpu

HEAD_DIM = 64
N_HEADS_A = 8
N_KV_HEADS_A = 2
N_HEADS_B = 8
QA_W = N_HEADS_A * HEAD_DIM
KVA_W = N_KV_HEADS_A * HEAD_DIM
QKVB_W = N_HEADS_B * HEAD_DIM
PROJ_W = QA_W + 2 * KVA_W + 3 * QKVB_W
WINDOW = 128
GRID_W = 64
NA_ROWS = 8
NA_COLS = 16
N_EXPERTS = 16
EC_CAPACITY = 2
EPS = 1e-6
NEG = -1e30
LOG2E = 1.4426950408889634

LANE = 128
V7X_VMEM_BYTES = 64 * 1024 * 1024

PROJ_TM = 512
WIN_TQ = 512
WIN_SUB = WINDOW
NA_GROUP_ROWS = 4
NA_TQ = NA_GROUP_ROWS * GRID_W
NA_KEY_ROWS = 3 * NA_GROUP_ROWS
NA_KC = 128
POST_TM = 512
FFN_TM = 512
FFN_TF = 512
RT_T = 256
RT_CH = 64
RT_G = 16

_NT = (((1,), (1,)), ((), ()))


def _vmem_limit(nbytes):
    return int(min(nbytes, V7X_VMEM_BYTES - 4 * 1024 * 1024))


def _proj_kernel(x_ref, g_ref, w_ref, hg_ref,
                 qa_ref, ka_ref, va_ref, qb_ref, kb_ref, vb_ref):
    x = x_ref[...]
    ms = jnp.mean(x * x, axis=-1, keepdims=True)
    h = (x * lax.rsqrt(ms + EPS) * g_ref[...]).astype(jnp.bfloat16)

    def seg(lo, hi):
        return lax.dot_general(w_ref[lo:hi, :], h, _NT,
                               preferred_element_type=jnp.float32)

    def head_norm(blk, gain):
        ssq = jnp.sum(blk * blk, axis=0, keepdims=True)
        return blk * lax.rsqrt(ssq * (1.0 / HEAD_DIM) + EPS) * gain

    g_qa = hg_ref[0 * HEAD_DIM:1 * HEAD_DIM, :]
    g_ka = hg_ref[1 * HEAD_DIM:2 * HEAD_DIM, :]
    g_qb = hg_ref[2 * HEAD_DIM:3 * HEAD_DIM, :]
    g_kb = hg_ref[3 * HEAD_DIM:4 * HEAD_DIM, :]

    o = 0
    p = seg(o, o + QA_W)
    for hd in range(N_HEADS_A):
        r = slice(hd * HEAD_DIM, (hd + 1) * HEAD_DIM)
        qa_ref[r, :] = head_norm(p[r, :], g_qa).astype(qa_ref.dtype)
    o += QA_W
    p = seg(o, o + 2 * KVA_W)
    kn = jnp.concatenate(
        [head_norm(p[hd * HEAD_DIM:(hd + 1) * HEAD_DIM, :], g_ka)
         for hd in range(N_KV_HEADS_A)], axis=0)
    ka_ref[...] = kn.T.astype(ka_ref.dtype)
    va_ref[...] = p[KVA_W:2 * KVA_W, :].astype(va_ref.dtype)
    o += 2 * KVA_W
    p = seg(o, o + QKVB_W)
    for hd in range(N_HEADS_B):
        r = slice(hd * HEAD_DIM, (hd + 1) * HEAD_DIM)
        qb_ref[r, :] = head_norm(p[r, :], g_qb).astype(qb_ref.dtype)
    o += QKVB_W
    p = seg(o, o + QKVB_W)
    kn = jnp.concatenate(
        [head_norm(p[hd * HEAD_DIM:(hd + 1) * HEAD_DIM, :], g_kb)
         for hd in range(N_HEADS_B)], axis=0)
    kb_ref[...] = kn.T.astype(kb_ref.dtype)
    o += QKVB_W
    vb_ref[...] = seg(o, o + QKVB_W).astype(vb_ref.dtype)


def _in_proj(x2d, g_mix, w_in_t, head_gains):
    n, d = x2d.shape
    tm = PROJ_TM
    bf = jnp.bfloat16
    col = lambda i: (0, i)
    row = lambda i: (i, 0)
    const = lambda i: (0, 0)
    out_shape = (
        jax.ShapeDtypeStruct((QA_W, n), bf),
        jax.ShapeDtypeStruct((n, KVA_W), bf),
        jax.ShapeDtypeStruct((KVA_W, n), bf),
        jax.ShapeDtypeStruct((QKVB_W, n), bf),
        jax.ShapeDtypeStruct((n, QKVB_W), bf),
        jax.ShapeDtypeStruct((QKVB_W, n), bf),
    )
    out_specs = (
        pl.BlockSpec((QA_W, tm), col),
        pl.BlockSpec((tm, KVA_W), row),
        pl.BlockSpec((KVA_W, tm), col),
        pl.BlockSpec((QKVB_W, tm), col),
        pl.BlockSpec((tm, QKVB_W), row),
        pl.BlockSpec((QKVB_W, tm), col),
    )
    return pl.pallas_call(
        _proj_kernel,
        grid=(n // tm,),
        in_specs=[
            pl.BlockSpec((tm, d), row),
            pl.BlockSpec((1, d), const),
            pl.BlockSpec((PROJ_W, d), const),
            pl.BlockSpec((4 * HEAD_DIM, tm), const),
        ],
        out_specs=out_specs,
        out_shape=out_shape,
        compiler_params=pltpu.CompilerParams(
            dimension_semantics=("arbitrary",),
            vmem_limit_bytes=_vmem_limit(48 * 1024 * 1024)),
        name="in_proj",
    )(x2d, g_mix, w_in_t, head_gains)


def _fold(acc, v, op):
    return v if acc is None else op(acc, v)


def _window_kernel(blocks_per_seq, q_ref, kp_ref, kc_ref, kn_ref,
                   vp_ref, vc_ref, vn_ref, bias_ref, sink_ref, o_ref, s_scr, p_scr):
    i = pl.program_id(0)
    pos = i % blocks_per_seq
    pen_prev = jnp.where(pos == 0, NEG, 0.0).astype(jnp.float32)
    pen_next = jnp.where(pos == blocks_per_seq - 1, NEG, 0.0).astype(jnp.float32)

    sink = sink_ref[...]
    n_sub = WIN_TQ // WIN_SUB
    n_chunks = 3
    gq = N_HEADS_A // N_KV_HEADS_A
    zero = jnp.zeros((HEAD_DIM, WIN_SUB), jnp.bfloat16)
    krefs = (kp_ref, kc_ref, kn_ref)

    def key_block(kb):
        if kb == 0:
            return 0, 0
        if kb == n_sub + 1:
            return 2, 0
        return 1, (kb - 1) * WIN_SUB

    def qblock(j):
        cols = slice(j * WIN_SUB, (j + 1) * WIN_SUB)
        halves = []
        for kv in range(N_KV_HEADS_A):
            parts = []
            for hd in range(N_HEADS_A):
                if hd // gq == kv:
                    parts.append(q_ref[hd * HEAD_DIM:(hd + 1) * HEAD_DIM, cols])
                else:
                    parts.append(zero)
            halves.append(jnp.concatenate(parts, axis=1))
        return jnp.concatenate(halves, axis=0)

    def score_chunk(j, c, qblk):
        rows = slice(c * WIN_SUB, (c + 1) * WIN_SUB)
        r, off = key_block(j + c)
        s = jnp.dot(krefs[r][off:off + WIN_SUB, :], qblk,
                    preferred_element_type=jnp.float32)
        s = s + bias_ref[rows, :]
        if j + c == 0:
            s = s + pen_prev
        if j + c == n_sub + 1:
            s = s + pen_next
        s_scr[j % 2, rows, :] = s
        return jnp.max(s, axis=0, keepdims=True)

    def prob_chunk(j, c, m):
        rows = slice(c * WIN_SUB, (c + 1) * WIN_SUB)
        p_scr[j % 2, rows, :] = jnp.exp2(s_scr[j % 2, rows, :] - m).astype(jnp.bfloat16)

    def finish(j, m):
        cols = slice(j * WIN_SUB, (j + 1) * WIN_SUB)
        vparts = []
        for c in range(n_chunks):
            r, off = key_block(j + c)
            vparts.append((vp_ref, vc_ref, vn_ref)[r][:, off:off + WIN_SUB])
        vwin = jnp.concatenate(vparts, axis=1)
        ones = jnp.ones((16, n_chunks * WIN_SUB), jnp.bfloat16)
        sink_term = jnp.exp2(sink - m)
        outs = []
        for kv in range(N_KV_HEADS_A):
            lanes = slice(kv * gq * WIN_SUB, (kv + 1) * gq * WIN_SUB)
            vt = jnp.concatenate([vwin[kv * HEAD_DIM:(kv + 1) * HEAD_DIM, :], ones], axis=0)
            o_t = jnp.dot(vt, p_scr[j % 2, :, lanes],
                          preferred_element_type=jnp.float32)
            o_t = o_t[:HEAD_DIM] / (o_t[HEAD_DIM:HEAD_DIM + 1] + sink_term[:, lanes])
            for g in range(gq):
                outs.append(o_t[:, g * WIN_SUB:(g + 1) * WIN_SUB])
        for a in range(N_HEADS_A // 2):
            pair = jnp.concatenate([outs[2 * a], outs[2 * a + 1]], axis=0)
            o_ref[cols, a * LANE:(a + 1) * LANE] = pair.T.astype(o_ref.dtype)

    qb = qblock(0)
    m = None
    for c in range(n_chunks):
        m = _fold(m, score_chunk(0, c, qb), jnp.maximum)
    m = jnp.maximum(m, sink)
    m_done = None
    for j in range(n_sub):
        m_next = None
        if j + 1 < n_sub:
            qb = qblock(j + 1)
        for c in range(n_chunks):
            prob_chunk(j, c, m)
            if j + 1 < n_sub:
                m_next = _fold(m_next, score_chunk(j + 1, c, qb), jnp.maximum)
            if c == 0 and j >= 1:
                finish(j - 1, m_done)
        m_done = m
        if j + 1 < n_sub:
            m = jnp.maximum(m_next, sink)
    finish(n_sub - 1, m_done)


def _window_attention(qa_t, ka, va_t, bias_t, sink_row, seq_len):
    n = ka.shape[0]
    nblk = n // WIN_TQ
    bps = seq_len // WIN_TQ
    r = WIN_TQ // WIN_SUB
    nsub = n // WIN_SUB
    prev_i = lambda i: jnp.maximum(r * i - 1, 0)
    next_i = lambda i: jnp.minimum(r * i + r, nsub - 1)
    const = lambda i: (0, 0)
    return pl.pallas_call(
        functools.partial(_window_kernel, bps),
        grid=(nblk,),
        in_specs=[
            pl.BlockSpec((QA_W, WIN_TQ), lambda i: (0, i)),
            pl.BlockSpec((WIN_SUB, KVA_W), lambda i: (prev_i(i), 0)),
            pl.BlockSpec((WIN_TQ, KVA_W), lambda i: (i, 0)),
            pl.BlockSpec((WIN_SUB, KVA_W), lambda i: (next_i(i), 0)),
            pl.BlockSpec((KVA_W, WIN_SUB), lambda i: (0, prev_i(i))),
            pl.BlockSpec((KVA_W, WIN_TQ), lambda i: (0, i)),
            pl.BlockSpec((KVA_W, WIN_SUB), lambda i: (0, next_i(i))),
            pl.BlockSpec((3 * WIN_SUB, N_HEADS_A * WIN_SUB), const),
            pl.BlockSpec((1, N_HEADS_A * WIN_SUB), const),
        ],
        out_specs=pl.BlockSpec((WIN_TQ, QA_W), lambda i: (i, 0)),
        out_shape=jax.ShapeDtypeStruct((n, QA_W), jnp.bfloat16),
        scratch_shapes=[
            pltpu.VMEM((2, 3 * WIN_SUB, N_HEADS_A * WIN_SUB), jnp.float32),
            pltpu.VMEM((2, 3 * WIN_SUB, N_HEADS_A * WIN_SUB), jnp.bfloat16),
        ],
        compiler_params=pltpu.CompilerParams(
            dimension_semantics=("arbitrary",),
            vmem_limit_bytes=_vmem_limit(40 * 1024 * 1024)),
        name="window_attn",
    )(qa_t, ka, ka, ka, va_t, va_t, va_t, bias_t, sink_row)


def _na_kernel(q_ref, kp_ref, kc_ref, kn_ref, vp_ref, vc_ref, vn_ref,
               bias_ref, o_ref, s_scr, p_scr):
    zero = jnp.zeros((HEAD_DIM, NA_TQ), jnp.bfloat16)
    n_keys = NA_KEY_ROWS * GRID_W
    n_chunks = n_keys // NA_KC
    n_pairs = N_HEADS_B // 2
    krefs = (kp_ref, kc_ref, kn_ref)

    def qblock(pr):
        q0 = q_ref[(2 * pr) * HEAD_DIM:(2 * pr + 1) * HEAD_DIM, :]
        q1 = q_ref[(2 * pr + 1) * HEAD_DIM:(2 * pr + 2) * HEAD_DIM, :]
        return jnp.concatenate(
            [jnp.concatenate([q0, zero], axis=1),
             jnp.concatenate([zero, q1], axis=1)], axis=0)

    def score_chunk(pr, c, qblk):
        rows = slice(c * NA_KC, (c + 1) * NA_KC)
        blk, off = divmod(c * NA_KC, NA_TQ)
        kchunk = krefs[blk][off:off + NA_KC, pr * LANE:(pr + 1) * LANE]
        s = jnp.dot(kchunk, qblk, preferred_element_type=jnp.float32)
        s = s + bias_ref[0, pr, rows, :]
        s_scr[pr % 2, rows, :] = s
        return jnp.max(s, axis=0, keepdims=True)

    def prob_chunk(pr, c, m):
        rows = slice(c * NA_KC, (c + 1) * NA_KC)
        p_scr[pr % 2, rows, :] = jnp.exp2(s_scr[pr % 2, rows, :] - m).astype(jnp.bfloat16)

    def finish(pr):
        lanes = slice(pr * LANE, (pr + 1) * LANE)
        vwin = jnp.concatenate(
            [vp_ref[lanes, :], vc_ref[lanes, :], vn_ref[lanes, :]], axis=1)
        ones = jnp.ones((16, n_keys), jnp.bfloat16)
        outs = []
        for t in range(2):
            cols = slice(t * NA_TQ, (t + 1) * NA_TQ)
            vt = jnp.concatenate([vwin[t * HEAD_DIM:(t + 1) * HEAD_DIM, :], ones], axis=0)
            o_t = jnp.dot(vt, p_scr[pr % 2, :, cols],
                          preferred_element_type=jnp.float32)
            outs.append(o_t[:HEAD_DIM] / o_t[HEAD_DIM:HEAD_DIM + 1])
        pair = jnp.concatenate(outs, axis=0)
        o_ref[:, lanes] = pair.T.astype(o_ref.dtype)

    qb = qblock(0)
    m = None
    for c in range(n_chunks):
        m = _fold(m, score_chunk(0, c, qb), jnp.maximum)
    for pr in range(n_pairs):
        m_next = None
        if pr + 1 < n_pairs:
            qb = qblock(pr + 1)
        for c in range(n_chunks):
            prob_chunk(pr, c, m)
            if pr + 1 < n_pairs:
                m_next = _fold(m_next, score_chunk(pr + 1, c, qb), jnp.maximum)
            if c == 0 and pr >= 1:
                finish(pr - 1)
        m = m_next
    finish(n_pairs - 1)


def _na_attention(qb_t, kb, vb_t, bias, seq_len):
    n = kb.shape[0]
    ng = n // NA_TQ
    gps = seq_len // NA_TQ
    prev_i = lambda g: jnp.maximum(g - 1, 0)
    next_i = lambda g: jnp.minimum(g + 1, ng - 1)

    def variant(g):
        pos = g % gps
        return jnp.where(pos == 0, 0, jnp.where(pos == gps - 1, 2, 1))

    return pl.pallas_call(
        _na_kernel,
        grid=(ng,),
        in_specs=[
            pl.BlockSpec((QKVB_W, NA_TQ), lambda g: (0, g)),
            pl.BlockSpec((NA_TQ, QKVB_W), lambda g: (prev_i(g), 0)),
            pl.BlockSpec((NA_TQ, QKVB_W), lambda g: (g, 0)),
            pl.BlockSpec((NA_TQ, QKVB_W), lambda g: (next_i(g), 0)),
            pl.BlockSpec((QKVB_W, NA_TQ), lambda g: (0, prev_i(g))),
            pl.BlockSpec((QKVB_W, NA_TQ), lambda g: (0, g)),
            pl.BlockSpec((QKVB_W, NA_TQ), lambda g: (0, next_i(g))),
            pl.BlockSpec((1, N_HEADS_B // 2, NA_KEY_ROWS * GRID_W, 2 * NA_TQ),
                         lambda g: (variant(g), 0, 0, 0)),
        ],
        out_specs=pl.BlockSpec((NA_TQ, QKVB_W), lambda g: (g, 0)),
        out_shape=jax.ShapeDtypeStruct((n, QKVB_W), jnp.bfloat16),
        scratch_shapes=[
            pltpu.VMEM((2, NA_KEY_ROWS * GRID_W, 2 * NA_TQ), jnp.float32),
            pltpu.VMEM((2, NA_KEY_ROWS * GRID_W, 2 * NA_TQ), jnp.bfloat16),
        ],
        compiler_params=pltpu.CompilerParams(
            dimension_semantics=("arbitrary",),
            vmem_limit_bytes=_vmem_limit(48 * 1024 * 1024)),
        name="na_attn",
    )(qb_t, kb, kb, kb, vb_t, vb_t, vb_t, bias)


def _post_kernel(a_ref, b_ref, x_ref, ga_ref, gb_ref, w_ref, gf_ref,
                 wrh_ref, wrl_ref, x1_ref, h_ref, aff_ref):
    def rms(v, g):
        ms = jnp.mean(v * v, axis=-1, keepdims=True)
        return v * lax.rsqrt(ms + EPS) * g

    an = rms(a_ref[...].astype(jnp.float32), ga_ref[...]).astype(jnp.bfloat16)
    bn = rms(b_ref[...].astype(jnp.float32), gb_ref[...]).astype(jnp.bfloat16)
    y = jnp.dot(an, w_ref[:QA_W, :], preferred_element_type=jnp.float32)
    y = y + jnp.dot(bn, w_ref[QA_W:, :], preferred_element_type=jnp.float32)
    x1 = x_ref[...] + y
    x1_ref[...] = x1
    h = rms(x1, gf_ref[...])
    h_hi = h.astype(jnp.bfloat16)
    h_lo = (h - h_hi.astype(jnp.float32)).astype(jnp.bfloat16)
    h_ref[...] = h_hi
    wrh = wrh_ref[...]
    logits = lax.dot_general(wrh, h_hi, _NT, preferred_element_type=jnp.float32)
    logits = logits + lax.dot_general(wrh, h_lo, _NT, preferred_element_type=jnp.float32)
    logits = logits + lax.dot_general(wrl_ref[...], h_hi, _NT,
                                      preferred_element_type=jnp.float32)
    m = jnp.max(logits, axis=0, keepdims=True)
    e = jnp.exp(logits - m)
    aff_ref[...] = e / jnp.sum(e, axis=0, keepdims=True)


def _post_attn(out_a, out_b, x2d, g_a, g_b, w_out, g_ffn, wr_hi, wr_lo):
    n, d = x2d.shape
    tm = POST_TM
    row = lambda i: (i, 0)
    const = lambda i: (0, 0)
    return pl.pallas_call(
        _post_kernel,
        grid=(n // tm,),
        in_specs=[
            pl.BlockSpec((tm, QA_W), row),
            pl.BlockSpec((tm, QKVB_W), row),
            pl.BlockSpec((tm, d), row),
            pl.BlockSpec((1, QA_W), const),
            pl.BlockSpec((1, QKVB_W), const),
            pl.BlockSpec((QA_W + QKVB_W, d), const),
            pl.BlockSpec((1, d), const),
            pl.BlockSpec((N_EXPERTS, d), const),
            pl.BlockSpec((N_EXPERTS, d), const),
        ],
        out_specs=(
            pl.BlockSpec((tm, d), row),
            pl.BlockSpec((tm, d), row),
            pl.BlockSpec((N_EXPERTS, tm), lambda i: (0, i)),
        ),
        out_shape=(
            jax.ShapeDtypeStruct((n, d), jnp.float32),
            jax.ShapeDtypeStruct((n, d), jnp.bfloat16),
            jax.ShapeDtypeStruct((N_EXPERTS, n), jnp.float32),
        ),
        compiler_params=pltpu.CompilerParams(
            dimension_semantics=("arbitrary",),
            vmem_limit_bytes=_vmem_limit(40 * 1024 * 1024)),
        name="post_attn",
    )(out_a, out_b, x2d, g_a, g_b, w_out, g_ffn, wr_hi, wr_lo)


def _strict_upper(n):
    r = lax.broadcasted_iota(jnp.int32, (n, n), 0)
    c = lax.broadcasted_iota(jnp.int32, (n, n), 1)
    return jnp.where(r < c, 1.0, 0.0).astype(jnp.bfloat16)


def _select_kernel(cap, aff_ref, sel_ref):
    n = aff_ref.shape[1]
    nt = n // RT_T
    cap_f = jnp.float32(cap)

    def count(mask):
        return jnp.sum(jnp.where(mask, 1.0, 0.0), axis=1, keepdims=True)

    def search(b, ans):
        cand = ans | jnp.left_shift(jnp.int32(1), 30 - b)
        bits = pltpu.bitcast(aff_ref[...], jnp.int32)
        return jnp.where(count(bits >= cand) >= cap_f, cand, ans)

    thr = lax.fori_loop(0, 31, search, jnp.zeros((N_EXPERTS, 1), jnp.int32))
    need = cap_f - count(pltpu.bitcast(aff_ref[...], jnp.int32) > thr)
    tri = _strict_upper(RT_T)

    def tile(c, run_eq):
        start = pl.multiple_of(c * RT_T, RT_T)
        bits = pltpu.bitcast(aff_ref[:, pl.ds(start, RT_T)], jnp.int32)
        eq = bits == thr
        eq_b = jnp.where(eq, 1.0, 0.0).astype(jnp.bfloat16)
        eq_rank = jnp.dot(eq_b, tri, preferred_element_type=jnp.float32)
        sel = (bits > thr) | (eq & (run_eq + eq_rank < need))
        sel_ref[:, pl.ds(start, RT_T)] = jnp.where(sel, 1.0, 0.0)
        return run_eq + count(eq)

    lax.fori_loop(0, nt, tile, need * 0.0)


def _rank_kernel(sel_ref, rel_ref, lo_ref):
    n = sel_ref.shape[1]
    nt = n // RT_T
    w = lo_ref.shape[1]
    tri = _strict_upper(RT_T)
    lane = lax.broadcasted_iota(jnp.int32, (N_EXPERTS, w), 1)
    lo_ref[...] = jnp.zeros_like(lo_ref)

    def tile(c, run):
        start = pl.multiple_of(c * RT_T, RT_T)
        sel = sel_ref[:, pl.ds(start, RT_T)]
        rank = jnp.dot(sel.astype(jnp.bfloat16), tri, preferred_element_type=jnp.float32)
        rel_ref[:, pl.ds(start, RT_T)] = jnp.where(sel > 0.0, rank, -1.0).astype(jnp.int32)
        lo_ref[...] = jnp.where(lane == c, run.astype(jnp.int32), lo_ref[...])
        return run + jnp.sum(sel, axis=1, keepdims=True)

    first = jnp.sum(sel_ref[:, 0:LANE], axis=1, keepdims=True)
    run = lax.fori_loop(0, nt, tile, first * 0.0)
    lo_ref[...] = jnp.where(lane >= nt, run.astype(jnp.int32), lo_ref[...])


def _select(aff_t, cap):
    e, n = aff_t.shape
    full = lambda i: (0, 0)
    return pl.pallas_call(
        functools.partial(_select_kernel, cap),
        grid=(1,),
        in_specs=[pl.BlockSpec((e, n), full)],
        out_specs=pl.BlockSpec((e, n), full),
        out_shape=jax.ShapeDtypeStruct((e, n), jnp.float32),
        compiler_params=pltpu.CompilerParams(
            dimension_semantics=("arbitrary",),
            vmem_limit_bytes=_vmem_limit(40 * 1024 * 1024)),
        name="route_select",
    )(aff_t)


def _rank(sel_t):
    e, n = sel_t.shape
    nt = n // RT_T
    w = nt + LANE
    full = lambda i: (0, 0)
    return pl.pallas_call(
        _rank_kernel,
        grid=(1,),
        in_specs=[pl.BlockSpec((e, n), full)],
        out_specs=(pl.BlockSpec((e, n), full), pl.BlockSpec((e, w), full)),
        out_shape=(jax.ShapeDtypeStruct((e, n), jnp.int32),
                   jax.ShapeDtypeStruct((e, w), jnp.int32)),
        compiler_params=pltpu.CompilerParams(
            dimension_semantics=("arbitrary",),
            vmem_limit_bytes=_vmem_limit(40 * 1024 * 1024)),
        name="route_rank",
    )(sel_t)


def _tile_major(a, nt):
    e, n = a.shape
    g = RT_T // RT_G
    return a.reshape(e, g, nt, RT_G).transpose(0, 2, 1, 3).reshape(e, n)


def _granule_view(x, nt):
    n, d = x.shape
    return x.reshape(RT_T // RT_G, nt, RT_G, d)


def _pack_pairs(x):
    w = x.shape[1] // 2
    lo = pltpu.bitcast(x[:, :w], jnp.uint32)
    hi = pltpu.bitcast(x[:, w:], jnp.uint32)
    return lo | (hi >> 16)


def _unpack_pairs(p):
    lo = pltpu.bitcast(p & jnp.uint32(0xFFFF0000), jnp.float32).astype(jnp.bfloat16)
    hi = pltpu.bitcast(p << 16, jnp.float32).astype(jnp.bfloat16)
    return lo, hi


def _one_hot_rows(rel_ref, shift):
    kio = lax.broadcasted_iota(jnp.int32, (RT_CH, RT_T), 0)
    blocks = []
    for e in range(N_EXPERTS):
        hit = (rel_ref[e:e + 1, :] - shift) == kio
        blocks.append(jnp.where(hit, 1.0, 0.0).astype(jnp.bfloat16))
    return jnp.concatenate(blocks, axis=0)


def _dispatch_kernel(cap, cpad, nt, w, lo_ref, h_ref, rel_ref, xe_hbm,
                     stage, stage_x, sem, sem_x):
    j = pl.program_id(0)
    slot = j % 2

    def dst(e, jj, c):
        row = e * cpad + lo_ref[e * w + jj] + c * RT_CH
        return xe_hbm.at[pl.ds(row, RT_CH), 0]

    def chunk_copy(e, jj, sl):
        return pltpu.make_async_copy(
            stage.at[sl, pl.ds(e * RT_CH, RT_CH)], dst(e, jj, 0), sem.at[sl])

    @pl.when(j == 0)
    def _():
        pad = cpad - cap
        stage_x[...] = jnp.zeros_like(stage_x)
        fills = [pltpu.make_async_copy(
            stage_x.at[pl.ds(0, pad)], xe_hbm.at[pl.ds(e * cpad + cap, pad), 0], sem_x)
            for e in range(N_EXPERTS)]
        for f in fills:
            f.start()
        for f in fills:
            f.wait()

    h_tile = h_ref[...].reshape(RT_T, h_ref.shape[-1])
    x = jnp.dot(_one_hot_rows(rel_ref, 0), h_tile, preferred_element_type=jnp.float32)
    stage[slot] = _pack_pairs(x)

    @pl.when(j > 0)
    def _():
        for e in range(N_EXPERTS):
            chunk_copy(e, j - 1, 1 - slot).wait()

    for e in range(N_EXPERTS):
        chunk_copy(e, j, slot).start()

    cnts = [lo_ref[e * w + j + 1] - lo_ref[e * w + j] for e in range(N_EXPERTS)]
    most = functools.reduce(jnp.maximum, cnts)
    n_pass = jnp.right_shift(most + (RT_CH - 1), RT_CH.bit_length() - 1)

    def extra(c, carry):
        kio = lax.broadcasted_iota(jnp.int32, (RT_CH, RT_T), 0)

        def extra_copy(e):
            return pltpu.make_async_copy(
                stage_x.at[pl.ds(e * RT_CH, RT_CH)], dst(e, j, c), sem_x)

        for e in range(N_EXPERTS):
            @pl.when(cnts[e] > c * RT_CH)
            def _():
                hit = (rel_ref[e:e + 1, :] - c * RT_CH) == kio
                xx = jnp.dot(jnp.where(hit, 1.0, 0.0).astype(jnp.bfloat16),
                             h_ref[...].reshape(RT_T, h_ref.shape[-1]),
                             preferred_element_type=jnp.float32)
                stage_x[e * RT_CH:(e + 1) * RT_CH, :] = _pack_pairs(xx)
                extra_copy(e).start()
        for e in range(N_EXPERTS):
            @pl.when(cnts[e] > c * RT_CH)
            def _():
                extra_copy(e).wait()
        return carry

    lax.fori_loop(1, n_pass, extra, 0)

    @pl.when(j == nt - 1)
    def _():
        for e in range(N_EXPERTS):
            chunk_copy(e, j, slot).wait()


def _dispatch(h4, rel_t, lo_flat, cap, cpad):
    _, nt, _, d = h4.shape
    w = lo_flat.shape[0] // N_EXPERTS
    rows = N_EXPERTS * RT_CH
    assert cpad - cap <= rows
    return pl.pallas_call(
        functools.partial(_dispatch_kernel, cap, cpad, nt, w),
        grid_spec=pltpu.PrefetchScalarGridSpec(
            num_scalar_prefetch=1,
            grid=(nt,),
            in_specs=[
                pl.BlockSpec((RT_T // RT_G, 1, RT_G, d), lambda j, lo: (0, j, 0, 0)),
                pl.BlockSpec((N_EXPERTS, RT_T), lambda j, lo: (0, j)),
            ],
            out_specs=pl.BlockSpec(memory_space=pl.ANY),
            scratch_shapes=[
                pltpu.VMEM((2, rows, d // 2), jnp.uint32),
                pltpu.VMEM((rows, d // 2), jnp.uint32),
                pltpu.SemaphoreType.DMA((2,)),
                pltpu.SemaphoreType.DMA(()),
            ],
        ),
        out_shape=jax.ShapeDtypeStruct((N_EXPERTS * cpad, 1, d // 2), jnp.uint32),
        compiler_params=pltpu.CompilerParams(
            dimension_semantics=("arbitrary",),
            vmem_limit_bytes=_vmem_limit(40 * 1024 * 1024)),
        name="dispatch",
    )(lo_flat, h4, rel_t)


def _ffn_kernel(layer, cpad, tiles, tm, x_hbm, wg_hbm, wu_hbm, wd_hbm, o_ref,
                xbuf, wg_b, wu_b, wd_b, stg_g, stg_u, stg_d, xsem, wsem):
    i = pl.program_id(0)
    j = pl.program_id(1)
    n_exp = pl.num_programs(0)
    step = i * tiles + j
    slot = step % 2
    prev_slot = (step + 1) % 2
    rg = wg_b.shape[1] // tiles
    rd = wd_b.shape[1] // tiles

    def x_copy(ii, jj, sl):
        return pltpu.make_async_copy(
            x_hbm.at[pl.ds(ii * cpad + jj * tm, tm), 0], xbuf.at[sl], xsem.at[sl])

    def slab_copies(e, k, sl):
        r_g = pl.multiple_of(k * rg, rg)
        r_d = pl.multiple_of(k * rd, rd)
        return (
            pltpu.make_async_copy(wg_hbm.at[layer, e, pl.ds(r_g, rg)], stg_g.at[sl], wsem.at[sl]),
            pltpu.make_async_copy(wu_hbm.at[layer, e, pl.ds(r_g, rg)], stg_u.at[sl], wsem.at[sl]),
            pltpu.make_async_copy(wd_hbm.at[layer, e, pl.ds(r_d, rd)], stg_d.at[sl], wsem.at[sl]),
        )

    def cast_slab(wslot, k, sl):
        r_g = pl.multiple_of(k * rg, rg)
        r_d = pl.multiple_of(k * rd, rd)
        wg_b[wslot, pl.ds(r_g, rg), :] = stg_g[sl].astype(jnp.bfloat16)
        wu_b[wslot, pl.ds(r_g, rg), :] = stg_u[sl].astype(jnp.bfloat16)
        wd_b[wslot, pl.ds(r_d, rd), :] = stg_d[sl].astype(jnp.bfloat16)

    @pl.when(step == 0)
    def _():
        x_copy(0, 0, 0).start()

        def load(k, carry):
            sl = (k + tiles) % 2
            for cp in slab_copies(0, k, sl):
                cp.start()
            for cp in slab_copies(0, k, sl):
                cp.wait()
            cast_slab(0, k, sl)
            return carry

        lax.fori_loop(0, tiles, load, 0)

    j_prev = jnp.where(j > 0, j - 1, tiles - 1)
    e_prev = jnp.where(j > 0, i + 1, i)
    pending = jnp.logical_and(step > 0, e_prev < n_exp)

    @pl.when(pending)
    def _():
        for cp in slab_copies(e_prev, j_prev, prev_slot):
            cp.wait()

    @pl.when(i + 1 < n_exp)
    def _():
        for cp in slab_copies(i + 1, j, slot):
            cp.start()

    cast_slab(e_prev % 2, j_prev, prev_slot)

    @pl.when(step + 1 < n_exp * tiles)
    def _():
        wrap = j + 1 == tiles
        x_copy(jnp.where(wrap, i + 1, i), jnp.where(wrap, 0, j + 1), 1 - slot).start()

    x_copy(i, j, slot).wait()
    x_lo, x_hi = _unpack_pairs(xbuf[slot])
    half = x_lo.shape[1]
    d_ff = wg_b.shape[2]
    wg, wu, wd = wg_b.at[i % 2], wu_b.at[i % 2], wd_b.at[i % 2]
    acc = None
    for c in range(d_ff // FFN_TF):
        f = slice(c * FFN_TF, (c + 1) * FFN_TF)
        g = jnp.dot(x_lo, wg[:half, f], preferred_element_type=jnp.float32)
        g = g + jnp.dot(x_hi, wg[half:, f], preferred_element_type=jnp.float32)
        u = jnp.dot(x_lo, wu[:half, f], preferred_element_type=jnp.float32)
        u = u + jnp.dot(x_hi, wu[half:, f], preferred_element_type=jnp.float32)
        act = (g * jax.nn.sigmoid(g) * u).astype(jnp.bfloat16)
        part = jnp.dot(act, wd[f, :], preferred_element_type=jnp.float32)
        acc = part if acc is None else acc + part
    o_ref[...] = acc.astype(o_ref.dtype)


def _expert_ffn(xe, w_gate, w_up, w_down, layer, cap, cpad, tm):
    _, e, d, d_ff = w_gate.shape
    tiles = cap // tm
    assert d % tiles == 0 and (d // tiles) % 16 == 0
    any_spec = pl.BlockSpec(memory_space=pl.ANY)
    return pl.pallas_call(
        functools.partial(_ffn_kernel, layer, cpad, tiles, tm),
        grid=(e, tiles),
        in_specs=[any_spec, any_spec, any_spec, any_spec],
        out_specs=pl.BlockSpec((tm, d), lambda i, j: (i * tiles + j, 0)),
        out_shape=jax.ShapeDtypeStruct((e * cap, d), jnp.bfloat16),
        scratch_shapes=[
            pltpu.VMEM((2, tm, d // 2), jnp.uint32),
            pltpu.VMEM((2, d, d_ff), jnp.bfloat16),
            pltpu.VMEM((2, d, d_ff), jnp.bfloat16),
            pltpu.VMEM((2, d_ff, d), jnp.bfloat16),
            pltpu.VMEM((2, d // tiles, d_ff), jnp.float32),
            pltpu.VMEM((2, d // tiles, d_ff), jnp.float32),
            pltpu.VMEM((2, d_ff // tiles, d), jnp.float32),
            pltpu.SemaphoreType.DMA((2,)),
            pltpu.SemaphoreType.DMA((2,)),
        ],
        compiler_params=pltpu.CompilerParams(
            dimension_semantics=("arbitrary", "arbitrary"),
            vmem_limit_bytes=_vmem_limit(58 * 1024 * 1024)),
        name="expert_ffn",
    )(xe, w_gate, w_up, w_down)


def _combine_kernel(cap, nt, w, lo_ref, x_ref, rel_ref, aff_ref, ye_hbm, o_ref,
                    ybuf, ybuf_x, sem, sem_x):
    j = pl.program_id(0)
    slot = j % 2
    last_start = N_EXPERTS * cap - RT_CH
    align = 16

    def start_row(e, jj, c):
        lo = lo_ref[e * w + jj]
        a = e * cap + lo - (lo & (align - 1)) + c * RT_CH
        return pl.multiple_of(jnp.minimum(a, last_start), align)

    def fetch(e, jj, sl):
        return pltpu.make_async_copy(
            ye_hbm.at[pl.ds(start_row(e, jj, 0), RT_CH)],
            ybuf.at[sl, pl.ds(e * RT_CH, RT_CH)], sem.at[sl])

    @pl.when(j == 0)
    def _():
        for e in range(N_EXPERTS):
            fetch(e, 0, 0).start()

    @pl.when(j + 1 < nt)
    def _():
        for e in range(N_EXPERTS):
            fetch(e, j + 1, 1 - slot).start()

    los =[lo_ref[e * w + j] for e in range(N_EXPERTS)]
    cnts = [lo_ref[e * w + j + 1] - los[e] for e in range(N_EXPERTS)]
    lead = [los[e] & (align - 1) for e in range(N_EXPERTS)]

    def weight_block(e, c):
        kio = lax.broadcasted_iota(jnp.int32, (RT_CH, RT_T), 0)
        r = rel_ref[e:e + 1, :]
        p = r + lead[e]
        member = (r >= 0) & (p >= c * RT_CH) & (p < (c + 1) * RT_CH)
        off = e * cap + los[e] - start_row(e, j, c)
        hit = member & ((r + off) == kio)
        return jnp.where(hit, aff_ref[e:e + 1, :], 0.0).astype(jnp.bfloat16)

    tn = (((0,), (0,)), ((), ()))
    wt0 = jnp.concatenate([weight_block(e, 0) for e in range(N_EXPERTS)], axis=0)
    for e in range(N_EXPERTS):
        fetch(e, j, slot).wait()
    o_ref[...] = x_ref[...] + lax.dot_general(
        wt0, ybuf[slot], tn, preferred_element_type=jnp.float32).reshape(o_ref.shape)

    spans = [lead[e] + cnts[e] for e in range(N_EXPERTS)]
    most = functools.reduce(jnp.maximum, spans)
    n_pass = jnp.right_shift(most + (RT_CH - 1), RT_CH.bit_length() - 1)

    def extra(c, carry):
        def extra_fetch(e):
            return pltpu.make_async_copy(
                ye_hbm.at[pl.ds(start_row(e, j, c), RT_CH)],
                ybuf_x.at[pl.ds(e * RT_CH, RT_CH)], sem_x)

        for e in range(N_EXPERTS):
            @pl.when(spans[e] > c * RT_CH)
            def _():
                extra_fetch(e).start()
        for e in range(N_EXPERTS):
            @pl.when(spans[e] > c * RT_CH)
            def _():
                extra_fetch(e).wait()
        for e in range(N_EXPERTS):
            @pl.when(spans[e] > c * RT_CH)
            def _():
                wte = weight_block(e, c)
                o_ref[...] += lax.dot_general(
                    wte, ybuf_x[e * RT_CH:(e + 1) * RT_CH, :], tn,
                    preferred_element_type=jnp.float32).reshape(o_ref.shape)
        return carry

    lax.fori_loop(1, n_pass, extra, 0)


def _combine(x4, rel_t, aff_t, ye, lo_flat, cap):
    _, nt, _, d = x4.shape
    w = lo_flat.shape[0] // N_EXPERTS
    rows = N_EXPERTS * RT_CH
    return pl.pallas_call(
        functools.partial(_combine_kernel, cap, nt, w),
        grid_spec=pltpu.PrefetchScalarGridSpec(
            num_scalar_prefetch=1,
            grid=(nt,),
            in_specs=[
                pl.BlockSpec((RT_T // RT_G, 1, RT_G, d), lambda j, lo: (0, j, 0, 0)),
                pl.BlockSpec((N_EXPERTS, RT_T), lambda j, lo: (0, j)),
                pl.BlockSpec((N_EXPERTS, RT_T), lambda j, lo: (0, j)),
                pl.BlockSpec(memory_space=pl.ANY),
            ],
            out_specs=pl.BlockSpec((RT_T // RT_G, 1, RT_G, d), lambda j, lo: (0, j, 0, 0)),
            scratch_shapes=[
                pltpu.VMEM((2, rows, d), jnp.bfloat16),
                pltpu.VMEM((rows, d), jnp.bfloat16),
                pltpu.SemaphoreType.DMA((2,)),
                pltpu.SemaphoreType.DMA(()),
            ],
        ),
        out_shape=jax.ShapeDtypeStruct(x4.shape, jnp.float32),
        compiler_params=pltpu.CompilerParams(
            dimension_semantics=("arbitrary",),
            vmem_limit_bytes=_vmem_limit(40 * 1024 * 1024)),
        name="combine",
    )(lo_flat, x4, rel_t, aff_t, ye)


def _window_bias_t():
    j = jnp.arange(3 * WIN_SUB)[:, None]
    i = jnp.arange(WIN_SUB)[None, :]
    dist = jnp.abs(i + WIN_SUB - j).astype(jnp.float32)
    slopes = jnp.exp2(-8.0 * (jnp.arange(N_HEADS_A, dtype=jnp.float32) + 1.0) / N_HEADS_A)
    b = jnp.where(dist[None] <= WINDOW, -(slopes[:, None, None] * dist[None]), NEG)
    return jnp.transpose(b, (1, 0, 2)).reshape(3 * WIN_SUB, N_HEADS_A * WIN_SUB) * LOG2E


def _na_bias_t(rpb):
    kk = jnp.arange(NA_KEY_ROWS)[:, None]
    rho = jnp.arange(NA_GROUP_ROWS)[None, :]
    rel = kk - NA_GROUP_ROWS
    r0 = jnp.stack([
        jnp.zeros_like(rho),
        rho - NA_ROWS // 2,
        jnp.full_like(rho, NA_GROUP_ROWS - NA_ROWS),
    ])
    row_ok = (rel[None] >= r0) & (rel[None] < r0 + NA_ROWS)
    dr = jnp.clip(rel - rho + (NA_ROWS - 1), 0, 2 * NA_ROWS - 2)
    ck = jnp.arange(GRID_W)[:, None]
    cq = jnp.arange(GRID_W)[None, :]
    c0 = jnp.clip(cq - NA_COLS // 2, 0, GRID_W - NA_COLS)
    col_ok = (ck >= c0) & (ck < c0 + NA_COLS)
    dc = jnp.clip(ck - cq + (NA_COLS - 1), 0, 2 * NA_COLS - 2)
    hi = lax.Precision.HIGHEST
    oh_r = (dr[:, :, None] == jnp.arange(2 * NA_ROWS - 1)).astype(jnp.float32)
    oh_c = (dc[None] == jnp.arange(2 * NA_COLS - 1)[:, None, None]).astype(jnp.float32)
    rows = jnp.einsum('krs,hsd->hkrd', oh_r, rpb.astype(jnp.float32), precision=hi)
    vals = jnp.einsum('hkrd,dcq->hkcrq', rows, oh_c, precision=hi)
    ok = row_ok[:, :, None, :, None] & col_ok[None, None, :, None, :]
    b = jnp.where(ok[:, None], vals[None], NEG)
    h = rpb.shape[0]
    b = b.reshape(3, h // 2, 2, NA_KEY_ROWS * GRID_W, NA_TQ)
    b = jnp.transpose(b, (0, 1, 3, 2, 4)).reshape(3, h // 2, NA_KEY_ROWS * GRID_W, 2 * NA_TQ)
    return b * LOG2E


def _layer_params(p, l):
    scale = LOG2E / math.sqrt(HEAD_DIM)
    gains = jnp.concatenate([p["qnorm_a"][l] * scale, p["knorm_a"][l],
                             p["qnorm_b"][l] * scale, p["knorm_b"][l]])
    wr = p["w_router"][l].T
    wr_hi = wr.astype(jnp.bfloat16)
    return dict(
        g_mix=p["norm_mix"][l][None, :],
        w_in_t=p["w_in"][l].T.astype(jnp.bfloat16),
        head_gains=jnp.broadcast_to(gains[:, None], (4 * HEAD_DIM, PROJ_TM)),
        sink_row=jnp.repeat(p["sink_a"][l].astype(jnp.float32) * LOG2E, WIN_SUB)[None, :],
        na_bias=_na_bias_t(p["rpb_b"][l]),
        g_a=p["onorm_a"][l][None, :],
        g_b=p["onorm_b"][l][None, :],
        w_out=p["w_out"][l].astype(jnp.bfloat16),
        g_ffn=p["norm_ffn"][l][None, :],
        wr_hi=wr_hi,
        wr_lo=(wr - wr_hi.astype(jnp.float32)).astype(jnp.bfloat16),
        layer=l,
        w_gate=p["w_gate"],
        w_up=p["w_up"],
        w_down=p["w_down"],
    )


def _trunk(x, layers, win_bias):
    b, s, d = x.shape
    n = b * s
    assert s % WIN_TQ == 0 and s // NA_TQ >= 3 and n % PROJ_TM == 0 and n % RT_T == 0
    cap = EC_CAPACITY * n // N_EXPERTS
    tm = min(FFN_TM, cap)
    assert cap % tm == 0 and tm % RT_CH == 0
    cpad = cap + tm
    x2 = x.reshape(n, d)
    for q in layers:
        qa_t, ka, va_t, qb_t, kb, vb_t = _in_proj(x2, q["g_mix"], q["w_in_t"], q["head_gains"])
        out_a = _window_attention(qa_t, ka, va_t, win_bias, q["sink_row"], s)
        out_b = _na_attention(qb_t, kb, vb_t, q["na_bias"], s)
        x1, h, aff_t = _post_attn(out_a, out_b, x2, q["g_a"], q["g_b"], q["w_out"],
                                  q["g_ffn"], q["wr_hi"], q["wr_lo"])
        nt = n // RT_T
        sel_t = _select(aff_t, cap)
        rel_t, lo = _rank(_tile_major(sel_t, nt))
        lo_flat = lo.reshape(-1)
        xe = _dispatch(_granule_view(h, nt), rel_t, lo_flat, cap, cpad)
        ye = _expert_ffn(xe, q["w_gate"], q["w_up"], q["w_down"], q["layer"], cap, cpad, tm)
        x2 = _combine(_granule_view(x1, nt), rel_t, _tile_major(aff_t, nt), ye,
                      lo_flat, cap).reshape(n, d)
    return x2.reshape(b, s, d)


def kernel(x_prompt, x_sample, norm_mix, w_in, qnorm_a, knorm_a, sink_a, qnorm_b, knorm_b,
           rpb_b, onorm_a, onorm_b, w_out, norm_ffn, w_router, w_gate, w_up, w_down):
    p = dict(norm_mix=norm_mix, w_in=w_in, qnorm_a=qnorm_a, knorm_a=knorm_a, sink_a=sink_a,
             qnorm_b=qnorm_b, knorm_b=knorm_b, rpb_b=rpb_b, onorm_a=onorm_a, onorm_b=onorm_b,
             w_out=w_out, norm_ffn=norm_ffn, w_router=w_router, w_gate=w_gate, w_up=w_up,
             w_down=w_down)
    layers = [_layer_params(p, l) for l in range(w_in.shape[0])]
    win_bias = _window_bias_t()
    return (_trunk(x_prompt, layers, win_bias), _trunk(x_sample, layers, win_bias))
```

```python
import functools
import math

import jax
import jax.numpy as jnp
from jax import lax
from jax.experimental import pallas as pl
from jax.experimental.pallas import tpu as pltpu

HEAD_DIM = 64
N_HEADS_A = 8
N_KV_HEADS_A = 2
N_HEADS_B = 8
QA_W = N_HEADS_A * HEAD_DIM
KVA_W = N_KV_HEADS_A * HEAD_DIM
QKVB_W = N_HEADS_B * HEAD_DIM
PROJ_W = QA_W + 2 * KVA_W + 3 * QKVB_W
WINDOW = 128
GRID_W = 64
NA_ROWS = 8
NA_COLS = 16
N_EXPERTS = 16
EC_CAPACITY = 2
EPS = 1e-6
NEG = -1e30
LOG2E = 1.4426950408889634

LANE = 128
V7X_VMEM_BYTES = 64 * 1024 * 1024

PROJ_TM = 512
WIN_TQ = 512
WIN_SUB = WINDOW
NA_GROUP_ROWS = 4
NA_TQ = NA_GROUP_ROWS * GRID_W
NA_KEY_ROWS = 3 * NA_GROUP_ROWS
NA_KC = 128
POST_TM = 512
FFN_TM = 512
FFN_TF = 512
RT_T = 256
RT_CH = 64
RT_G = 16
RETILE_ROWS = 2048
RANK_TILES = 8

_NT = (((1,), (1,)), ((), ()))


def _vmem_limit(nbytes):
    return int(min(nbytes, V7X_VMEM_BYTES - 4 * 1024 * 1024))


def _proj_kernel(x_ref, g_ref, w_ref, hg_ref,
                 qa_ref, ka_ref, va_ref, qb_ref, kb_ref, vb_ref):
    x = x_ref[...]
    ms = jnp.mean(x * x, axis=-1, keepdims=True)
    h = (x * lax.rsqrt(ms + EPS) * g_ref[...]).astype(jnp.bfloat16)

    def seg(lo, hi):
        return lax.dot_general(w_ref[lo:hi, :], h, _NT,
                               preferred_element_type=jnp.float32)

    def head_norm(blk, gain):
        ssq = jnp.sum(blk * blk, axis=0, keepdims=True)
        return blk * lax.rsqrt(ssq * (1.0 / HEAD_DIM) + EPS) * gain

    g_qa = hg_ref[0 * HEAD_DIM:1 * HEAD_DIM, :]
    g_ka = hg_ref[1 * HEAD_DIM:2 * HEAD_DIM, :]
    g_qb = hg_ref[2 * HEAD_DIM:3 * HEAD_DIM, :]
    g_kb = hg_ref[3 * HEAD_DIM:4 * HEAD_DIM, :]

    o = 0
    p = seg(o, o + QA_W)
    for hd in range(N_HEADS_A):
        r = slice(hd * HEAD_DIM, (hd + 1) * HEAD_DIM)
        qa_ref[r, :] = head_norm(p[r, :], g_qa).astype(qa_ref.dtype)
    o += QA_W
    p = seg(o, o + 2 * KVA_W)
    kn = jnp.concatenate(
        [head_norm(p[hd * HEAD_DIM:(hd + 1) * HEAD_DIM, :], g_ka)
         for hd in range(N_KV_HEADS_A)], axis=0)
    ka_ref[...] = kn.T.astype(ka_ref.dtype)
    va_ref[...] = p[KVA_W:2 * KVA_W, :].astype(va_ref.dtype)
    o += 2 * KVA_W
    p = seg(o, o + QKVB_W)
    for hd in range(N_HEADS_B):
        r = slice(hd * HEAD_DIM, (hd + 1) * HEAD_DIM)
        qb_ref[r, :] = head_norm(p[r, :], g_qb).astype(qb_ref.dtype)
    o += QKVB_W
    p = seg(o, o + QKVB_W)
    kn = jnp.concatenate(
        [head_norm(p[hd * HEAD_DIM:(hd + 1) * HEAD_DIM, :], g_kb)
         for hd in range(N_HEADS_B)], axis=0)
    kb_ref[...] = kn.T.astype(kb_ref.dtype)
    o += QKVB_W
    vb_ref[...] = seg(o, o + QKVB_W).astype(vb_ref.dtype)


def _in_proj(x2d, g_mix, w_in_t, head_gains):
    n, d = x2d.shape
    tm = PROJ_TM
    bf = jnp.bfloat16
    col = lambda i: (0, i)
    row = lambda i: (i, 0)
    const = lambda i: (0, 0)
    out_shape = (
        jax.ShapeDtypeStruct((QA_W, n), bf),
        jax.ShapeDtypeStruct((n, KVA_W), bf),
        jax.ShapeDtypeStruct((KVA_W, n), bf),
        jax.ShapeDtypeStruct((QKVB_W, n), bf),
        jax.ShapeDtypeStruct((n, QKVB_W), bf),
        jax.ShapeDtypeStruct((QKVB_W, n), bf),
    )
    out_specs = (
        pl.BlockSpec((QA_W, tm), col),
        pl.BlockSpec((tm, KVA_W), row),
        pl.BlockSpec((KVA_W, tm), col),
        pl.BlockSpec((QKVB_W, tm), col),
        pl.BlockSpec((tm, QKVB_W), row),
        pl.BlockSpec((QKVB_W, tm), col),
    )
    return pl.pallas_call(
        _proj_kernel,
        grid=(n // tm,),
        in_specs=[
            pl.BlockSpec((tm, d), row),
            pl.BlockSpec((1, d), const),
            pl.BlockSpec((PROJ_W, d), const),
            pl.BlockSpec((4 * HEAD_DIM, tm), const),
        ],
        out_specs=out_specs,
        out_shape=out_shape,
        compiler_params=pltpu.CompilerParams(
            dimension_semantics=("arbitrary",),
            vmem_limit_bytes=_vmem_limit(48 * 1024 * 1024)),
        name="in_proj",
    )(x2d, g_mix, w_in_t, head_gains)


def _fold(acc, v, op):
    return v if acc is None else op(acc, v)


def _window_kernel(blocks_per_seq, q_ref, kp_ref, kc_ref, kn_ref,
                   vp_ref, vc_ref, vn_ref, bias_ref, sink_ref, o_ref, s_scr, p_scr):
    i = pl.program_id(0)
    pos = i % blocks_per_seq
    pen_prev = jnp.where(pos == 0, NEG, 0.0).astype(jnp.float32)
    pen_next = jnp.where(pos == blocks_per_seq - 1, NEG, 0.0).astype(jnp.float32)

    sink = sink_ref[...]
    n_sub = WIN_TQ // WIN_SUB
    n_chunks = 3
    gq = N_HEADS_A // N_KV_HEADS_A
    zero = jnp.zeros((HEAD_DIM, WIN_SUB), jnp.bfloat16)
    krefs = (kp_ref, kc_ref, kn_ref)

    def key_block(kb):
        if kb == 0:
            return 0, 0
        if kb == n_sub + 1:
            return 2, 0
        return 1, (kb - 1) * WIN_SUB

    def qblock(j):
        cols = slice(j * WIN_SUB, (j + 1) * WIN_SUB)
        halves = []
        for kv in range(N_KV_HEADS_A):
            parts = []
            for hd in range(N_HEADS_A):
                if hd // gq == kv:
                    parts.append(q_ref[hd * HEAD_DIM:(hd + 1) * HEAD_DIM, cols])
                else:
                    parts.append(zero)
            halves.append(jnp.concatenate(parts, axis=1))
        return jnp.concatenate(halves, axis=0)

    def score_chunk(j, c, qblk):
        rows = slice(c * WIN_SUB, (c + 1) * WIN_SUB)
        r, off = key_block(j + c)
        s = jnp.dot(krefs[r][off:off + WIN_SUB, :], qblk,
                    preferred_element_type=jnp.float32)
        s = s + bias_ref[rows, :]
        if j + c == 0:
            s = s + pen_prev
        if j + c == n_sub + 1:
            s = s + pen_next
        s_scr[j % 2, rows, :] = s
        return jnp.max(s, axis=0, keepdims=True)

    def prob_chunk(j, c, m):
        rows = slice(c * WIN_SUB, (c + 1) * WIN_SUB)
        p_scr[j % 2, rows, :] = jnp.exp2(s_scr[j % 2, rows, :] - m).astype(jnp.bfloat16)

    def finish(j, m):
        cols = slice(j * WIN_SUB, (j + 1) * WIN_SUB)
        vparts = []
        for c in range(n_chunks):
            r, off = key_block(j + c)
            vparts.append((vp_ref, vc_ref, vn_ref)[r][:, off:off + WIN_SUB])
        vwin = jnp.concatenate(vparts, axis=1)
        ones = jnp.ones((16, n_chunks * WIN_SUB), jnp.bfloat16)
        sink_term = jnp.exp2(sink - m)
        outs = []
        for kv in range(N_KV_HEADS_A):
            lanes = slice(kv * gq * WIN_SUB, (kv + 1) * gq * WIN_SUB)
            vt = jnp.concatenate([vwin[kv * HEAD_DIM:(kv + 1) * HEAD_DIM, :], ones], axis=0)
            o_t = jnp.dot(vt, p_scr[j % 2, :, lanes],
                          preferred_element_type=jnp.float32)
            o_t = o_t[:HEAD_DIM] / (o_t[HEAD_DIM:HEAD_DIM + 1] + sink_term[:, lanes])
            for g in range(gq):
                outs.append(o_t[:, g * WIN_SUB:(g + 1) * WIN_SUB])
        for a in range(N_HEADS_A // 2):
            pair = jnp.concatenate([outs[2 * a], outs[2 * a + 1]], axis=0)
            o_ref[cols, a * LANE:(a + 1) * LANE] = pair.T.astype(o_ref.dtype)

    qb = qblock(0)
    m = None
    for c in range(n_chunks):
        m = _fold(m, score_chunk(0, c, qb), jnp.maximum)
    m = jnp.maximum(m, sink)
    m_done = None
    for j in range(n_sub):
        m_next = None
        if j + 1 < n_sub:
            qb = qblock(j + 1)
        for c in range(n_chunks):
            prob_chunk(j, c, m)
            if j + 1 < n_sub:
                m_next = _fold(m_next, score_chunk(j + 1, c, qb), jnp.maximum)
            if c == 0 and j >= 1:
                finish(j - 1, m_done)
        m_done = m
        if j + 1 < n_sub:
            m = jnp.maximum(m_next, sink)
    finish(n_sub - 1, m_done)


def _window_attention(qa_t, ka, va_t, bias_t, sink_row, seq_len):
    n = ka.shape[0]
    nblk = n // WIN_TQ
    bps = seq_len // WIN_TQ
    r = WIN_TQ // WIN_SUB
    nsub = n // WIN_SUB
    prev_i = lambda i: jnp.maximum(r * i - 1, 0)
    next_i = lambda i: jnp.minimum(r * i + r, nsub - 1)
    const = lambda i: (0, 0)
    return pl.pallas_call(
        functools.partial(_window_kernel, bps),
        grid=(nblk,),
        in_specs=[
            pl.BlockSpec((QA_W, WIN_TQ), lambda i: (0, i)),
            pl.BlockSpec((WIN_SUB, KVA_W), lambda i: (prev_i(i), 0)),
            pl.BlockSpec((WIN_TQ, KVA_W), lambda i: (i, 0)),
            pl.BlockSpec((WIN_SUB, KVA_W), lambda i: (next_i(i), 0)),
            pl.BlockSpec((KVA_W, WIN_SUB), lambda i: (0, prev_i(i))),
            pl.BlockSpec((KVA_W, WIN_TQ), lambda i: (0, i)),
            pl.BlockSpec((KVA_W, WIN_SUB), lambda i: (0, next_i(i))),
            pl.BlockSpec((3 * WIN_SUB, N_HEADS_A * WIN_SUB), const),
            pl.BlockSpec((1, N_HEADS_A * WIN_SUB), const),
        ],
        out_specs=pl.BlockSpec((WIN_TQ, QA_W), lambda i: (i, 0)),
        out_shape=jax.ShapeDtypeStruct((n, QA_W), jnp.bfloat16),
        scratch_shapes=[
            pltpu.VMEM((2, 3 * WIN_SUB, N_HEADS_A * WIN_SUB), jnp.float32),
            pltpu.VMEM((2, 3 * WIN_SUB, N_HEADS_A * WIN_SUB), jnp.bfloat16),
        ],
        compiler_params=pltpu.CompilerParams(
            dimension_semantics=("arbitrary",),
            vmem_limit_bytes=_vmem_limit(40 * 1024 * 1024)),
        name="window_attn",
    )(qa_t, ka, ka, ka, va_t, va_t, va_t, bias_t, sink_row)


def _na_kernel(q_ref, kp_ref, kc_ref, kn_ref, vp_ref, vc_ref, vn_ref,
               bias_ref, o_ref, s_scr, p_scr):
    zero = jnp.zeros((HEAD_DIM, NA_TQ), jnp.bfloat16)
    n_keys = NA_KEY_ROWS * GRID_W
    n_chunks = n_keys // NA_KC
    n_pairs = N_HEADS_B // 2
    krefs = (kp_ref, kc_ref, kn_ref)

    def qblock(pr):
        q0 = q_ref[(2 * pr) * HEAD_DIM:(2 * pr + 1) * HEAD_DIM, :]
        q1 = q_ref[(2 * pr + 1) * HEAD_DIM:(2 * pr + 2) * HEAD_DIM, :]
        return jnp.concatenate(
            [jnp.concatenate([q0, zero], axis=1),
             jnp.concatenate([zero, q1], axis=1)], axis=0)

    def score_chunk(pr, c, qblk):
        rows = slice(c * NA_KC, (c + 1) * NA_KC)
        blk, off = divmod(c * NA_KC, NA_TQ)
        kchunk = krefs[blk][off:off + NA_KC, pr * LANE:(pr + 1) * LANE]
        s = jnp.dot(kchunk, qblk, preferred_element_type=jnp.float32)
        s = s + bias_ref[0, pr, rows, :]
        s_scr[pr % 2, rows, :] = s
        return jnp.max(s, axis=0, keepdims=True)

    def prob_chunk(pr, c, m):
        rows = slice(c * NA_KC, (c + 1) * NA_KC)
        p_scr[pr % 2, rows, :] = jnp.exp2(s_scr[pr % 2, rows, :] - m).astype(jnp.bfloat16)

    def finish(pr):
        lanes = slice(pr * LANE, (pr + 1) * LANE)
        vwin = jnp.concatenate(
            [vp_ref[lanes, :], vc_ref[lanes, :], vn_ref[lanes, :]], axis=1)
        ones = jnp.ones((16, n_keys), jnp.bfloat16)
        outs = []
        for t in range(2):
            cols = slice(t * NA_TQ, (t + 1) * NA_TQ)
            vt = jnp.concatenate([vwin[t * HEAD_DIM:(t + 1) * HEAD_DIM, :], ones], axis=0)
            o_t = jnp.dot(vt, p_scr[pr % 2, :, cols],
                          preferred_element_type=jnp.float32)
            outs.append(o_t[:HEAD_DIM] / o_t[HEAD_DIM:HEAD_DIM + 1])
        pair = jnp.concatenate(outs, axis=0)
        o_ref[:, lanes] = pair.T.astype(o_ref.dtype)

    qb = qblock(0)
    m = None
    for c in range(n_chunks):
        m = _fold(m, score_chunk(0, c, qb), jnp.maximum)
    for pr in range(n_pairs):
        m_next = None
        if pr + 1 < n_pairs:
            qb = qblock(pr + 1)
        for c in range(n_chunks):
            prob_chunk(pr, c, m)
            if pr + 1 < n_pairs:
                m_next = _fold(m_next, score_chunk(pr + 1, c, qb), jnp.maximum)
            if c == 0 and pr >= 1:
                finish(pr - 1)
        m = m_next
    finish(n_pairs - 1)


def _na_attention(qb_t, kb, vb_t, bias, seq_len):
    n = kb.shape[0]
    ng = n // NA_TQ
    gps = seq_len // NA_TQ
    prev_i = lambda g: jnp.maximum(g - 1, 0)
    next_i = lambda g: jnp.minimum(g + 1, ng - 1)

    def variant(g):
        pos = g % gps
        return jnp.where(pos == 0, 0, jnp.where(pos == gps - 1, 2, 1))

    return pl.pallas_call(
        _na_kernel,
        grid=(ng,),
        in_specs=[
            pl.BlockSpec((QKVB_W, NA_TQ), lambda g: (0, g)),
            pl.BlockSpec((NA_TQ, QKVB_W), lambda g: (prev_i(g), 0)),
            pl.BlockSpec((NA_TQ, QKVB_W), lambda g: (g, 0)),
            pl.BlockSpec((NA_TQ, QKVB_W), lambda g: (next_i(g), 0)),
            pl.BlockSpec((QKVB_W, NA_TQ), lambda g: (0, prev_i(g))),
            pl.BlockSpec((QKVB_W, NA_TQ), lambda g: (0, g)),
            pl.BlockSpec((QKVB_W, NA_TQ), lambda g: (0, next_i(g))),
            pl.BlockSpec((1, N_HEADS_B // 2, NA_KEY_ROWS * GRID_W, 2 * NA_TQ),
                         lambda g: (variant(g), 0, 0, 0)),
        ],
        out_specs=pl.BlockSpec((NA_TQ, QKVB_W), lambda g: (g, 0)),
        out_shape=jax.ShapeDtypeStruct((n, QKVB_W), jnp.bfloat16),
        scratch_shapes=[
            pltpu.VMEM((2, NA_KEY_ROWS * GRID_W, 2 * NA_TQ), jnp.float32),
            pltpu.VMEM((2, NA_KEY_ROWS * GRID_W, 2 * NA_TQ), jnp.bfloat16),
        ],
        compiler_params=pltpu.CompilerParams(
            dimension_semantics=("arbitrary",),
            vmem_limit_bytes=_vmem_limit(48 * 1024 * 1024)),
        name="na_attn",
    )(qb_t, kb, kb, kb, vb_t, vb_t, vb_t, bias)


def _post_kernel(a_ref, b_ref, x_ref, ga_ref, gb_ref, w_ref, gf_ref,
                 wrh_ref, wrl_ref, x1_ref, h_ref, aff_ref):
    def rms(v, g):
        ms = jnp.mean(v * v, axis=-1, keepdims=True)
        return v * lax.rsqrt(ms + EPS) * g

    an = rms(a_ref[...].astype(jnp.float32), ga_ref[...]).astype(jnp.bfloat16)
    bn = rms(b_ref[...].astype(jnp.float32), gb_ref[...]).astype(jnp.bfloat16)
    y = jnp.dot(an, w_ref[:QA_W, :], preferred_element_type=jnp.float32)
    y = y + jnp.dot(bn, w_ref[QA_W:, :], preferred_element_type=jnp.float32)
    x1 = x_ref[...] + y
    x1_ref[...] = x1
    h = rms(x1, gf_ref[...])
    h_hi = h.astype(jnp.bfloat16)
    h_lo = (h - h_hi.astype(jnp.float32)).astype(jnp.bfloat16)
    h_ref[...] = h_hi
    wrh = wrh_ref[...]
    logits = lax.dot_general(wrh, h_hi, _NT, preferred_element_type=jnp.float32)
    logits = logits + lax.dot_general(wrh, h_lo, _NT, preferred_element_type=jnp.float32)
    logits = logits + lax.dot_general(wrl_ref[...], h_hi, _NT,
                                      preferred_element_type=jnp.float32)
    m = jnp.max(logits, axis=0, keepdims=True)
    e = jnp.exp(logits - m)
    aff_ref[...] = e / jnp.sum(e, axis=0, keepdims=True)


def _post_attn(out_a, out_b, x2d, g_a, g_b, w_out, g_ffn, wr_hi, wr_lo):
    n, d = x2d.shape
    tm = POST_TM
    row = lambda i: (i, 0)
    const = lambda i: (0, 0)
    return pl.pallas_call(
        _post_kernel,
        grid=(n // tm,),
        in_specs=[
            pl.BlockSpec((tm, QA_W), row),
            pl.BlockSpec((tm, QKVB_W), row),
            pl.BlockSpec((tm, d), row),
            pl.BlockSpec((1, QA_W), const),
            pl.BlockSpec((1, QKVB_W), const),
            pl.BlockSpec((QA_W + QKVB_W, d), const),
            pl.BlockSpec((1, d), const),
            pl.BlockSpec((N_EXPERTS, d), const),
            pl.BlockSpec((N_EXPERTS, d), const),
        ],
        out_specs=(
            pl.BlockSpec((tm, d), row),
            pl.BlockSpec((tm, d), row),
            pl.BlockSpec((N_EXPERTS, tm), lambda i: (0, i)),
        ),
        out_shape=(
            jax.ShapeDtypeStruct((n, d), jnp.float32),
            jax.ShapeDtypeStruct((n, d), jnp.bfloat16),
            jax.ShapeDtypeStruct((N_EXPERTS, n), jnp.float32),
        ),
        compiler_params=pltpu.CompilerParams(
            dimension_semantics=("arbitrary",),
            vmem_limit_bytes=_vmem_limit(40 * 1024 * 1024)),
        name="post_attn",
    )(out_a, out_b, x2d, g_a, g_b, w_out, g_ffn, wr_hi, wr_lo)


def _strict_upper(n):
    r = lax.broadcasted_iota(jnp.int32, (n, n), 0)
    c = lax.broadcasted_iota(jnp.int32, (n, n), 1)
    return jnp.where(r < c, 1.0, 0.0).astype(jnp.bfloat16)


def _select_kernel(cap, aff_ref, sel_ref):
    n = aff_ref.shape[1]
    nt = n // RT_T
    cap_f = jnp.float32(cap)

    def count(mask):
        return jnp.sum(jnp.where(mask, 1.0, 0.0), axis=1, keepdims=True)

    def search(b, ans):
        cand = ans | jnp.left_shift(jnp.int32(1), 30 - b)
        bits = pltpu.bitcast(aff_ref[...], jnp.int32)
        return jnp.where(count(bits >= cand) >= cap_f, cand, ans)

    thr = lax.fori_loop(0, 31, search, jnp.zeros((N_EXPERTS, 1), jnp.int32))
    need = cap_f - count(pltpu.bitcast(aff_ref[...], jnp.int32) > thr)
    tri = _strict_upper(RT_T)

    def tile(c, run_eq):
        start = pl.multiple_of(c * RT_T, RT_T)
        bits = pltpu.bitcast(aff_ref[:, pl.ds(start, RT_T)], jnp.int32)
        eq = bits == thr
        eq_b = jnp.where(eq, 1.0, 0.0).astype(jnp.bfloat16)
        eq_rank = jnp.dot(eq_b, tri, preferred_element_type=jnp.float32)
        sel = (bits > thr) | (eq & (run_eq + eq_rank < need))
        sel_ref[:, pl.ds(start, RT_T)] = jnp.where(sel, aff_ref[:, pl.ds(start, RT_T)], -1.0)
        return run_eq + count(eq)

    lax.fori_loop(0, nt, tile, need * 0.0)


def _retile_kernel(m_ref, o_ref):
    pad = jnp.zeros((LANE - N_EXPERTS, RT_T), jnp.float32)
    for q in range(m_ref.shape[1] // RT_T):
        cols = slice(q * RT_T, (q + 1) * RT_T)
        o_ref[cols, :] = jnp.concatenate([m_ref[:, cols], pad], axis=0).T


def _tile_columns(m_ref, t):
    rows = m_ref[:, t].reshape(RT_T, LANE)
    return rows.T[:N_EXPERTS, :]


def _rank_kernel(nt, m_ref, rel_ref, lo_ref, run_ref):
    jb = pl.program_id(0)
    tb = m_ref.shape[1]
    w = lo_ref.shape[1]
    tri = _strict_upper(RT_T)
    lane = lax.broadcasted_iota(jnp.int32, (N_EXPERTS, w), 1)

    @pl.when(jb == 0)
    def _():
        lo_ref[...] = jnp.zeros_like(lo_ref)
        run_ref[...] = jnp.zeros_like(run_ref)

    for t in range(tb):
        sel = _tile_columns(m_ref, t) >= 0.0
        sel_f = jnp.where(sel, 1.0, 0.0)
        rank = jnp.dot(sel_f.astype(jnp.bfloat16), tri, preferred_element_type=jnp.float32)
        rel_ref[:, t * RT_T:(t + 1) * RT_T] = jnp.where(sel, rank, -1.0).astype(jnp.int32)
        run = run_ref[...]
        lo_ref[...] = jnp.where(lane == jb * tb + t, run.astype(jnp.int32), lo_ref[...])
        run_ref[...] = run + jnp.sum(sel_f, axis=1, keepdims=True)

    @pl.when(jb == pl.num_programs(0) - 1)
    def _():
        lo_ref[...] = jnp.where(lane >= nt, run_ref[...].astype(jnp.int32), lo_ref[...])


def _select(aff_t, cap):
    e, n = aff_t.shape
    full = lambda i: (0, 0)
    return pl.pallas_call(
        functools.partial(_select_kernel, cap),
        grid=(1,),
        in_specs=[pl.BlockSpec((e, n), full)],
        out_specs=pl.BlockSpec((e, n), full),
        out_shape=jax.ShapeDtypeStruct((e, n), jnp.float32),
        compiler_params=pltpu.CompilerParams(
            dimension_semantics=("arbitrary",),
            vmem_limit_bytes=_vmem_limit(40 * 1024 * 1024)),
        name="route_select",
    )(aff_t)


def _retile(m_t):
    e, n = m_t.shape
    rb = min(RETILE_ROWS, n)
    return pl.pallas_call(
        _retile_kernel,
        grid=(n // rb,),
        in_specs=[pl.BlockSpec((e, rb), lambda i: (0, i))],
        out_specs=pl.BlockSpec((rb, LANE), lambda i: (i, 0)),
        out_shape=jax.ShapeDtypeStruct((n, LANE), jnp.float32),
        compiler_params=pltpu.CompilerParams(dimension_semantics=("arbitrary",)),
        name="route_retile",
    )(m_t)


def _rank(m4):
    g, nt, rg, lanes = m4.shape
    tb = min(RANK_TILES, nt)
    w = nt + LANE
    return pl.pallas_call(
        functools.partial(_rank_kernel, nt),
        grid=(nt // tb,),
        in_specs=[pl.BlockSpec((g, tb, rg, lanes), lambda i: (0, i, 0, 0))],
        out_specs=(pl.BlockSpec((N_EXPERTS, tb * RT_T), lambda i: (0, i)),
                   pl.BlockSpec((N_EXPERTS, w), lambda i: (0, 0))),
        out_shape=(jax.ShapeDtypeStruct((N_EXPERTS, nt * RT_T), jnp.int32),
                   jax.ShapeDtypeStruct((N_EXPERTS, w), jnp.int32)),
        scratch_shapes=[pltpu.VMEM((N_EXPERTS, 1), jnp.float32)],
        compiler_params=pltpu.CompilerParams(dimension_semantics=("arbitrary",)),
        name="route_rank",
    )(m4)


def _granule_view(x, nt):
    n, d = x.shape
    return x.reshape(RT_T // RT_G, nt, RT_G, d)


def _pack_pairs(x):
    w = x.shape[1] // 2
    lo = pltpu.bitcast(x[:, :w], jnp.uint32)
    hi = pltpu.bitcast(x[:, w:], jnp.uint32)
    return lo | (hi >> 16)


def _unpack_pairs(p):
    lo = pltpu.bitcast(p & jnp.uint32(0xFFFF0000), jnp.float32).astype(jnp.bfloat16)
    hi = pltpu.bitcast(p << 16, jnp.float32).astype(jnp.bfloat16)
    return lo, hi


def _one_hot_rows(rel_ref, shift):
    kio = lax.broadcasted_iota(jnp.int32, (RT_CH, RT_T), 0)
    blocks = []
    for e in range(N_EXPERTS):
        hit = (rel_ref[e:e + 1, :] - shift) == kio
        blocks.append(jnp.where(hit, 1.0, 0.0).astype(jnp.bfloat16))
    return jnp.concatenate(blocks, axis=0)


def _dispatch_kernel(cap, cpad, nt, w, lo_ref, h_ref, rel_ref, xe_hbm,
                     stage, stage_x, sem, sem_x):
    j = pl.program_id(0)
    slot = j % 2

    def dst(e, jj, c):
        row = e * cpad + lo_ref[e * w + jj] + c * RT_CH
        return xe_hbm.at[pl.ds(row, RT_CH), 0]

    def chunk_copy(e, jj, sl):
        return pltpu.make_async_copy(
            stage.at[sl, pl.ds(e * RT_CH, RT_CH)], dst(e, jj, 0), sem.at[sl])

    @pl.when(j == 0)
    def _():
        pad = cpad - cap
        stage_x[...] = jnp.zeros_like(stage_x)
        fills = [pltpu.make_async_copy(
            stage_x.at[pl.ds(0, pad)], xe_hbm.at[pl.ds(e * cpad + cap, pad), 0], sem_x)
            for e in range(N_EXPERTS)]
        for f in fills:
            f.start()
        for f in fills:
            f.wait()

    h_tile = h_ref[...].reshape(RT_T, h_ref.shape[-1])
    x = jnp.dot(_one_hot_rows(rel_ref, 0), h_tile, preferred_element_type=jnp.float32)
    stage[slot] = _pack_pairs(x)

    @pl.when(j > 0)
    def _():
        for e in range(N_EXPERTS):
            chunk_copy(e, j - 1, 1 - slot).wait()

    for e in range(N_EXPERTS):
        chunk_copy(e, j, slot).start()

    cnts = [lo_ref[e * w + j + 1] - lo_ref[e * w + j] for e in range(N_EXPERTS)]
    most = functools.reduce(jnp.maximum, cnts)
    n_pass = jnp.right_shift(most + (RT_CH - 1), RT_CH.bit_length() - 1)

    def extra(c, carry):
        kio = lax.broadcasted_iota(jnp.int32, (RT_CH, RT_T), 0)

        def extra_copy(e):
            return pltpu.make_async_copy(
                stage_x.at[pl.ds(e * RT_CH, RT_CH)], dst(e, j, c), sem_x)

        for e in range(N_EXPERTS):
            @pl.when(cnts[e] > c * RT_CH)
            def _():
                hit = (rel_ref[e:e + 1, :] - c * RT_CH) == kio
                xx = jnp.dot(jnp.where(hit, 1.0, 0.0).astype(jnp.bfloat16),
                             h_ref[...].reshape(RT_T, h_ref.shape[-1]),
                             preferred_element_type=jnp.float32)
                stage_x[e * RT_CH:(e + 1) * RT_CH, :] = _pack_pairs(xx)
                extra_copy(e).start()
        for e in range(N_EXPERTS):
            @pl.when(cnts[e] > c * RT_CH)
            def _():
                extra_copy(e).wait()
        return carry

    lax.fori_loop(1, n_pass, extra, 0)

    @pl.when(j == nt - 1)
    def _():
        for e in range(N_EXPERTS):
            chunk_copy(e, j, slot).wait()


def _dispatch(h4, rel_t, lo_flat, cap, cpad):
    _, nt, _, d = h4.shape
    w = lo_flat.shape[0] // N_EXPERTS
    rows = N_EXPERTS * RT_CH
    assert cpad - cap <= rows
    return pl.pallas_call(
        functools.partial(_dispatch_kernel, cap, cpad, nt, w),
        grid_spec=pltpu.PrefetchScalarGridSpec(
            num_scalar_prefetch=1,
            grid=(nt,),
            in_specs=[
                pl.BlockSpec((RT_T // RT_G, 1, RT_G, d), lambda j, lo: (0, j, 0, 0)),
                pl.BlockSpec((N_EXPERTS, RT_T), lambda j, lo: (0, j)),
            ],
            out_specs=pl.BlockSpec(memory_space=pl.ANY),
            scratch_shapes=[
                pltpu.VMEM((2, rows, d // 2), jnp.uint32),
                pltpu.VMEM((rows, d // 2), jnp.uint32),
                pltpu.SemaphoreType.DMA((2,)),
                pltpu.SemaphoreType.DMA(()),
            ],
        ),
        out_shape=jax.ShapeDtypeStruct((N_EXPERTS * cpad, 1, d // 2), jnp.uint32),
        compiler_params=pltpu.CompilerParams(
            dimension_semantics=("arbitrary",),
            vmem_limit_bytes=_vmem_limit(40 * 1024 * 1024)),
        name="dispatch",
    )(lo_flat, h4, rel_t)


def _ffn_kernel(layer, cpad, tiles, tm, x_hbm, wg_hbm, wu_hbm, wd_hbm, o_ref,
                xbuf, wg_b, wu_b, wd_b, stg_g, stg_u, stg_d, xsem, wsem):
    i = pl.program_id(0)
    j = pl.program_id(1)
    n_exp = pl.num_programs(0)
    step = i * tiles + j
    slot = step % 2
    prev_slot = (step + 1) % 2
    rg = wg_b.shape[1] // tiles
    rd = wd_b.shape[1] // tiles

    def x_copy(ii, jj, sl):
        return pltpu.make_async_copy(
            x_hbm.at[pl.ds(ii * cpad + jj * tm, tm), 0], xbuf.at[sl], xsem.at[sl])

    def slab_copies(e, k, sl):
        r_g = pl.multiple_of(k * rg, rg)
        r_d = pl.multiple_of(k * rd, rd)
        return (
            pltpu.make_async_copy(wg_hbm.at[layer, e, pl.ds(r_g, rg)], stg_g.at[sl], wsem.at[sl]),
            pltpu.make_async_copy(wu_hbm.at[layer, e, pl.ds(r_g, rg)], stg_u.at[sl], wsem.at[sl]),
            pltpu.make_async_copy(wd_hbm.at[layer, e, pl.ds(r_d, rd)], stg_d.at[sl], wsem.at[sl]),
        )

    def cast_slab(wslot, k, sl):
        r_g = pl.multiple_of(k * rg, rg)
        r_d = pl.multiple_of(k * rd, rd)
        wg_b[wslot, pl.ds(r_g, rg), :] = stg_g[sl].astype(jnp.bfloat16)
        wu_b[wslot, pl.ds(r_g, rg), :] = stg_u[sl].astype(jnp.bfloat16)
        wd_b[wslot, pl.ds(r_d, rd), :] = stg_d[sl].astype(jnp.bfloat16)

    @pl.when(step == 0)
    def _():
        x_copy(0, 0, 0).start()

        def load(k, carry):
            sl = (k + tiles) % 2
            for cp in slab_copies(0, k, sl):
                cp.start()
            for cp in slab_copies(0, k, sl):
                cp.wait()
            cast_slab(0, k, sl)
            return carry

        lax.fori_loop(0, tiles, load, 0)

    j_prev = jnp.where(j > 0, j - 1, tiles - 1)
    e_prev = jnp.where(j > 0, i + 1, i)
    pending = jnp.logical_and(step > 0, e_prev < n_exp)

    @pl.when(pending)
    def _():
        for cp in slab_copies(e_prev, j_prev, prev_slot):
            cp.wait()

    @pl.when(i + 1 < n_exp)
    def _():
        for cp in slab_copies(i + 1, j, slot):
            cp.start()

    cast_slab(e_prev % 2, j_prev, prev_slot)

    @pl.when(step + 1 < n_exp * tiles)
    def _():
        wrap = j + 1 == tiles
        x_copy(jnp.where(wrap, i + 1, i), jnp.where(wrap, 0, j + 1), 1 - slot).start()

    x_copy(i, j, slot).wait()
    x_lo, x_hi = _unpack_pairs(xbuf[slot])
    half = x_lo.shape[1]
    d_ff = wg_b.shape[2]
    wg, wu, wd = wg_b.at[i % 2], wu_b.at[i % 2], wd_b.at[i % 2]
    acc = None
    for c in range(d_ff // FFN_TF):
        f = slice(c * FFN_TF, (c + 1) * FFN_TF)
        g = jnp.dot(x_lo, wg[:half, f], preferred_element_type=jnp.float32)
        g = g + jnp.dot(x_hi, wg[half:, f], preferred_element_type=jnp.float32)
        u = jnp.dot(x_lo, wu[:half, f], preferred_element_type=jnp.float32)
        u = u + jnp.dot(x_hi, wu[half:, f], preferred_element_type=jnp.float32)
        act = (g * jax.nn.sigmoid(g) * u).astype(jnp.bfloat16)
        part = jnp.dot(act, wd[f, :], preferred_element_type=jnp.float32)
        acc = part if acc is None else acc + part
    o_ref[...] = acc.astype(o_ref.dtype)


def _expert_ffn(xe, w_gate, w_up, w_down, layer, cap, cpad, tm):
    _, e, d, d_ff = w_gate.shape
    tiles = cap // tm
    assert d % tiles == 0 and (d // tiles) % 16 == 0
    any_spec = pl.BlockSpec(memory_space=pl.ANY)
    return pl.pallas_call(
        functools.partial(_ffn_kernel, layer, cpad, tiles, tm),
        grid=(e, tiles),
        in_specs=[any_spec, any_spec, any_spec, any_spec],
        out_specs=pl.BlockSpec((tm, d), lambda i, j: (i * tiles + j, 0)),
        out_shape=jax.ShapeDtypeStruct((e * cap, d), jnp.bfloat16),
        scratch_shapes=[
            pltpu.VMEM((2, tm, d // 2), jnp.uint32),
            pltpu.VMEM((2, d, d_ff), jnp.bfloat16),
            pltpu.VMEM((2, d, d_ff), jnp.bfloat16),
            pltpu.VMEM((2, d_ff, d), jnp.bfloat16),
            pltpu.VMEM((2, d // tiles, d_ff), jnp.float32),
            pltpu.VMEM((2, d // tiles, d_ff), jnp.float32),
            pltpu.VMEM((2, d_ff // tiles, d), jnp.float32),
            pltpu.SemaphoreType.DMA((2,)),
            pltpu.SemaphoreType.DMA((2,)),
        ],
        compiler_params=pltpu.CompilerParams(
            dimension_semantics=("arbitrary", "arbitrary"),
            vmem_limit_bytes=_vmem_limit(58 * 1024 * 1024)),
        name="expert_ffn",
    )(xe, w_gate, w_up, w_down)


def _combine_kernel(cap, nt, w, lo_ref, x_ref, rel_ref, m_ref, ye_hbm, o_ref,
                    ybuf, ybuf_x, gate_ref, sem, sem_x):
    j = pl.program_id(0)
    slot = j % 2
    last_start = N_EXPERTS * cap - RT_CH
    align = 16

    def start_row(e, jj, c):
        lo = lo_ref[e * w + jj]
        a = e * cap + lo - (lo & (align - 1)) + c * RT_CH
        return pl.multiple_of(jnp.minimum(a, last_start), align)

    def fetch(e, jj, sl):
        return pltpu.make_async_copy(
            ye_hbm.at[pl.ds(start_row(e, jj, 0), RT_CH)],
            ybuf.at[sl, pl.ds(e * RT_CH, RT_CH)], sem.at[sl])

    @pl.when(j == 0)
    def _():
        for e in range(N_EXPERTS):
            fetch(e, 0, 0).start()

    @pl.when(j + 1 < nt)
    def _():
        for e in range(N_EXPERTS):
            fetch(e, j + 1, 1 - slot).start()

    gate_ref[...] = _tile_columns(m_ref, 0)
    los = [lo_ref[e * w + j] for e in range(N_EXPERTS)]
    cnts = [lo_ref[e * w + j + 1] - los[e] for e in range(N_EXPERTS)]
    lead = [los[e] & (align - 1) for e in range(N_EXPERTS)]

    def weight_block(e, c):
        kio = lax.broadcasted_iota(jnp.int32, (RT_CH, RT_T), 0)
        r = rel_ref[e:e + 1, :]
        p = r + lead[e]
        member = (r >= 0) & (p >= c * RT_CH) & (p < (c + 1) * RT_CH)
        off = e * cap + los[e] - start_row(e, j, c)
        hit = member & ((r + off) == kio)
        return jnp.where(hit, gate_ref[e:e + 1, :], 0.0).astype(jnp.bfloat16)

    tn = (((0,), (0,)), ((), ()))
    wt0 = jnp.concatenate([weight_block(e, 0) for e in range(N_EXPERTS)], axis=0)
    for e in range(N_EXPERTS):
        fetch(e, j, slot).wait()
    o_ref[...] = x_ref[...] + lax.dot_general(
        wt0, ybuf[slot], tn, preferred_element_type=jnp.float32).reshape(o_ref.shape)

    spans = [lead[e] + cnts[e] for e in range(N_EXPERTS)]
    most = functools.reduce(jnp.maximum, spans)
    n_pass = jnp.right_shift(most + (RT_CH - 1), RT_CH.bit_length() - 1)

    def extra(c, carry):
        def extra_fetch(e):
            return pltpu.make_async_copy(
                ye_hbm.at[pl.ds(start_row(e, j, c), RT_CH)],
                ybuf_x.at[pl.ds(e * RT_CH, RT_CH)], sem_x)

        for e in range(N_EXPERTS):
            @pl.when(spans[e] > c * RT_CH)
            def _():
                extra_fetch(e).start()
        for e in range(N_EXPERTS):
            @pl.when(spans[e] > c * RT_CH)
            def _():
                extra_fetch(e).wait()
        for e in range(N_EXPERTS):
            @pl.when(spans[e] > c * RT_CH)
            def _():
                wte = weight_block(e, c)
                o_ref[...] += lax.dot_general(
                    wte, ybuf_x[e * RT_CH:(e + 1) * RT_CH, :], tn,
                    preferred_element_type=jnp.float32).reshape(o_ref.shape)
        return carry

    lax.fori_loop(1, n_pass, extra, 0)


def _combine(x4, rel_t, m4, ye, lo_flat, cap):
    _, nt, _, d = x4.shape
    w = lo_flat.shape[0] // N_EXPERTS
    rows = N_EXPERTS * RT_CH
    return pl.pallas_call(
        functools.partial(_combine_kernel, cap, nt, w),
        grid_spec=pltpu.PrefetchScalarGridSpec(
            num_scalar_prefetch=1,
            grid=(nt,),
            in_specs=[
                pl.BlockSpec((RT_T // RT_G, 1, RT_G, d), lambda j, lo: (0, j, 0, 0)),
                pl.BlockSpec((N_EXPERTS, RT_T), lambda j, lo: (0, j)),
                pl.BlockSpec((RT_T // RT_G, 1, RT_G, LANE), lambda j, lo: (0, j, 0, 0)),
                pl.BlockSpec(memory_space=pl.ANY),
            ],
            out_specs=pl.BlockSpec((RT_T // RT_G, 1, RT_G, d), lambda j, lo: (0, j, 0, 0)),
            scratch_shapes=[
                pltpu.VMEM((2, rows, d), jnp.bfloat16),
                pltpu.VMEM((rows, d), jnp.bfloat16),
                pltpu.VMEM((N_EXPERTS, RT_T), jnp.float32),
                pltpu.SemaphoreType.DMA((2,)),
                pltpu.SemaphoreType.DMA(()),
            ],
        ),
        out_shape=jax.ShapeDtypeStruct(x4.shape, jnp.float32),
        compiler_params=pltpu.CompilerParams(
            dimension_semantics=("arbitrary",),
            vmem_limit_bytes=_vmem_limit(40 * 1024 * 1024)),
        name="combine",
    )(lo_flat, x4, rel_t, m4, ye)


def _window_bias_t():
    j = jnp.arange(3 * WIN_SUB)[:, None]
    i = jnp.arange(WIN_SUB)[None, :]
    dist = jnp.abs(i + WIN_SUB - j).astype(jnp.float32)
    slopes = jnp.exp2(-8.0 * (jnp.arange(N_HEADS_A, dtype=jnp.float32) + 1.0) / N_HEADS_A)
    b = jnp.where(dist[None] <= WINDOW, -(slopes[:, None, None] * dist[None]), NEG)
    return jnp.transpose(b, (1, 0, 2)).reshape(3 * WIN_SUB, N_HEADS_A * WIN_SUB) * LOG2E


def _na_bias_t(rpb):
    kk = jnp.arange(NA_KEY_ROWS)[:, None]
    rho = jnp.arange(NA_GROUP_ROWS)[None, :]
    rel = kk - NA_GROUP_ROWS
    r0 = jnp.stack([
        jnp.zeros_like(rho),
        rho - NA_ROWS // 2,
        jnp.full_like(rho, NA_GROUP_ROWS - NA_ROWS),
    ])
    row_ok = (rel[None] >= r0) & (rel[None] < r0 + NA_ROWS)
    dr = jnp.clip(rel - rho + (NA_ROWS - 1), 0, 2 * NA_ROWS - 2)
    ck = jnp.arange(GRID_W)[:, None]
    cq = jnp.arange(GRID_W)[None, :]
    c0 = jnp.clip(cq - NA_COLS // 2, 0, GRID_W - NA_COLS)
    col_ok = (ck >= c0) & (ck < c0 + NA_COLS)
    dc = jnp.clip(ck - cq + (NA_COLS - 1), 0, 2 * NA_COLS - 2)
    hi = lax.Precision.HIGHEST
    oh_r = (dr[:, :, None] == jnp.arange(2 * NA_ROWS - 1)).astype(jnp.float32)
    oh_c = (dc[None] == jnp.arange(2 * NA_COLS - 1)[:, None, None]).astype(jnp.float32)
    rows = jnp.einsum('krs,hsd->hkrd', oh_r, rpb.astype(jnp.float32), precision=hi)
    vals = jnp.einsum('hkrd,dcq->hkcrq', rows, oh_c, precision=hi)
    ok = row_ok[:, :, None, :, None] & col_ok[None, None, :, None, :]
    b = jnp.where(ok[:, None], vals[None], NEG)
    h = rpb.shape[0]
    b = b.reshape(3, h // 2, 2, NA_KEY_ROWS * GRID_W, NA_TQ)
    b = jnp.transpose(b, (0, 1, 3, 2, 4)).reshape(3, h // 2, NA_KEY_ROWS * GRID_W, 2 * NA_TQ)
    return b * LOG2E


def _layer_params(p, l):
    scale = LOG2E / math.sqrt(HEAD_DIM)
    gains = jnp.concatenate([p["qnorm_a"][l] * scale, p["knorm_a"][l],
                             p["qnorm_b"][l] * scale, p["knorm_b"][l]])
    wr = p["w_router"][l].T
    wr_hi = wr.astype(jnp.bfloat16)
    return dict(
        g_mix=p["norm_mix"][l][None, :],
        w_in_t=p["w_in"][l].T.astype(jnp.bfloat16),
        head_gains=jnp.broadcast_to(gains[:, None], (4 * HEAD_DIM, PROJ_TM)),
        sink_row=jnp.repeat(p["sink_a"][l].astype(jnp.float32) * LOG2E, WIN_SUB)[None, :],
        na_bias=_na_bias_t(p["rpb_b"][l]),
        g_a=p["onorm_a"][l][None, :],
        g_b=p["onorm_b"][l][None, :],
        w_out=p["w_out"][l].astype(jnp.bfloat16),
        g_ffn=p["norm_ffn"][l][None, :],
        wr_hi=wr_hi,
        wr_lo=(wr - wr_hi.astype(jnp.float32)).astype(jnp.bfloat16),
        layer=l,
        w_gate=p["w_gate"],
        w_up=p["w_up"],
        w_down=p["w_down"],
    )


def _trunk(x, layers, win_bias):
    b, s, d = x.shape
    n = b * s
    assert s % WIN_TQ == 0 and s // NA_TQ >= 3 and n % PROJ_TM == 0 and n % RT_T == 0
    cap = EC_CAPACITY * n // N_EXPERTS
    tm = min(FFN_TM, cap)
    assert cap % tm == 0 and tm % RT_CH == 0
    cpad = cap + tm
    x2 = x.reshape(n, d)
    for q in layers:
        qa_t, ka, va_t, qb_t, kb, vb_t = _in_proj(x2, q["g_mix"], q["w_in_t"], q["head_gains"])
        out_a = _window_attention(qa_t, ka, va_t, win_bias, q["sink_row"], s)
        out_b = _na_attention(qb_t, kb, vb_t, q["na_bias"], s)
        x1, h, aff_t = _post_attn(out_a, out_b, x2, q["g_a"], q["g_b"], q["w_out"],
                                  q["g_ffn"], q["wr_hi"], q["wr_lo"])
        nt = n // RT_T
        m4 = _retile(_select(aff_t, cap)).reshape(RT_T // RT_G, nt, RT_G, LANE)
        rel_t, lo = _rank(m4)
        lo_flat = lo.reshape(-1)
        xe = _dispatch(_granule_view(h, nt), rel_t, lo_flat, cap, cpad)
        ye = _expert_ffn(xe, q["w_gate"], q["w_up"], q["w_down"], q["layer"], cap, cpad, tm)
        x2 = _combine(_granule_view(x1, nt), rel_t, m4, ye, lo_flat, cap).reshape(n, d)
    return x2.reshape(b, s, d)


def kernel(x_prompt, x_sample, norm_mix, w_in, qnorm_a, knorm_a, sink_a, qnorm_b, knorm_b,
           rpb_b, onorm_a, onorm_b, w_out, norm_ffn, w_router, w_gate, w_up, w_down):
    p = dict(norm_mix=norm_mix, w_in=w_in, qnorm_a=qnorm_a, knorm_a=knorm_a, sink_a=sink_a,
             qnorm_b=qnorm_b, knorm_b=knorm_b, rpb_b=rpb_b, onorm_a=onorm_a, onorm_b=onorm_b,
             w_out=w_out, norm_ffn=norm_ffn, w_router=w_router, w_gate=w_gate, w_up=w_up,
             w_down=w_down)
    layers = [_layer_params(p, l) for l in range(w_in.shape[0])]
    win_bias = _window_bias_t()
    return (_trunk(x_prompt, layers, win_bias), _trunk(x_sample, layers, win_bias))
```

```python
import functools
import math

import jax
import jax.numpy as jnp
from jax import lax
from jax.experimental import pallas as pl
from jax.experimental.pallas import tpu as pltpu

HEAD_DIM = 64
N_HEADS_A = 8
N_KV_HEADS_A = 2
N_HEADS_B = 8
QA_W = N_HEADS_A * HEAD_DIM
KVA_W = N_KV_HEADS_A * HEAD_DIM
QKVB_W = N_HEADS_B * HEAD_DIM
PROJ_W = QA_W + 2 * KVA_W + 3 * QKVB_W
WINDOW = 128
GRID_W = 64
NA_ROWS = 8
NA_COLS = 16
N_EXPERTS = 16
EC_CAPACITY = 2
EPS = 1e-6
NEG = -1e30
LOG2E = 1.4426950408889634

LANE = 128
V7X_VMEM_BYTES = 64 * 1024 * 1024

PROJ_TM = 512
WIN_TQ = 512
WIN_SUB = WINDOW
NA_GROUP_ROWS = 4
NA_TQ = NA_GROUP_ROWS * GRID_W
NA_KEY_ROWS = 3 * NA_GROUP_ROWS
NA_KC = 128
POST_TM = 512
FFN_TM = 512
FFN_TF = 512
RT_T = 256
RT_CH = 64
DISP_CH = 48
RT_G = 16
RT_TPS = 2
RETILE_ROWS = 2048
RANK_TILES = 8

_NT = (((1,), (1,)), ((), ()))


def _vmem_limit(nbytes):
    return int(min(nbytes, V7X_VMEM_BYTES - 4 * 1024 * 1024))


def _proj_kernel(x_ref, g_ref, w_ref, hg_ref,
                 qa_ref, ka_ref, va_ref, qb_ref, kb_ref, vb_ref):
    x = x_ref[...]
    ms = jnp.mean(x * x, axis=-1, keepdims=True)
    h = (x * lax.rsqrt(ms + EPS) * g_ref[...]).astype(jnp.bfloat16)

    def seg(lo, hi):
        return lax.dot_general(w_ref[lo:hi, :], h, _NT,
                               preferred_element_type=jnp.float32)

    def head_norm(blk, gain):
        ssq = jnp.sum(blk * blk, axis=0, keepdims=True)
        return blk * lax.rsqrt(ssq * (1.0 / HEAD_DIM) + EPS) * gain

    g_qa = hg_ref[0 * HEAD_DIM:1 * HEAD_DIM, :]
    g_ka = hg_ref[1 * HEAD_DIM:2 * HEAD_DIM, :]
    g_qb = hg_ref[2 * HEAD_DIM:3 * HEAD_DIM, :]
    g_kb = hg_ref[3 * HEAD_DIM:4 * HEAD_DIM, :]

    o = 0
    p = seg(o, o + QA_W)
    for hd in range(N_HEADS_A):
        r = slice(hd * HEAD_DIM, (hd + 1) * HEAD_DIM)
        qa_ref[r, :] = head_norm(p[r, :], g_qa).astype(qa_ref.dtype)
    o += QA_W
    p = seg(o, o + 2 * KVA_W)
    kn = jnp.concatenate(
        [head_norm(p[hd * HEAD_DIM:(hd + 1) * HEAD_DIM, :], g_ka)
         for hd in range(N_KV_HEADS_A)], axis=0)
    ka_ref[...] = kn.T.astype(ka_ref.dtype)
    va_ref[...] = p[KVA_W:2 * KVA_W, :].astype(va_ref.dtype)
    o += 2 * KVA_W
    p = seg(o, o + QKVB_W)
    for hd in range(N_HEADS_B):
        r = slice(hd * HEAD_DIM, (hd + 1) * HEAD_DIM)
        qb_ref[r, :] = head_norm(p[r, :], g_qb).astype(qb_ref.dtype)
    o += QKVB_W
    p = seg(o, o + QKVB_W)
    kn = jnp.concatenate(
        [head_norm(p[hd * HEAD_DIM:(hd + 1) * HEAD_DIM, :], g_kb)
         for hd in range(N_HEADS_B)], axis=0)
    kb_ref[...] = kn.T.astype(kb_ref.dtype)
    o += QKVB_W
    vb_ref[...] = seg(o, o + QKVB_W).astype(vb_ref.dtype)


def _in_proj(x2d, g_mix, w_in_t, head_gains):
    n, d = x2d.shape
    tm = PROJ_TM
    bf = jnp.bfloat16
    col = lambda i: (0, i)
    row = lambda i: (i, 0)
    const = lambda i: (0, 0)
    out_shape = (
        jax.ShapeDtypeStruct((QA_W, n), bf),
        jax.ShapeDtypeStruct((n, KVA_W), bf),
        jax.ShapeDtypeStruct((KVA_W, n), bf),
        jax.ShapeDtypeStruct((QKVB_W, n), bf),
        jax.ShapeDtypeStruct((n, QKVB_W), bf),
        jax.ShapeDtypeStruct((QKVB_W, n), bf),
    )
    out_specs = (
        pl.BlockSpec((QA_W, tm), col),
        pl.BlockSpec((tm, KVA_W), row),
        pl.BlockSpec((KVA_W, tm), col),
        pl.BlockSpec((QKVB_W, tm), col),
        pl.BlockSpec((tm, QKVB_W), row),
        pl.BlockSpec((QKVB_W, tm), col),
    )
    return pl.pallas_call(
        _proj_kernel,
        grid=(n // tm,),
        in_specs=[
            pl.BlockSpec((tm, d), row),
            pl.BlockSpec((1, d), const),
            pl.BlockSpec((PROJ_W, d), const),
            pl.BlockSpec((4 * HEAD_DIM, tm), const),
        ],
        out_specs=out_specs,
        out_shape=out_shape,
        compiler_params=pltpu.CompilerParams(
            dimension_semantics=("arbitrary",),
            vmem_limit_bytes=_vmem_limit(48 * 1024 * 1024)),
        name="in_proj",
    )(x2d, g_mix, w_in_t, head_gains)


def _fold(acc, v, op):
    return v if acc is None else op(acc, v)


def _window_kernel(blocks_per_seq, q_ref, kp_ref, kc_ref, kn_ref,
                   vp_ref, vc_ref, vn_ref, bias_ref, sink_ref, o_ref, s_scr, p_scr):
    i = pl.program_id(0)
    pos = i % blocks_per_seq
    pen_prev = jnp.where(pos == 0, NEG, 0.0).astype(jnp.float32)
    pen_next = jnp.where(pos == blocks_per_seq - 1, NEG, 0.0).astype(jnp.float32)

    sink = sink_ref[...]
    n_sub = WIN_TQ // WIN_SUB
    n_chunks = 3
    gq = N_HEADS_A // N_KV_HEADS_A
    zero = jnp.zeros((HEAD_DIM, WIN_SUB), jnp.bfloat16)
    krefs = (kp_ref, kc_ref, kn_ref)

    def key_block(kb):
        if kb == 0:
            return 0, 0
        if kb == n_sub + 1:
            return 2, 0
        return 1, (kb - 1) * WIN_SUB

    def qblock(j):
        cols = slice(j * WIN_SUB, (j + 1) * WIN_SUB)
        halves = []
        for kv in range(N_KV_HEADS_A):
            parts = []
            for hd in range(N_HEADS_A):
                if hd // gq == kv:
                    parts.append(q_ref[hd * HEAD_DIM:(hd + 1) * HEAD_DIM, cols])
                else:
                    parts.append(zero)
            halves.append(jnp.concatenate(parts, axis=1))
        return jnp.concatenate(halves, axis=0)

    def score_chunk(j, c, qblk):
        rows = slice(c * WIN_SUB, (c + 1) * WIN_SUB)
        r, off = key_block(j + c)
        s = jnp.dot(krefs[r][off:off + WIN_SUB, :], qblk,
                    preferred_element_type=jnp.float32)
        s = s + bias_ref[rows, :]
        if j + c == 0:
            s = s + pen_prev
        if j + c == n_sub + 1:
            s = s + pen_next
        s_scr[j % 2, rows, :] = s
        return jnp.max(s, axis=0, keepdims=True)

    def prob_chunk(j, c, m):
        rows = slice(c * WIN_SUB, (c + 1) * WIN_SUB)
        p_scr[j % 2, rows, :] = jnp.exp2(s_scr[j % 2, rows, :] - m).astype(jnp.bfloat16)

    def finish(j, m):
        cols = slice(j * WIN_SUB, (j + 1) * WIN_SUB)
        vparts = []
        for c in range(n_chunks):
            r, off = key_block(j + c)
            vparts.append((vp_ref, vc_ref, vn_ref)[r][:, off:off + WIN_SUB])
        vwin = jnp.concatenate(vparts, axis=1)
        ones = jnp.ones((16, n_chunks * WIN_SUB), jnp.bfloat16)
        sink_term = jnp.exp2(sink - m)
        outs = []
        for kv in range(N_KV_HEADS_A):
            lanes = slice(kv * gq * WIN_SUB, (kv + 1) * gq * WIN_SUB)
            vt = jnp.concatenate([vwin[kv * HEAD_DIM:(kv + 1) * HEAD_DIM, :], ones], axis=0)
            o_t = jnp.dot(vt, p_scr[j % 2, :, lanes],
                          preferred_element_type=jnp.float32)
            o_t = o_t[:HEAD_DIM] / (o_t[HEAD_DIM:HEAD_DIM + 1] + sink_term[:, lanes])
            for g in range(gq):
                outs.append(o_t[:, g * WIN_SUB:(g + 1) * WIN_SUB])
        for a in range(N_HEADS_A // 2):
            pair = jnp.concatenate([outs[2 * a], outs[2 * a + 1]], axis=0)
            o_ref[cols, a * LANE:(a + 1) * LANE] = pair.T.astype(o_ref.dtype)

    qb = qblock(0)
    m = None
    for c in range(n_chunks):
        m = _fold(m, score_chunk(0, c, qb), jnp.maximum)
    m = jnp.maximum(m, sink)
    m_done = None
    for j in range(n_sub):
        m_next = None
        if j + 1 < n_sub:
            qb = qblock(j + 1)
        for c in range(n_chunks):
            prob_chunk(j, c, m)
            if j + 1 < n_sub:
                m_next = _fold(m_next, score_chunk(j + 1, c, qb), jnp.maximum)
            if c == 0 and j >= 1:
                finish(j - 1, m_done)
        m_done = m
        if j + 1 < n_sub:
            m = jnp.maximum(m_next, sink)
    finish(n_sub - 1, m_done)


def _window_attention(qa_t, ka, va_t, bias_t, sink_row, seq_len):
    n = ka.shape[0]
    nblk = n // WIN_TQ
    bps = seq_len // WIN_TQ
    r = WIN_TQ // WIN_SUB
    nsub = n // WIN_SUB
    prev_i = lambda i: jnp.maximum(r * i - 1, 0)
    next_i = lambda i: jnp.minimum(r * i + r, nsub - 1)
    const = lambda i: (0, 0)
    return pl.pallas_call(
        functools.partial(_window_kernel, bps),
        grid=(nblk,),
        in_specs=[
            pl.BlockSpec((QA_W, WIN_TQ), lambda i: (0, i)),
            pl.BlockSpec((WIN_SUB, KVA_W), lambda i: (prev_i(i), 0)),
            pl.BlockSpec((WIN_TQ, KVA_W), lambda i: (i, 0)),
            pl.BlockSpec((WIN_SUB, KVA_W), lambda i: (next_i(i), 0)),
            pl.BlockSpec((KVA_W, WIN_SUB), lambda i: (0, prev_i(i))),
            pl.BlockSpec((KVA_W, WIN_TQ), lambda i: (0, i)),
            pl.BlockSpec((KVA_W, WIN_SUB), lambda i: (0, next_i(i))),
            pl.BlockSpec((3 * WIN_SUB, N_HEADS_A * WIN_SUB), const),
            pl.BlockSpec((1, N_HEADS_A * WIN_SUB), const),
        ],
        out_specs=pl.BlockSpec((WIN_TQ, QA_W), lambda i: (i, 0)),
        out_shape=jax.ShapeDtypeStruct((n, QA_W), jnp.bfloat16),
        scratch_shapes=[
            pltpu.VMEM((2, 3 * WIN_SUB, N_HEADS_A * WIN_SUB), jnp.float32),
            pltpu.VMEM((2, 3 * WIN_SUB, N_HEADS_A * WIN_SUB), jnp.bfloat16),
        ],
        compiler_params=pltpu.CompilerParams(
            dimension_semantics=("arbitrary",),
            vmem_limit_bytes=_vmem_limit(40 * 1024 * 1024)),
        name="window_attn",
    )(qa_t, ka, ka, ka, va_t, va_t, va_t, bias_t, sink_row)


def _na_kernel(q_ref, kp_ref, kc_ref, kn_ref, vp_ref, vc_ref, vn_ref,
               bias_ref, o_ref, s_scr, p_scr):
    zero = jnp.zeros((HEAD_DIM, NA_TQ), jnp.bfloat16)
    n_keys = NA_KEY_ROWS * GRID_W
    n_chunks = n_keys // NA_KC
    n_pairs = N_HEADS_B // 2
    krefs = (kp_ref, kc_ref, kn_ref)

    def qblock(pr):
        q0 = q_ref[(2 * pr) * HEAD_DIM:(2 * pr + 1) * HEAD_DIM, :]
        q1 = q_ref[(2 * pr + 1) * HEAD_DIM:(2 * pr + 2) * HEAD_DIM, :]
        return jnp.concatenate(
            [jnp.concatenate([q0, zero], axis=1),
             jnp.concatenate([zero, q1], axis=1)], axis=0)

    def score_chunk(pr, c, qblk):
        rows = slice(c * NA_KC, (c + 1) * NA_KC)
        blk, off = divmod(c * NA_KC, NA_TQ)
        kchunk = krefs[blk][off:off + NA_KC, pr * LANE:(pr + 1) * LANE]
        s = jnp.dot(kchunk, qblk, preferred_element_type=jnp.float32)
        s = s + bias_ref[0, pr, rows, :]
        s_scr[pr % 2, rows, :] = s
        return jnp.max(s, axis=0, keepdims=True)

    def prob_chunk(pr, c, m):
        rows = slice(c * NA_KC, (c + 1) * NA_KC)
        p_scr[pr % 2, rows, :] = jnp.exp2(s_scr[pr % 2, rows, :] - m).astype(jnp.bfloat16)

    def finish(pr):
        lanes = slice(pr * LANE, (pr + 1) * LANE)
        vwin = jnp.concatenate(
            [vp_ref[lanes, :], vc_ref[lanes, :], vn_ref[lanes, :]], axis=1)
        ones = jnp.ones((16, n_keys), jnp.bfloat16)
        outs = []
        for t in range(2):
            cols = slice(t * NA_TQ, (t + 1) * NA_TQ)
            vt = jnp.concatenate([vwin[t * HEAD_DIM:(t + 1) * HEAD_DIM, :], ones], axis=0)
            o_t = jnp.dot(vt, p_scr[pr % 2, :, cols],
                          preferred_element_type=jnp.float32)
            outs.append(o_t[:HEAD_DIM] / o_t[HEAD_DIM:HEAD_DIM + 1])
        pair = jnp.concatenate(outs, axis=0)
        o_ref[:, lanes] = pair.T.astype(o_ref.dtype)

    qb = qblock(0)
    m = None
    for c in range(n_chunks):
        m = _fold(m, score_chunk(0, c, qb), jnp.maximum)
    for pr in range(n_pairs):
        m_next = None
        if pr + 1 < n_pairs:
            qb = qblock(pr + 1)
        for c in range(n_chunks):
            prob_chunk(pr, c, m)
            if pr + 1 < n_pairs:
                m_next = _fold(m_next, score_chunk(pr + 1, c, qb), jnp.maximum)
            if c == 0 and pr >= 1:
                finish(pr - 1)
        m = m_next
    finish(n_pairs - 1)


def _na_attention(qb_t, kb, vb_t, bias, seq_len):
    n = kb.shape[0]
    ng = n // NA_TQ
    gps = seq_len // NA_TQ
    prev_i = lambda g: jnp.maximum(g - 1, 0)
    next_i = lambda g: jnp.minimum(g + 1, ng - 1)

    def variant(g):
        pos = g % gps
        return jnp.where(pos == 0, 0, jnp.where(pos == gps - 1, 2, 1))

    return pl.pallas_call(
        _na_kernel,
        grid=(ng,),
        in_specs=[
            pl.BlockSpec((QKVB_W, NA_TQ), lambda g: (0, g)),
            pl.BlockSpec((NA_TQ, QKVB_W), lambda g: (prev_i(g), 0)),
            pl.BlockSpec((NA_TQ, QKVB_W), lambda g: (g, 0)),
            pl.BlockSpec((NA_TQ, QKVB_W), lambda g: (next_i(g), 0)),
            pl.BlockSpec((QKVB_W, NA_TQ), lambda g: (0, prev_i(g))),
            pl.BlockSpec((QKVB_W, NA_TQ), lambda g: (0, g)),
            pl.BlockSpec((QKVB_W, NA_TQ), lambda g: (0, next_i(g))),
            pl.BlockSpec((1, N_HEADS_B // 2, NA_KEY_ROWS * GRID_W, 2 * NA_TQ),
                         lambda g: (variant(g), 0, 0, 0)),
        ],
        out_specs=pl.BlockSpec((NA_TQ, QKVB_W), lambda g: (g, 0)),
        out_shape=jax.ShapeDtypeStruct((n, QKVB_W), jnp.bfloat16),
        scratch_shapes=[
            pltpu.VMEM((2, NA_KEY_ROWS * GRID_W, 2 * NA_TQ), jnp.float32),
            pltpu.VMEM((2, NA_KEY_ROWS * GRID_W, 2 * NA_TQ), jnp.bfloat16),
        ],
        compiler_params=pltpu.CompilerParams(
            dimension_semantics=("arbitrary",),
            vmem_limit_bytes=_vmem_limit(48 * 1024 * 1024)),
        name="na_attn",
    )(qb_t, kb, kb, kb, vb_t, vb_t, vb_t, bias)


def _post_kernel(a_ref, b_ref, x_ref, ga_ref, gb_ref, w_ref, gf_ref,
                 wr_ref, x1_ref, h_ref, aff_ref):
    def rms(v, g):
        ms = jnp.mean(v * v, axis=-1, keepdims=True)
        return v * lax.rsqrt(ms + EPS) * g

    an = rms(a_ref[...].astype(jnp.float32), ga_ref[...]).astype(jnp.bfloat16)
    bn = rms(b_ref[...].astype(jnp.float32), gb_ref[...]).astype(jnp.bfloat16)
    y = jnp.dot(an, w_ref[:QA_W, :], preferred_element_type=jnp.float32)
    y = y + jnp.dot(bn, w_ref[QA_W:, :], preferred_element_type=jnp.float32)
    x1 = x_ref[...] + y
    x1_ref[...] = x1
    h = rms(x1, gf_ref[...])
    h_hi = h.astype(jnp.bfloat16)
    h_lo = (h - h_hi.astype(jnp.float32)).astype(jnp.bfloat16)
    h_ref[...] = h_hi
    both = lax.dot_general(wr_ref[...], h_hi, _NT, preferred_element_type=jnp.float32)
    logits = both[:N_EXPERTS] + both[N_EXPERTS:]
    logits = logits + lax.dot_general(wr_ref[:N_EXPERTS, :], h_lo, _NT,
                                      preferred_element_type=jnp.float32)
    m = jnp.max(logits, axis=0, keepdims=True)
    e = jnp.exp(logits - m)
    aff_ref[...] = e / jnp.sum(e, axis=0, keepdims=True)


def _post_attn(out_a, out_b, x2d, g_a, g_b, w_out, g_ffn, wr_hilo):
    n, d = x2d.shape
    tm = POST_TM
    row = lambda i: (i, 0)
    const = lambda i: (0, 0)
    return pl.pallas_call(
        _post_kernel,
        grid=(n // tm,),
        in_specs=[
            pl.BlockSpec((tm, QA_W), row),
            pl.BlockSpec((tm, QKVB_W), row),
            pl.BlockSpec((tm, d), row),
            pl.BlockSpec((1, QA_W), const),
            pl.BlockSpec((1, QKVB_W), const),
            pl.BlockSpec((QA_W + QKVB_W, d), const),
            pl.BlockSpec((1, d), const),
            pl.BlockSpec((2 * N_EXPERTS, d), const),
        ],
        out_specs=(
            pl.BlockSpec((tm, d), row),
            pl.BlockSpec((tm, d), row),
            pl.BlockSpec((N_EXPERTS, tm), lambda i: (0, i)),
        ),
        out_shape=(
            jax.ShapeDtypeStruct((n, d), jnp.float32),
            jax.ShapeDtypeStruct((n, d), jnp.bfloat16),
            jax.ShapeDtypeStruct((N_EXPERTS, n), jnp.float32),
        ),
        compiler_params=pltpu.CompilerParams(
            dimension_semantics=("arbitrary",),
            vmem_limit_bytes=_vmem_limit(40 * 1024 * 1024)),
        name="post_attn",
    )(out_a, out_b, x2d, g_a, g_b, w_out, g_ffn, wr_hilo)


def _strict_upper(n):
    r = lax.broadcasted_iota(jnp.int32, (n, n), 0)
    c = lax.broadcasted_iota(jnp.int32, (n, n), 1)
    return jnp.where(r < c, 1.0, 0.0).astype(jnp.bfloat16)


def _select_kernel(cap, aff_ref, sel_ref):
    n = aff_ref.shape[1]
    nt = n // RT_T
    cap_f = jnp.float32(cap)

    def count(mask):
        return jnp.sum(jnp.where(mask, 1.0, 0.0), axis=1, keepdims=True)

    def search(b, ans):
        cand = ans | jnp.left_shift(jnp.int32(1), 30 - b)
        bits = pltpu.bitcast(aff_ref[...], jnp.int32)
        return jnp.where(count(bits >= cand) >= cap_f, cand, ans)

    thr = lax.fori_loop(0, 31, search, jnp.zeros((N_EXPERTS, 1), jnp.int32))
    need = cap_f - count(pltpu.bitcast(aff_ref[...], jnp.int32) > thr)
    tri = _strict_upper(RT_T)

    def tile(c, run_eq):
        start = pl.multiple_of(c * RT_T, RT_T)
        bits = pltpu.bitcast(aff_ref[:, pl.ds(start, RT_T)], jnp.int32)
        eq = bits == thr
        eq_b = jnp.where(eq, 1.0, 0.0).astype(jnp.bfloat16)
        eq_rank = jnp.dot(eq_b, tri, preferred_element_type=jnp.float32)
        sel = (bits > thr) | (eq & (run_eq + eq_rank < need))
        sel_ref[:, pl.ds(start, RT_T)] = jnp.where(sel, aff_ref[:, pl.ds(start, RT_T)], -1.0)
        return run_eq + count(eq)

    lax.fori_loop(0, nt, tile, need * 0.0)


def _retile_kernel(m_ref, o_ref):
    pad = jnp.zeros((LANE - N_EXPERTS, RT_T), jnp.float32)
    for q in range(m_ref.shape[1] // RT_T):
        cols = slice(q * RT_T, (q + 1) * RT_T)
        o_ref[cols, :] = jnp.concatenate([m_ref[:, cols], pad], axis=0).T


def _tile_columns(m_ref, t):
    rows = m_ref[:, t].reshape(RT_T, LANE)
    return rows.T[:N_EXPERTS, :]


def _rank_kernel(nt, m_ref, rel_ref, lo_ref, run_ref):
    jb = pl.program_id(0)
    tb = m_ref.shape[1]
    w = lo_ref.shape[1]
    tri = _strict_upper(RT_T)
    lane = lax.broadcasted_iota(jnp.int32, (N_EXPERTS, w), 1)

    @pl.when(jb == 0)
    def _():
        lo_ref[...] = jnp.zeros_like(lo_ref)
        run_ref[...] = jnp.zeros_like(run_ref)

    for t in range(tb):
        sel = _tile_columns(m_ref, t) >= 0.0
        sel_f = jnp.where(sel, 1.0, 0.0)
        rank = jnp.dot(sel_f.astype(jnp.bfloat16), tri, preferred_element_type=jnp.float32)
        rel_ref[:, t * RT_T:(t + 1) * RT_T] = jnp.where(sel, rank, -1.0).astype(jnp.int32)
        run = run_ref[...]
        lo_ref[...] = jnp.where(lane == jb * tb + t, run.astype(jnp.int32), lo_ref[...])
        run_ref[...] = run + jnp.sum(sel_f, axis=1, keepdims=True)

    @pl.when(jb == pl.num_programs(0) - 1)
    def _():
        lo_ref[...] = jnp.where(lane >= nt, run_ref[...].astype(jnp.int32), lo_ref[...])


def _select(aff_t, cap):
    e, n = aff_t.shape
    full = lambda i: (0, 0)
    return pl.pallas_call(
        functools.partial(_select_kernel, cap),
        grid=(1,),
        in_specs=[pl.BlockSpec((e, n), full)],
        out_specs=pl.BlockSpec((e, n), full),
        out_shape=jax.ShapeDtypeStruct((e, n), jnp.float32),
        compiler_params=pltpu.CompilerParams(
            dimension_semantics=("arbitrary",),
            vmem_limit_bytes=_vmem_limit(40 * 1024 * 1024)),
        name="route_select",
    )(aff_t)


def _retile(m_t):
    e, n = m_t.shape
    rb = min(RETILE_ROWS, n)
    return pl.pallas_call(
        _retile_kernel,
        grid=(n // rb,),
        in_specs=[pl.BlockSpec((e, rb), lambda i: (0, i))],
        out_specs=pl.BlockSpec((rb, LANE), lambda i: (i, 0)),
        out_shape=jax.ShapeDtypeStruct((n, LANE), jnp.float32),
        compiler_params=pltpu.CompilerParams(dimension_semantics=("arbitrary",)),
        name="route_retile",
    )(m_t)


def _rank(m4):
    g, nt, rg, lanes = m4.shape
    tb = min(RANK_TILES, nt)
    w = nt + LANE
    return pl.pallas_call(
        functools.partial(_rank_kernel, nt),
        grid=(nt // tb,),
        in_specs=[pl.BlockSpec((g, tb, rg, lanes), lambda i: (0, i, 0, 0))],
        out_specs=(pl.BlockSpec((N_EXPERTS, tb * RT_T), lambda i: (0, i)),
                   pl.BlockSpec((N_EXPERTS, w), lambda i: (0, 0))),
        out_shape=(jax.ShapeDtypeStruct((N_EXPERTS, nt * RT_T), jnp.int32),
                   jax.ShapeDtypeStruct((N_EXPERTS, w), jnp.int32)),
        scratch_shapes=[pltpu.VMEM((N_EXPERTS, 1), jnp.float32)],
        compiler_params=pltpu.CompilerParams(dimension_semantics=("arbitrary",)),
        name="route_rank",
    )(m4)


def _granule_view(x, nt):
    n, d = x.shape
    return x.reshape(RT_T // RT_G, nt, RT_G, d)


def _pack_pairs(x):
    w = x.shape[1] // 2
    lo = pltpu.bitcast(x[:, :w], jnp.uint32)
    hi = pltpu.bitcast(x[:, w:], jnp.uint32)
    return lo | (hi >> 16)


def _unpack_pairs(p):
    lo = pltpu.bitcast(p & jnp.uint32(0xFFFF0000), jnp.float32).astype(jnp.bfloat16)
    hi = pltpu.bitcast(p << 16, jnp.float32).astype(jnp.bfloat16)
    return lo, hi


def _one_hot_rows(rel_ref, cols, shift):
    kio = lax.broadcasted_iota(jnp.int32, (DISP_CH, RT_T), 0)
    blocks = []
    for e in range(N_EXPERTS):
        hit = (rel_ref[e:e + 1, cols] - shift) == kio
        blocks.append(jnp.where(hit, 1.0, 0.0).astype(jnp.bfloat16))
    return jnp.concatenate(blocks, axis=0)


def _dispatch_kernel(cap, cpad, nt, w, lo_ref, h_ref, rel_ref, xe_hbm,
                     stage, stage_x, sem, sem_x):
    for t in range(RT_TPS):
        _dispatch_tile(cap, cpad, nt, w, pl.program_id(0) * RT_TPS + t, t,
                       lo_ref, h_ref, rel_ref, xe_hbm, stage, stage_x, sem, sem_x)


def _dispatch_tile(cap, cpad, nt, w, j, t, lo_ref, h_ref, rel_ref, xe_hbm,
                   stage, stage_x, sem, sem_x):
    slot = j % 2
    cols = slice(t * RT_T, (t + 1) * RT_T)

    def dst(e, jj, c):
        row = e * cpad + lo_ref[e * w + jj] + c * DISP_CH
        return xe_hbm.at[pl.ds(row, DISP_CH), 0]

    def chunk_copy(e, jj, sl):
        return pltpu.make_async_copy(
            stage.at[sl, pl.ds(e * DISP_CH, DISP_CH)], dst(e, jj, 0), sem.at[sl])

    @pl.when(j == 0)
    def _():
        pad = cpad - cap
        stage_x[...] = jnp.zeros_like(stage_x)
        fills = [pltpu.make_async_copy(
            stage_x.at[pl.ds(0, pad)], xe_hbm.at[pl.ds(e * cpad + cap, pad), 0], sem_x)
            for e in range(N_EXPERTS)]
        for f in fills:
            f.start()
        for f in fills:
            f.wait()

    h_tile = h_ref[:, t].reshape(RT_T, h_ref.shape[-1])
    x = jnp.dot(_one_hot_rows(rel_ref, cols, 0), h_tile, preferred_element_type=jnp.float32)
    stage[slot] = _pack_pairs(x)

    @pl.when(j > 0)
    def _():
        for e in range(N_EXPERTS):
            chunk_copy(e, j - 1, 1 - slot).wait()

    for e in range(N_EXPERTS):
        chunk_copy(e, j, slot).start()

    cnts = [lo_ref[e * w + j + 1] - lo_ref[e * w + j] for e in range(N_EXPERTS)]
    most = functools.reduce(jnp.maximum, cnts)
    n_pass = lax.div(most + (DISP_CH - 1), jnp.int32(DISP_CH))

    def extra(c, carry):
        kio = lax.broadcasted_iota(jnp.int32, (DISP_CH, RT_T), 0)

        def extra_copy(e):
            return pltpu.make_async_copy(
                stage_x.at[pl.ds(e * DISP_CH, DISP_CH)], dst(e, j, c), sem_x)

        for e in range(N_EXPERTS):
            @pl.when(cnts[e] > c * DISP_CH)
            def _():
                hit = (rel_ref[e:e + 1, cols] - c * DISP_CH) == kio
                xx = jnp.dot(jnp.where(hit, 1.0, 0.0).astype(jnp.bfloat16),
                             h_ref[:, t].reshape(RT_T, h_ref.shape[-1]),
                             preferred_element_type=jnp.float32)
                stage_x[e * DISP_CH:(e + 1) * DISP_CH, :] = _pack_pairs(xx)
                extra_copy(e).start()
        for e in range(N_EXPERTS):
            @pl.when(cnts[e] > c * DISP_CH)
            def _():
                extra_copy(e).wait()
        return carry

    lax.fori_loop(1, n_pass, extra, 0)

    @pl.when(j == nt - 1)
    def _():
        for e in range(N_EXPERTS):
            chunk_copy(e, j, slot).wait()


def _dispatch(h4, rel_t, lo_flat, cap, cpad):
    _, nt, _, d = h4.shape
    w = lo_flat.shape[0] // N_EXPERTS
    rows = N_EXPERTS * DISP_CH
    assert cpad - cap <= rows
    return pl.pallas_call(
        functools.partial(_dispatch_kernel, cap, cpad, nt, w),
        grid_spec=pltpu.PrefetchScalarGridSpec(
            num_scalar_prefetch=1,
            grid=(nt // RT_TPS,),
            in_specs=[
                pl.BlockSpec((RT_T // RT_G, RT_TPS, RT_G, d), lambda j, lo: (0, j, 0, 0)),
                pl.BlockSpec((N_EXPERTS, RT_TPS * RT_T), lambda j, lo: (0, j)),
            ],
            out_specs=pl.BlockSpec(memory_space=pl.ANY),
            scratch_shapes=[
                pltpu.VMEM((2, rows, d // 2), jnp.uint32),
                pltpu.VMEM((rows, d // 2), jnp.uint32),
                pltpu.SemaphoreType.DMA((2,)),
                pltpu.SemaphoreType.DMA(()),
            ],
        ),
        out_shape=jax.ShapeDtypeStruct((N_EXPERTS * cpad, 1, d // 2), jnp.uint32),
        compiler_params=pltpu.CompilerParams(
            dimension_semantics=("arbitrary",),
            vmem_limit_bytes=_vmem_limit(40 * 1024 * 1024)),
        name="dispatch",
    )(lo_flat, h4, rel_t)


def _ffn_kernel(layer, cpad, tiles, tm, x_hbm, wg_hbm, wu_hbm, wd_hbm, o_ref,
                xbuf, wg_b, wu_b, wd_b, stg_g, stg_u, stg_d, xsem, wsem):
    i = pl.program_id(0)
    j = pl.program_id(1)
    n_exp = pl.num_programs(0)
    step = i * tiles + j
    slot = step % 2
    prev_slot = (step + 1) % 2
    rg = wg_b.shape[1] // tiles
    rd = wd_b.shape[1] // tiles

    def x_copy(ii, jj, sl):
        return pltpu.make_async_copy(
            x_hbm.at[pl.ds(ii * cpad + jj * tm, tm), 0], xbuf.at[sl], xsem.at[sl])

    def slab_copies(e, k, sl):
        r_g = pl.multiple_of(k * rg, rg)
        r_d = pl.multiple_of(k * rd, rd)
        return (
            pltpu.make_async_copy(wg_hbm.at[layer, e, pl.ds(r_g, rg)], stg_g.at[sl], wsem.at[sl]),
            pltpu.make_async_copy(wu_hbm.at[layer, e, pl.ds(r_g, rg)], stg_u.at[sl], wsem.at[sl]),
            pltpu.make_async_copy(wd_hbm.at[layer, e, pl.ds(r_d, rd)], stg_d.at[sl], wsem.at[sl]),
        )

    def cast_slab(wslot, k, sl):
        r_g = pl.multiple_of(k * rg, rg)
        r_d = pl.multiple_of(k * rd, rd)
        wg_b[wslot, pl.ds(r_g, rg), :] = stg_g[sl].astype(jnp.bfloat16)
        wu_b[wslot, pl.ds(r_g, rg), :] = stg_u[sl].astype(jnp.bfloat16)
        wd_b[wslot, pl.ds(r_d, rd), :] = stg_d[sl].astype(jnp.bfloat16)

    @pl.when(step == 0)
    def _():
        x_copy(0, 0, 0).start()

        def load(k, carry):
            sl = (k + tiles) % 2
            for cp in slab_copies(0, k, sl):
                cp.start()
            for cp in slab_copies(0, k, sl):
                cp.wait()
            cast_slab(0, k, sl)
            return carry

        lax.fori_loop(0, tiles, load, 0)

    j_prev = jnp.where(j > 0, j - 1, tiles - 1)
    e_prev = jnp.where(j > 0, i + 1, i)
    pending = jnp.logical_and(step > 0, e_prev < n_exp)

    @pl.when(pending)
    def _():
        for cp in slab_copies(e_prev, j_prev, prev_slot):
            cp.wait()

    @pl.when(i + 1 < n_exp)
    def _():
        for cp in slab_copies(i + 1, j, slot):
            cp.start()

    cast_slab(e_prev % 2, j_prev, prev_slot)

    @pl.when(step + 1 < n_exp * tiles)
    def _():
        wrap = j + 1 == tiles
        x_copy(jnp.where(wrap, i + 1, i), jnp.where(wrap, 0, j + 1), 1 - slot).start()

    x_copy(i, j, slot).wait()
    x_lo, x_hi = _unpack_pairs(xbuf[slot])
    half = x_lo.shape[1]
    d_ff = wg_b.shape[2]
    wg, wu, wd = wg_b.at[i % 2], wu_b.at[i % 2], wd_b.at[i % 2]
    acc = None
    for c in range(d_ff // FFN_TF):
        f = slice(c * FFN_TF, (c + 1) * FFN_TF)
        g = jnp.dot(x_lo, wg[:half, f], preferred_element_type=jnp.float32)
        g = g + jnp.dot(x_hi, wg[half:, f], preferred_element_type=jnp.float32)
        u = jnp.dot(x_lo, wu[:half, f], preferred_element_type=jnp.float32)
        u = u + jnp.dot(x_hi, wu[half:, f], preferred_element_type=jnp.float32)
        act = (g * jax.nn.sigmoid(g) * u).astype(jnp.bfloat16)
        part = jnp.dot(act, wd[f, :], preferred_element_type=jnp.float32)
        acc = part if acc is None else acc + part
    o_ref[...] = acc.astype(o_ref.dtype)


def _expert_ffn(xe, w_gate, w_up, w_down, layer, cap, cpad, tm):
    _, e, d, d_ff = w_gate.shape
    tiles = cap // tm
    assert d % tiles == 0 and (d // tiles) % 16 == 0
    any_spec = pl.BlockSpec(memory_space=pl.ANY)
    return pl.pallas_call(
        functools.partial(_ffn_kernel, layer, cpad, tiles, tm),
        grid=(e, tiles),
        in_specs=[any_spec, any_spec, any_spec, any_spec],
        out_specs=pl.BlockSpec((tm, d), lambda i, j: (i * tiles + j, 0)),
        out_shape=jax.ShapeDtypeStruct((e * cap, d), jnp.bfloat16),
        scratch_shapes=[
            pltpu.VMEM((2, tm, d // 2), jnp.uint32),
            pltpu.VMEM((2, d, d_ff), jnp.bfloat16),
            pltpu.VMEM((2, d, d_ff), jnp.bfloat16),
            pltpu.VMEM((2, d_ff, d), jnp.bfloat16),
            pltpu.VMEM((2, d // tiles, d_ff), jnp.float32),
            pltpu.VMEM((2, d // tiles, d_ff), jnp.float32),
            pltpu.VMEM((2, d_ff // tiles, d), jnp.float32),
            pltpu.SemaphoreType.DMA((2,)),
            pltpu.SemaphoreType.DMA((2,)),
        ],
        compiler_params=pltpu.CompilerParams(
            dimension_semantics=("arbitrary", "arbitrary"),
            vmem_limit_bytes=_vmem_limit(58 * 1024 * 1024)),
        name="expert_ffn",
    )(xe, w_gate, w_up, w_down)


def _combine_kernel(cap, nt, w, lo_ref, x_ref, rel_ref, m_ref, ye_hbm, o_ref,
                    ybuf, ybuf_x, gate_ref, sem, sem_x):
    for t in range(RT_TPS):
        _combine_tile(cap, nt, w, pl.program_id(0) * RT_TPS + t, t, lo_ref, x_ref, rel_ref,
                      m_ref, ye_hbm, o_ref, ybuf, ybuf_x, gate_ref, sem, sem_x)


def _combine_tile(cap, nt, w, j, t, lo_ref, x_ref, rel_ref, m_ref, ye_hbm, o_ref,
                  ybuf, ybuf_x, gate_ref, sem, sem_x):
    slot = j % 2
    cols = slice(t * RT_T, (t + 1) * RT_T)
    tile_shape = (o_ref.shape[0],) + o_ref.shape[2:]
    last_start = N_EXPERTS * cap - RT_CH
    align = 16

    def start_row(e, jj, c):
        lo = lo_ref[e * w + jj]
        a = e * cap + lo - (lo & (align - 1)) + c * RT_CH
        return pl.multiple_of(jnp.minimum(a, last_start), align)

    def fetch(e, jj, sl):
        return pltpu.make_async_copy(
            ye_hbm.at[pl.ds(start_row(e, jj, 0), RT_CH)],
            ybuf.at[sl, pl.ds(e * RT_CH, RT_CH)], sem.at[sl])

    @pl.when(j == 0)
    def _():
        for e in range(N_EXPERTS):
            fetch(e, 0, 0).start()

    @pl.when(j + 1 < nt)
    def _():
        for e in range(N_EXPERTS):
            fetch(e, j + 1, 1 - slot).start()

    gate_ref[...] = _tile_columns(m_ref, t)
    los = [lo_ref[e * w + j] for e in range(N_EXPERTS)]
    cnts = [lo_ref[e * w + j + 1] - los[e] for e in range(N_EXPERTS)]
    lead = [los[e] & (align - 1) for e in range(N_EXPERTS)]

    def weight_block(e, c):
        kio = lax.broadcasted_iota(jnp.int32, (RT_CH, RT_T), 0)
        r = rel_ref[e:e + 1, cols]
        p = r + lead[e]
        member = (r >= 0) & (p >= c * RT_CH) & (p < (c + 1) * RT_CH)
        off = e * cap + los[e] - start_row(e, j, c)
        hit = member & ((r + off) == kio)
        return jnp.where(hit, gate_ref[e:e + 1, :], 0.0).astype(jnp.bfloat16)

    tn = (((0,), (0,)), ((), ()))
    wt0 = jnp.concatenate([weight_block(e, 0) for e in range(N_EXPERTS)], axis=0)
    for e in range(N_EXPERTS):
        fetch(e, j, slot).wait()
    o_ref[:, t] = x_ref[:, t] + lax.dot_general(
        wt0, ybuf[slot], tn, preferred_element_type=jnp.float32).reshape(tile_shape)

    spans = [lead[e] + cnts[e] for e in range(N_EXPERTS)]
    most = functools.reduce(jnp.maximum, spans)
    n_pass = jnp.right_shift(most + (RT_CH - 1), RT_CH.bit_length() - 1)

    def extra(c, carry):
        def extra_fetch(e):
            return pltpu.make_async_copy(
                ye_hbm.at[pl.ds(start_row(e, j, c), RT_CH)],
                ybuf_x.at[pl.ds(e * RT_CH, RT_CH)], sem_x)

        for e in range(N_EXPERTS):
            @pl.when(spans[e] > c * RT_CH)
            def _():
                extra_fetch(e).start()
        for e in range(N_EXPERTS):
            @pl.when(spans[e] > c * RT_CH)
            def _():
                extra_fetch(e).wait()
        for e in range(N_EXPERTS):
            @pl.when(spans[e] > c * RT_CH)
            def _():
                wte = weight_block(e, c)
                o_ref[:, t] += lax.dot_general(
                    wte, ybuf_x[e * RT_CH:(e + 1) * RT_CH, :], tn,
                    preferred_element_type=jnp.float32).reshape(tile_shape)
        return carry

    lax.fori_loop(1, n_pass, extra, 0)


def _combine(x4, rel_t, m4, ye, lo_flat, cap):
    _, nt, _, d = x4.shape
    w = lo_flat.shape[0] // N_EXPERTS
    rows = N_EXPERTS * RT_CH
    return pl.pallas_call(
        functools.partial(_combine_kernel, cap, nt, w),
        grid_spec=pltpu.PrefetchScalarGridSpec(
            num_scalar_prefetch=1,
            grid=(nt // RT_TPS,),
            in_specs=[
                pl.BlockSpec((RT_T // RT_G, RT_TPS, RT_G, d), lambda j, lo: (0, j, 0, 0)),
                pl.BlockSpec((N_EXPERTS, RT_TPS * RT_T), lambda j, lo: (0, j)),
                pl.BlockSpec((RT_T // RT_G, RT_TPS, RT_G, LANE), lambda j, lo: (0, j, 0, 0)),
                pl.BlockSpec(memory_space=pl.ANY),
            ],
            out_specs=pl.BlockSpec((RT_T // RT_G, RT_TPS, RT_G, d), lambda j, lo: (0, j, 0, 0)),
            scratch_shapes=[
                pltpu.VMEM((2, rows, d), jnp.bfloat16),
                pltpu.VMEM((rows, d), jnp.bfloat16),
                pltpu.VMEM((N_EXPERTS, RT_T), jnp.float32),
                pltpu.SemaphoreType.DMA((2,)),
                pltpu.SemaphoreType.DMA(()),
            ],
        ),
        out_shape=jax.ShapeDtypeStruct(x4.shape, jnp.float32),
        compiler_params=pltpu.CompilerParams(
            dimension_semantics=("arbitrary",),
            vmem_limit_bytes=_vmem_limit(40 * 1024 * 1024)),
        name="combine",
    )(lo_flat, x4, rel_t, m4, ye)


def _window_bias_t():
    j = jnp.arange(3 * WIN_SUB)[:, None]
    i = jnp.arange(WIN_SUB)[None, :]
    dist = jnp.abs(i + WIN_SUB - j).astype(jnp.float32)
    slopes = jnp.exp2(-8.0 * (jnp.arange(N_HEADS_A, dtype=jnp.float32) + 1.0) / N_HEADS_A)
    b = jnp.where(dist[None] <= WINDOW, -(slopes[:, None, None] * dist[None]), NEG)
    return jnp.transpose(b, (1, 0, 2)).reshape(3 * WIN_SUB, N_HEADS_A * WIN_SUB) * LOG2E


def _na_bias_t(rpb):
    kk = jnp.arange(NA_KEY_ROWS)[:, None]
    rho = jnp.arange(NA_GROUP_ROWS)[None, :]
    rel = kk - NA_GROUP_ROWS
    r0 = jnp.stack([
        jnp.zeros_like(rho),
        rho - NA_ROWS // 2,
        jnp.full_like(rho, NA_GROUP_ROWS - NA_ROWS),
    ])
    row_ok = (rel[None] >= r0) & (rel[None] < r0 + NA_ROWS)
    dr = jnp.clip(rel - rho + (NA_ROWS - 1), 0, 2 * NA_ROWS - 2)
    ck = jnp.arange(GRID_W)[:, None]
    cq = jnp.arange(GRID_W)[None, :]
    c0 = jnp.clip(cq - NA_COLS // 2, 0, GRID_W - NA_COLS)
    col_ok = (ck >= c0) & (ck < c0 + NA_COLS)
    dc = jnp.clip(ck - cq + (NA_COLS - 1), 0, 2 * NA_COLS - 2)
    hi = lax.Precision.HIGHEST
    oh_r = (dr[:, :, None] == jnp.arange(2 * NA_ROWS - 1)).astype(jnp.float32)
    oh_c = (dc[None] == jnp.arange(2 * NA_COLS - 1)[:, None, None]).astype(jnp.float32)
    rows = jnp.einsum('krs,hsd->hkrd', oh_r, rpb.astype(jnp.float32), precision=hi)
    vals = jnp.einsum('hkrd,dcq->hkcrq', rows, oh_c, precision=hi)
    ok = row_ok[:, :, None, :, None] & col_ok[None, None, :, None, :]
    b = jnp.where(ok[:, None], vals[None], NEG)
    h = rpb.shape[0]
    b = b.reshape(3, h // 2, 2, NA_KEY_ROWS * GRID_W, NA_TQ)
    b = jnp.transpose(b, (0, 1, 3, 2, 4)).reshape(3, h // 2, NA_KEY_ROWS * GRID_W, 2 * NA_TQ)
    return b * LOG2E


def _layer_params(p, l):
    scale = LOG2E / math.sqrt(HEAD_DIM)
    gains = jnp.concatenate([p["qnorm_a"][l] * scale, p["knorm_a"][l],
                             p["qnorm_b"][l] * scale, p["knorm_b"][l]])
    wr = p["w_router"][l].T
    wr_hi = wr.astype(jnp.bfloat16)
    return dict(
        g_mix=p["norm_mix"][l][None, :],
        w_in_t=p["w_in"][l].T.astype(jnp.bfloat16),
        head_gains=jnp.broadcast_to(gains[:, None], (4 * HEAD_DIM, PROJ_TM)),
        sink_row=jnp.repeat(p["sink_a"][l].astype(jnp.float32) * LOG2E, WIN_SUB)[None, :],
        na_bias=_na_bias_t(p["rpb_b"][l]),
        g_a=p["onorm_a"][l][None, :],
        g_b=p["onorm_b"][l][None, :],
        w_out=p["w_out"][l].astype(jnp.bfloat16),
        g_ffn=p["norm_ffn"][l][None, :],
        wr_hilo=jnp.concatenate(
            [wr_hi, (wr - wr_hi.astype(jnp.float32)).astype(jnp.bfloat16)], axis=0),
        layer=l,
        w_gate=p["w_gate"],
        w_up=p["w_up"],
        w_down=p["w_down"],
    )


def _trunk(x, layers, win_bias):
    b, s, d = x.shape
    n = b * s
    assert s % WIN_TQ == 0 and s // NA_TQ >= 3 and n % PROJ_TM == 0 and n % RT_T == 0
    assert (n // RT_T) % RT_TPS == 0 and (n // RT_T) % min(RANK_TILES, n // RT_T) == 0
    cap = EC_CAPACITY * n // N_EXPERTS
    tm = min(FFN_TM, cap)
    assert cap % tm == 0 and tm >= max(RT_CH, DISP_CH)
    cpad = cap + tm
    x2 = x.reshape(n, d)
    for q in layers:
        qa_t, ka, va_t, qb_t, kb, vb_t = _in_proj(x2, q["g_mix"], q["w_in_t"], q["head_gains"])
        out_a = _window_attention(qa_t, ka, va_t, win_bias, q["sink_row"], s)
        out_b = _na_attention(qb_t, kb, vb_t, q["na_bias"], s)
        x1, h, aff_t = _post_attn(out_a, out_b, x2, q["g_a"], q["g_b"], q["w_out"],
                                  q["g_ffn"], q["wr_hilo"])
        nt = n // RT_T
        m4 = _retile(_select(aff_t, cap)).reshape(RT_T // RT_G, nt, RT_G, LANE)
        rel_t, lo = _rank(m4)
        lo_flat = lo.reshape(-1)
        xe = _dispatch(_granule_view(h, nt), rel_t, lo_flat, cap, cpad)
        ye = _expert_ffn(xe, q["w_gate"], q["w_up"], q["w_down"], q["layer"], cap, cpad, tm)
        x2 = _combine(_granule_view(x1, nt), rel_t, m4, ye, lo_flat, cap).reshape(n, d)
    return x2.reshape(b, s, d)


def kernel(x_prompt, x_sample, norm_mix, w_in, qnorm_a, knorm_a, sink_a, qnorm_b, knorm_b,
           rpb_b, onorm_a, onorm_b, w_out, norm_ffn, w_router, w_gate, w_up, w_down):
    p = dict(norm_mix=norm_mix, w_in=w_in, qnorm_a=qnorm_a, knorm_a=knorm_a, sink_a=sink_a,
             qnorm_b=qnorm_b, knorm_b=knorm_b, rpb_b=rpb_b, onorm_a=onorm_a, onorm_b=onorm_b,
             w_out=w_out, norm_ffn=norm_ffn, w_router=w_router, w_gate=w_gate, w_up=w_up,
             w_down=w_down)
    layers = [_layer_params(p, l) for l in range(w_in.shape[0])]
    win_bias = _window_bias_t()
    return (_trunk(x_prompt, layers, win_bias), _trunk(x_sample, layers, win_bias))
```

```python
import functools
import math

import jax
import jax.numpy as jnp
from jax import lax
from jax.experimental import pallas as pl
from jax.experimental.pallas import tpu as pltpu

HEAD_DIM = 64
N_HEADS_A = 8
N_KV_HEADS_A = 2
N_HEADS_B = 8
QA_W = N_HEADS_A * HEAD_DIM
KVA_W = N_KV_HEADS_A * HEAD_DIM
QKVB_W = N_HEADS_B * HEAD_DIM
PROJ_W = QA_W + 2 * KVA_W + 3 * QKVB_W
WINDOW = 128
GRID_W = 64
NA_ROWS = 8
NA_COLS = 16
N_EXPERTS = 16
EC_CAPACITY = 2
EPS = 1e-6
NEG = -1e30
LOG2E = 1.4426950408889634

LANE = 128
V7X_VMEM_BYTES = 64 * 1024 * 1024

PROJ_TM = 512
WIN_TQ = 512
WIN_SUB = WINDOW
NA_GROUP_ROWS = 4
NA_TQ = NA_GROUP_ROWS * GRID_W
NA_KEY_ROWS = 3 * NA_GROUP_ROWS
NA_KC = 128
POST_TM = 512
FFN_TM = 512
FFN_TF = 512
RT_T = 256
RT_CH = 64
DISP_CH = 64
RT_G = 16
RT_TPS = 1
RETILE_ROWS = 2048
RANK_TILES = 8

_NT = (((1,), (1,)), ((), ()))


def _vmem_limit(nbytes):
    return int(min(nbytes, V7X_VMEM_BYTES - 4 * 1024 * 1024))


def _proj_kernel(x_ref, g_ref, w_ref, hg_ref,
                 qa_ref, ka_ref, va_ref, qb_ref, kb_ref, vb_ref):
    x = x_ref[...]
    ms = jnp.mean(x * x, axis=-1, keepdims=True)
    h = (x * lax.rsqrt(ms + EPS) * g_ref[...]).astype(jnp.bfloat16)

    def seg(lo, hi):
        return lax.dot_general(w_ref[lo:hi, :], h, _NT,
                               preferred_element_type=jnp.float32)

    def head_norm(blk, gain):
        ssq = jnp.sum(blk * blk, axis=0, keepdims=True)
        return blk * lax.rsqrt(ssq * (1.0 / HEAD_DIM) + EPS) * gain

    g_qa = hg_ref[0 * HEAD_DIM:1 * HEAD_DIM, :]
    g_ka = hg_ref[1 * HEAD_DIM:2 * HEAD_DIM, :]
    g_qb = hg_ref[2 * HEAD_DIM:3 * HEAD_DIM, :]
    g_kb = hg_ref[3 * HEAD_DIM:4 * HEAD_DIM, :]

    o = 0
    p = seg(o, o + QA_W)
    for hd in range(N_HEADS_A):
        r = slice(hd * HEAD_DIM, (hd + 1) * HEAD_DIM)
        qa_ref[r, :] = head_norm(p[r, :], g_qa).astype(qa_ref.dtype)
    o += QA_W
    p = seg(o, o + 2 * KVA_W)
    kn = jnp.concatenate(
        [head_norm(p[hd * HEAD_DIM:(hd + 1) * HEAD_DIM, :], g_ka)
         for hd in range(N_KV_HEADS_A)], axis=0)
    ka_ref[...] = kn.T.astype(ka_ref.dtype)
    va_ref[...] = p[KVA_W:2 * KVA_W, :].astype(va_ref.dtype)
    o += 2 * KVA_W
    p = seg(o, o + QKVB_W)
    for hd in range(N_HEADS_B):
        r = slice(hd * HEAD_DIM, (hd + 1) * HEAD_DIM)
        qb_ref[r, :] = head_norm(p[r, :], g_qb).astype(qb_ref.dtype)
    o += QKVB_W
    p = seg(o, o + QKVB_W)
    kn = jnp.concatenate(
        [head_norm(p[hd * HEAD_DIM:(hd + 1) * HEAD_DIM, :], g_kb)
         for hd in range(N_HEADS_B)], axis=0)
    kb_ref[...] = kn.T.astype(kb_ref.dtype)
    o += QKVB_W
    vb_ref[...] = seg(o, o + QKVB_W).astype(vb_ref.dtype)


def _in_proj(x2d, g_mix, w_in_t, head_gains):
    n, d = x2d.shape
    tm = PROJ_TM
    bf = jnp.bfloat16
    col = lambda i: (0, i)
    row = lambda i: (i, 0)
    const = lambda i: (0, 0)
    out_shape = (
        jax.ShapeDtypeStruct((QA_W, n), bf),
        jax.ShapeDtypeStruct((n, KVA_W), bf),
        jax.ShapeDtypeStruct((KVA_W, n), bf),
        jax.ShapeDtypeStruct((QKVB_W, n), bf),
        jax.ShapeDtypeStruct((n, QKVB_W), bf),
        jax.ShapeDtypeStruct((QKVB_W, n), bf),
    )
    out_specs = (
        pl.BlockSpec((QA_W, tm), col),
        pl.BlockSpec((tm, KVA_W), row),
        pl.BlockSpec((KVA_W, tm), col),
        pl.BlockSpec((QKVB_W, tm), col),
        pl.BlockSpec((tm, QKVB_W), row),
        pl.BlockSpec((QKVB_W, tm), col),
    )
    return pl.pallas_call(
        _proj_kernel,
        grid=(n // tm,),
        in_specs=[
            pl.BlockSpec((tm, d), row),
            pl.BlockSpec((1, d), const),
            pl.BlockSpec((PROJ_W, d), const),
            pl.BlockSpec((4 * HEAD_DIM, tm), const),
        ],
        out_specs=out_specs,
        out_shape=out_shape,
        compiler_params=pltpu.CompilerParams(
            dimension_semantics=("arbitrary",),
            vmem_limit_bytes=_vmem_limit(48 * 1024 * 1024)),
        name="in_proj",
    )(x2d, g_mix, w_in_t, head_gains)


def _fold(acc, v, op):
    return v if acc is None else op(acc, v)


def _window_kernel(blocks_per_seq, q_ref, kp_ref, kc_ref, kn_ref,
                   vp_ref, vc_ref, vn_ref, bias_ref, sink_ref, o_ref, s_scr, p_scr):
    i = pl.program_id(0)
    pos = i % blocks_per_seq
    pen_prev = jnp.where(pos == 0, NEG, 0.0).astype(jnp.float32)
    pen_next = jnp.where(pos == blocks_per_seq - 1, NEG, 0.0).astype(jnp.float32)

    sink = sink_ref[...]
    n_sub = WIN_TQ // WIN_SUB
    n_chunks = 3
    gq = N_HEADS_A // N_KV_HEADS_A
    zero = jnp.zeros((HEAD_DIM, WIN_SUB), jnp.bfloat16)
    krefs = (kp_ref, kc_ref, kn_ref)

    def key_block(kb):
        if kb == 0:
            return 0, 0
        if kb == n_sub + 1:
            return 2, 0
        return 1, (kb - 1) * WIN_SUB

    def qblock(j):
        cols = slice(j * WIN_SUB, (j + 1) * WIN_SUB)
        halves = []
        for kv in range(N_KV_HEADS_A):
            parts = []
            for hd in range(N_HEADS_A):
                if hd // gq == kv:
                    parts.append(q_ref[hd * HEAD_DIM:(hd + 1) * HEAD_DIM, cols])
                else:
                    parts.append(zero)
            halves.append(jnp.concatenate(parts, axis=1))
        return jnp.concatenate(halves, axis=0)

    def score_chunk(j, c, qblk):
        rows = slice(c * WIN_SUB, (c + 1) * WIN_SUB)
        r, off = key_block(j + c)
        s = jnp.dot(krefs[r][off:off + WIN_SUB, :], qblk,
                    preferred_element_type=jnp.float32)
        s = s + bias_ref[rows, :]
        if j + c == 0:
            s = s + pen_prev
        if j + c == n_sub + 1:
            s = s + pen_next
        s_scr[j % 2, rows, :] = s
        return jnp.max(s, axis=0, keepdims=True)

    def prob_chunk(j, c, m):
        rows = slice(c * WIN_SUB, (c + 1) * WIN_SUB)
        p_scr[j % 2, rows, :] = jnp.exp2(s_scr[j % 2, rows, :] - m).astype(jnp.bfloat16)

    def finish(j, m):
        cols = slice(j * WIN_SUB, (j + 1) * WIN_SUB)
        vparts = []
        for c in range(n_chunks):
            r, off = key_block(j + c)
            vparts.append((vp_ref, vc_ref, vn_ref)[r][:, off:off + WIN_SUB])
        vwin = jnp.concatenate(vparts, axis=1)
        ones = jnp.ones((16, n_chunks * WIN_SUB), jnp.bfloat16)
        sink_term = jnp.exp2(sink - m)
        outs = []
        for kv in range(N_KV_HEADS_A):
            lanes = slice(kv * gq * WIN_SUB, (kv + 1) * gq * WIN_SUB)
            vt = jnp.concatenate([vwin[kv * HEAD_DIM:(kv + 1) * HEAD_DIM, :], ones], axis=0)
            o_t = jnp.dot(vt, p_scr[j % 2, :, lanes],
                          preferred_element_type=jnp.float32)
            o_t = o_t[:HEAD_DIM] / (o_t[HEAD_DIM:HEAD_DIM + 1] + sink_term[:, lanes])
            for g in range(gq):
                outs.append(o_t[:, g * WIN_SUB:(g + 1) * WIN_SUB])
        for a in range(N_HEADS_A // 2):
            pair = jnp.concatenate([outs[2 * a], outs[2 * a + 1]], axis=0)
            o_ref[cols, a * LANE:(a + 1) * LANE] = pair.T.astype(o_ref.dtype)

    qb = qblock(0)
    m = None
    for c in range(n_chunks):
        m = _fold(m, score_chunk(0, c, qb), jnp.maximum)
    m = jnp.maximum(m, sink)
    m_done = None
    for j in range(n_sub):
        m_next = None
        if j + 1 < n_sub:
            qb = qblock(j + 1)
        for c in range(n_chunks):
            prob_chunk(j, c, m)
            if j + 1 < n_sub:
                m_next = _fold(m_next, score_chunk(j + 1, c, qb), jnp.maximum)
            if c == 0 and j >= 1:
                finish(j - 1, m_done)
        m_done = m
        if j + 1 < n_sub:
            m = jnp.maximum(m_next, sink)
    finish(n_sub - 1, m_done)


def _window_attention(qa_t, ka, va_t, bias_t, sink_row, seq_len):
    n = ka.shape[0]
    nblk = n // WIN_TQ
    bps = seq_len // WIN_TQ
    r = WIN_TQ // WIN_SUB
    nsub = n // WIN_SUB
    prev_i = lambda i: jnp.maximum(r * i - 1, 0)
    next_i = lambda i: jnp.minimum(r * i + r, nsub - 1)
    const = lambda i: (0, 0)
    return pl.pallas_call(
        functools.partial(_window_kernel, bps),
        grid=(nblk,),
        in_specs=[
            pl.BlockSpec((QA_W, WIN_TQ), lambda i: (0, i)),
            pl.BlockSpec((WIN_SUB, KVA_W), lambda i: (prev_i(i), 0)),
            pl.BlockSpec((WIN_TQ, KVA_W), lambda i: (i, 0)),
            pl.BlockSpec((WIN_SUB, KVA_W), lambda i: (next_i(i), 0)),
            pl.BlockSpec((KVA_W, WIN_SUB), lambda i: (0, prev_i(i))),
            pl.BlockSpec((KVA_W, WIN_TQ), lambda i: (0, i)),
            pl.BlockSpec((KVA_W, WIN_SUB), lambda i: (0, next_i(i))),
            pl.BlockSpec((3 * WIN_SUB, N_HEADS_A * WIN_SUB), const),
            pl.BlockSpec((1, N_HEADS_A * WIN_SUB), const),
        ],
        out_specs=pl.BlockSpec((WIN_TQ, QA_W), lambda i: (i, 0)),
        out_shape=jax.ShapeDtypeStruct((n, QA_W), jnp.bfloat16),
        scratch_shapes=[
            pltpu.VMEM((2, 3 * WIN_SUB, N_HEADS_A * WIN_SUB), jnp.float32),
            pltpu.VMEM((2, 3 * WIN_SUB, N_HEADS_A * WIN_SUB), jnp.bfloat16),
        ],
        compiler_params=pltpu.CompilerParams(
            dimension_semantics=("arbitrary",),
            vmem_limit_bytes=_vmem_limit(40 * 1024 * 1024)),
        name="window_attn",
    )(qa_t, ka, ka, ka, va_t, va_t, va_t, bias_t, sink_row)


def _na_kernel(q_ref, kp_ref, kc_ref, kn_ref, vp_ref, vc_ref, vn_ref,
               bias_ref, o_ref, s_scr, p_scr):
    zero = jnp.zeros((HEAD_DIM, NA_TQ), jnp.bfloat16)
    n_keys = NA_KEY_ROWS * GRID_W
    n_chunks = n_keys // NA_KC
    n_pairs = N_HEADS_B // 2
    krefs = (kp_ref, kc_ref, kn_ref)

    def qblock(pr):
        q0 = q_ref[(2 * pr) * HEAD_DIM:(2 * pr + 1) * HEAD_DIM, :]
        q1 = q_ref[(2 * pr + 1) * HEAD_DIM:(2 * pr + 2) * HEAD_DIM, :]
        return jnp.concatenate(
            [jnp.concatenate([q0, zero], axis=1),
             jnp.concatenate([zero, q1], axis=1)], axis=0)

    def score_chunk(pr, c, qblk):
        rows = slice(c * NA_KC, (c + 1) * NA_KC)
        blk, off = divmod(c * NA_KC, NA_TQ)
        kchunk = krefs[blk][off:off + NA_KC, pr * LANE:(pr + 1) * LANE]
        s = jnp.dot(kchunk, qblk, preferred_element_type=jnp.float32)
        s = s + bias_ref[0, pr, rows, :]
        s_scr[pr % 2, rows, :] = s
        return jnp.max(s, axis=0, keepdims=True)

    def prob_chunk(pr, c, m):
        rows = slice(c * NA_KC, (c + 1) * NA_KC)
        p_scr[pr % 2, rows, :] = jnp.exp2(s_scr[pr % 2, rows, :] - m).astype(jnp.bfloat16)

    def finish(pr):
        lanes = slice(pr * LANE, (pr + 1) * LANE)
        vwin = jnp.concatenate(
            [vp_ref[lanes, :], vc_ref[lanes, :], vn_ref[lanes, :]], axis=1)
        ones = jnp.ones((16, n_keys), jnp.bfloat16)
        outs = []
        for t in range(2):
            cols = slice(t * NA_TQ, (t + 1) * NA_TQ)
            vt = jnp.concatenate([vwin[t * HEAD_DIM:(t + 1) * HEAD_DIM, :], ones], axis=0)
            o_t = jnp.dot(vt, p_scr[pr % 2, :, cols],
                          preferred_element_type=jnp.float32)
            outs.append(o_t[:HEAD_DIM] / o_t[HEAD_DIM:HEAD_DIM + 1])
        pair = jnp.concatenate(outs, axis=0)
        o_ref[:, lanes] = pair.T.astype(o_ref.dtype)

    qb = qblock(0)
    m = None
    for c in range(n_chunks):
        m = _fold(m, score_chunk(0, c, qb), jnp.maximum)
    for pr in range(n_pairs):
        m_next = None
        if pr + 1 < n_pairs:
            qb = qblock(pr + 1)
        for c in range(n_chunks):
            prob_chunk(pr, c, m)
            if pr + 1 < n_pairs:
                m_next = _fold(m_next, score_chunk(pr + 1, c, qb), jnp.maximum)
            if c == 0 and pr >= 1:
                finish(pr - 1)
        m = m_next
    finish(n_pairs - 1)


def _na_attention(qb_t, kb, vb_t, bias, seq_len):
    n = kb.shape[0]
    ng = n // NA_TQ
    gps = seq_len // NA_TQ
    prev_i = lambda g: jnp.maximum(g - 1, 0)
    next_i = lambda g: jnp.minimum(g + 1, ng - 1)

    def variant(g):
        pos = g % gps
        return jnp.where(pos == 0, 0, jnp.where(pos == gps - 1, 2, 1))

    return pl.pallas_call(
        _na_kernel,
        grid=(ng,),
        in_specs=[
            pl.BlockSpec((QKVB_W, NA_TQ), lambda g: (0, g)),
            pl.BlockSpec((NA_TQ, QKVB_W), lambda g: (prev_i(g), 0)),
            pl.BlockSpec((NA_TQ, QKVB_W), lambda g: (g, 0)),
            pl.BlockSpec((NA_TQ, QKVB_W), lambda g: (next_i(g), 0)),
            pl.BlockSpec((QKVB_W, NA_TQ), lambda g: (0, prev_i(g))),
            pl.BlockSpec((QKVB_W, NA_TQ), lambda g: (0, g)),
            pl.BlockSpec((QKVB_W, NA_TQ), lambda g: (0, next_i(g))),
            pl.BlockSpec((1, N_HEADS_B // 2, NA_KEY_ROWS * GRID_W, 2 * NA_TQ),
                         lambda g: (variant(g), 0, 0, 0)),
        ],
        out_specs=pl.BlockSpec((NA_TQ, QKVB_W), lambda g: (g, 0)),
        out_shape=jax.ShapeDtypeStruct((n, QKVB_W), jnp.bfloat16),
        scratch_shapes=[
            pltpu.VMEM((2, NA_KEY_ROWS * GRID_W, 2 * NA_TQ), jnp.float32),
            pltpu.VMEM((2, NA_KEY_ROWS * GRID_W, 2 * NA_TQ), jnp.bfloat16),
        ],
        compiler_params=pltpu.CompilerParams(
            dimension_semantics=("arbitrary",),
            vmem_limit_bytes=_vmem_limit(48 * 1024 * 1024)),
        name="na_attn",
    )(qb_t, kb, kb, kb, vb_t, vb_t, vb_t, bias)


def _post_kernel(a_ref, b_ref, x_ref, ga_ref, gb_ref, w_ref, gf_ref,
                 wr_ref, x1_ref, h_ref, aff_ref):
    def rms(v, g):
        ms = jnp.mean(v * v, axis=-1, keepdims=True)
        return v * lax.rsqrt(ms + EPS) * g

    an = rms(a_ref[...].astype(jnp.float32), ga_ref[...]).astype(jnp.bfloat16)
    bn = rms(b_ref[...].astype(jnp.float32), gb_ref[...]).astype(jnp.bfloat16)
    y = jnp.dot(an, w_ref[:QA_W, :], preferred_element_type=jnp.float32)
    y = y + jnp.dot(bn, w_ref[QA_W:, :], preferred_element_type=jnp.float32)
    x1 = x_ref[...] + y
    x1_ref[...] = x1
    h = rms(x1, gf_ref[...])
    h_hi = h.astype(jnp.bfloat16)
    h_lo = (h - h_hi.astype(jnp.float32)).astype(jnp.bfloat16)
    h_ref[...] = h_hi
    both = lax.dot_general(wr_ref[...], h_hi, _NT, preferred_element_type=jnp.float32)
    logits = both[:N_EXPERTS] + both[N_EXPERTS:]
    logits = logits + lax.dot_general(wr_ref[:N_EXPERTS, :], h_lo, _NT,
                                      preferred_element_type=jnp.float32)
    m = jnp.max(logits, axis=0, keepdims=True)
    e = jnp.exp(logits - m)
    aff_ref[...] = e / jnp.sum(e, axis=0, keepdims=True)


def _post_attn(out_a, out_b, x2d, g_a, g_b, w_out, g_ffn, wr_hilo):
    n, d = x2d.shape
    tm = POST_TM
    row = lambda i: (i, 0)
    const = lambda i: (0, 0)
    return pl.pallas_call(
        _post_kernel,
        grid=(n // tm,),
        in_specs=[
            pl.BlockSpec((tm, QA_W), row),
            pl.BlockSpec((tm, QKVB_W), row),
            pl.BlockSpec((tm, d), row),
            pl.BlockSpec((1, QA_W), const),
            pl.BlockSpec((1, QKVB_W), const),
            pl.BlockSpec((QA_W + QKVB_W, d), const),
            pl.BlockSpec((1, d), const),
            pl.BlockSpec((2 * N_EXPERTS, d), const),
        ],
        out_specs=(
            pl.BlockSpec((tm, d), row),
            pl.BlockSpec((tm, d), row),
            pl.BlockSpec((N_EXPERTS, tm), lambda i: (0, i)),
        ),
        out_shape=(
            jax.ShapeDtypeStruct((n, d), jnp.float32),
            jax.ShapeDtypeStruct((n, d), jnp.bfloat16),
            jax.ShapeDtypeStruct((N_EXPERTS, n), jnp.float32),
        ),
        compiler_params=pltpu.CompilerParams(
            dimension_semantics=("arbitrary",),
            vmem_limit_bytes=_vmem_limit(40 * 1024 * 1024)),
        name="post_attn",
    )(out_a, out_b, x2d, g_a, g_b, w_out, g_ffn, wr_hilo)


def _strict_upper(n):
    r = lax.broadcasted_iota(jnp.int32, (n, n), 0)
    c = lax.broadcasted_iota(jnp.int32, (n, n), 1)
    return jnp.where(r < c, 1.0, 0.0).astype(jnp.bfloat16)


def _select_kernel(cap, aff_ref, sel_ref):
    n = aff_ref.shape[1]
    nt = n // RT_T
    cap_f = jnp.float32(cap)

    def count(mask):
        return jnp.sum(jnp.where(mask, 1.0, 0.0), axis=1, keepdims=True)

    def search(b, ans):
        cand = ans | jnp.left_shift(jnp.int32(1), 30 - b)
        bits = pltpu.bitcast(aff_ref[...], jnp.int32)
        return jnp.where(count(bits >= cand) >= cap_f, cand, ans)

    thr = lax.fori_loop(0, 31, search, jnp.zeros((N_EXPERTS, 1), jnp.int32))
    need = cap_f - count(pltpu.bitcast(aff_ref[...], jnp.int32) > thr)
    tri = _strict_upper(RT_T)

    def tile(c, run_eq):
        start = pl.multiple_of(c * RT_T, RT_T)
        bits = pltpu.bitcast(aff_ref[:, pl.ds(start, RT_T)], jnp.int32)
        eq = bits == thr
        eq_b = jnp.where(eq, 1.0, 0.0).astype(jnp.bfloat16)
        eq_rank = jnp.dot(eq_b, tri, preferred_element_type=jnp.float32)
        sel = (bits > thr) | (eq & (run_eq + eq_rank < need))
        sel_ref[:, pl.ds(start, RT_T)] = jnp.where(sel, aff_ref[:, pl.ds(start, RT_T)], -1.0)
        return run_eq + count(eq)

    lax.fori_loop(0, nt, tile, need * 0.0)


def _retile_kernel(m_ref, o_ref):
    pad = jnp.zeros((LANE - N_EXPERTS, RT_T), jnp.float32)
    for q in range(m_ref.shape[1] // RT_T):
        cols = slice(q * RT_T, (q + 1) * RT_T)
        o_ref[cols, :] = jnp.concatenate([m_ref[:, cols], pad], axis=0).T


def _tile_columns(m_ref, t):
    rows = m_ref[:, t].reshape(RT_T, LANE)
    return rows.T[:N_EXPERTS, :]


def _rank_kernel(nt, m_ref, rel_ref, lo_ref, run_ref):
    jb = pl.program_id(0)
    tb = m_ref.shape[1]
    w = lo_ref.shape[1]
    tri = _strict_upper(RT_T)
    lane = lax.broadcasted_iota(jnp.int32, (N_EXPERTS, w), 1)

    @pl.when(jb == 0)
    def _():
        lo_ref[...] = jnp.zeros_like(lo_ref)
        run_ref[...] = jnp.zeros_like(run_ref)

    for t in range(tb):
        sel = _tile_columns(m_ref, t) >= 0.0
        sel_f = jnp.where(sel, 1.0, 0.0)
        rank = jnp.dot(sel_f.astype(jnp.bfloat16), tri, preferred_element_type=jnp.float32)
        rel_ref[:, t * RT_T:(t + 1) * RT_T] = jnp.where(sel, rank, -1.0).astype(jnp.int32)
        run = run_ref[...]
        lo_ref[...] = jnp.where(lane == jb * tb + t, run.astype(jnp.int32), lo_ref[...])
        run_ref[...] = run + jnp.sum(sel_f, axis=1, keepdims=True)

    @pl.when(jb == pl.num_programs(0) - 1)
    def _():
        lo_ref[...] = jnp.where(lane >= nt, run_ref[...].astype(jnp.int32), lo_ref[...])


def _select(aff_t, cap):
    e, n = aff_t.shape
    full = lambda i: (0, 0)
    return pl.pallas_call(
        functools.partial(_select_kernel, cap),
        grid=(1,),
        in_specs=[pl.BlockSpec((e, n), full)],
        out_specs=pl.BlockSpec((e, n), full),
        out_shape=jax.ShapeDtypeStruct((e, n), jnp.float32),
        compiler_params=pltpu.CompilerParams(
            dimension_semantics=("arbitrary",),
            vmem_limit_bytes=_vmem_limit(40 * 1024 * 1024)),
        name="route_select",
    )(aff_t)


def _retile(m_t):
    e, n = m_t.shape
    rb = min(RETILE_ROWS, n)
    return pl.pallas_call(
        _retile_kernel,
        grid=(n // rb,),
        in_specs=[pl.BlockSpec((e, rb), lambda i: (0, i))],
        out_specs=pl.BlockSpec((rb, LANE), lambda i: (i, 0)),
        out_shape=jax.ShapeDtypeStruct((n, LANE), jnp.float32),
        compiler_params=pltpu.CompilerParams(dimension_semantics=("arbitrary",)),
        name="route_retile",
    )(m_t)


def _rank(m4):
    g, nt, rg, lanes = m4.shape
    tb = min(RANK_TILES, nt)
    w = nt + LANE
    return pl.pallas_call(
        functools.partial(_rank_kernel, nt),
        grid=(nt // tb,),
        in_specs=[pl.BlockSpec((g, tb, rg, lanes), lambda i: (0, i, 0, 0))],
        out_specs=(pl.BlockSpec((N_EXPERTS, tb * RT_T), lambda i: (0, i)),
                   pl.BlockSpec((N_EXPERTS, w), lambda i: (0, 0))),
        out_shape=(jax.ShapeDtypeStruct((N_EXPERTS, nt * RT_T), jnp.int32),
                   jax.ShapeDtypeStruct((N_EXPERTS, w), jnp.int32)),
        scratch_shapes=[pltpu.VMEM((N_EXPERTS, 1), jnp.float32)],
        compiler_params=pltpu.CompilerParams(dimension_semantics=("arbitrary",)),
        name="route_rank",
    )(m4)


def _granule_view(x, nt):
    n, d = x.shape
    return x.reshape(RT_T // RT_G, nt, RT_G, d)


def _pack_pairs(x):
    w = x.shape[1] // 2
    lo = pltpu.bitcast(x[:, :w], jnp.uint32)
    hi = pltpu.bitcast(x[:, w:], jnp.uint32)
    return lo | (hi >> 16)


def _unpack_pairs(p):
    lo = pltpu.bitcast(p & jnp.uint32(0xFFFF0000), jnp.float32).astype(jnp.bfloat16)
    hi = pltpu.bitcast(p << 16, jnp.float32).astype(jnp.bfloat16)
    return lo, hi


def _one_hot_rows(rel_ref, cols, shift):
    kio = lax.broadcasted_iota(jnp.int32, (DISP_CH, RT_T), 0)
    blocks = []
    for e in range(N_EXPERTS):
        hit = (rel_ref[e:e + 1, cols] - shift) == kio
        blocks.append(jnp.where(hit, 1.0, 0.0).astype(jnp.bfloat16))
    return jnp.concatenate(blocks, axis=0)


def _dispatch_kernel(cap, cpad, nt, w, lo_ref, h_ref, rel_ref, xe_hbm,
                     stage, stage_x, sem, sem_x):
    for t in range(RT_TPS):
        _dispatch_tile(cap, cpad, nt, w, pl.program_id(0) * RT_TPS + t, t,
                       lo_ref, h_ref, rel_ref, xe_hbm, stage, stage_x, sem, sem_x)


def _dispatch_tile(cap, cpad, nt, w, j, t, lo_ref, h_ref, rel_ref, xe_hbm,
                   stage, stage_x, sem, sem_x):
    slot = j % 2
    cols = slice(t * RT_T, (t + 1) * RT_T)

    def dst(e, jj, c):
        row = e * cpad + lo_ref[e * w + jj] + c * DISP_CH
        return xe_hbm.at[pl.ds(row, DISP_CH), 0]

    def chunk_copy(e, jj, sl):
        return pltpu.make_async_copy(
            stage.at[sl, pl.ds(e * DISP_CH, DISP_CH)], dst(e, jj, 0), sem.at[sl])

    @pl.when(j == 0)
    def _():
        pad = cpad - cap
        stage_x[...] = jnp.zeros_like(stage_x)
        fills = [pltpu.make_async_copy(
            stage_x.at[pl.ds(0, pad)], xe_hbm.at[pl.ds(e * cpad + cap, pad), 0], sem_x)
            for e in range(N_EXPERTS)]
        for f in fills:
            f.start()
        for f in fills:
            f.wait()

    h_tile = h_ref[:, t].reshape(RT_T, h_ref.shape[-1])
    x = jnp.dot(_one_hot_rows(rel_ref, cols, 0), h_tile, preferred_element_type=jnp.float32)
    stage[slot] = _pack_pairs(x)

    @pl.when(j > 0)
    def _():
        for e in range(N_EXPERTS):
            chunk_copy(e, j - 1, 1 - slot).wait()

    for e in range(N_EXPERTS):
        chunk_copy(e, j, slot).start()

    cnts = [lo_ref[e * w + j + 1] - lo_ref[e * w + j] for e in range(N_EXPERTS)]
    most = functools.reduce(jnp.maximum, cnts)
    n_pass = lax.div(most + (DISP_CH - 1), jnp.int32(DISP_CH))

    def extra(c, carry):
        kio = lax.broadcasted_iota(jnp.int32, (DISP_CH, RT_T), 0)

        def extra_copy(e):
            return pltpu.make_async_copy(
                stage_x.at[pl.ds(e * DISP_CH, DISP_CH)], dst(e, j, c), sem_x)

        for e in range(N_EXPERTS):
            @pl.when(cnts[e] > c * DISP_CH)
            def _():
                hit = (rel_ref[e:e + 1, cols] - c * DISP_CH) == kio
                xx = jnp.dot(jnp.where(hit, 1.0, 0.0).astype(jnp.bfloat16),
                             h_ref[:, t].reshape(RT_T, h_ref.shape[-1]),
                             preferred_element_type=jnp.float32)
                stage_x[e * DISP_CH:(e + 1) * DISP_CH, :] = _pack_pairs(xx)
                extra_copy(e).start()
        for e in range(N_EXPERTS):
            @pl.when(cnts[e] > c * DISP_CH)
            def _():
                extra_copy(e).wait()
        return carry

    lax.fori_loop(1, n_pass, extra, 0)

    @pl.when(j == nt - 1)
    def _():
        for e in range(N_EXPERTS):
            chunk_copy(e, j, slot).wait()


def _dispatch(h4, rel_t, lo_flat, cap, cpad):
    _, nt, _, d = h4.shape
    w = lo_flat.shape[0] // N_EXPERTS
    rows = N_EXPERTS * DISP_CH
    assert cpad - cap <= rows
    return pl.pallas_call(
        functools.partial(_dispatch_kernel, cap, cpad, nt, w),
        grid_spec=pltpu.PrefetchScalarGridSpec(
            num_scalar_prefetch=1,
            grid=(nt // RT_TPS,),
            in_specs=[
                pl.BlockSpec((RT_T // RT_G, RT_TPS, RT_G, d), lambda j, lo: (0, j, 0, 0)),
                pl.BlockSpec((N_EXPERTS, RT_TPS * RT_T), lambda j, lo: (0, j)),
            ],
            out_specs=pl.BlockSpec(memory_space=pl.ANY),
            scratch_shapes=[
                pltpu.VMEM((2, rows, d // 2), jnp.uint32),
                pltpu.VMEM((rows, d // 2), jnp.uint32),
                pltpu.SemaphoreType.DMA((2,)),
                pltpu.SemaphoreType.DMA(()),
            ],
        ),
        out_shape=jax.ShapeDtypeStruct((N_EXPERTS * cpad, 1, d // 2), jnp.uint32),
        compiler_params=pltpu.CompilerParams(
            dimension_semantics=("arbitrary",),
            vmem_limit_bytes=_vmem_limit(40 * 1024 * 1024)),
        name="dispatch",
    )(lo_flat, h4, rel_t)


def _ffn_kernel(layer, cpad, tiles, tm, x_hbm, wg_hbm, wu_hbm, wd_hbm, o_ref,
                xbuf, wg_b, wu_b, wd_b, stg_g, stg_u, stg_d, xsem, wsem):
    i = pl.program_id(0)
    j = pl.program_id(1)
    n_exp = pl.num_programs(0)
    step = i * tiles + j
    slot = step % 2
    prev_slot = (step + 1) % 2
    rg = wg_b.shape[1] // tiles
    rd = wd_b.shape[1] // tiles

    def x_copy(ii, jj, sl):
        return pltpu.make_async_copy(
            x_hbm.at[pl.ds(ii * cpad + jj * tm, tm), 0], xbuf.at[sl], xsem.at[sl])

    def slab_copies(e, k, sl):
        r_g = pl.multiple_of(k * rg, rg)
        r_d = pl.multiple_of(k * rd, rd)
        return (
            pltpu.make_async_copy(wg_hbm.at[layer, e, pl.ds(r_g, rg)], stg_g.at[sl], wsem.at[sl]),
            pltpu.make_async_copy(wu_hbm.at[layer, e, pl.ds(r_g, rg)], stg_u.at[sl], wsem.at[sl]),
            pltpu.make_async_copy(wd_hbm.at[layer, e, pl.ds(r_d, rd)], stg_d.at[sl], wsem.at[sl]),
        )

    def cast_slab(wslot, k, sl):
        r_g = pl.multiple_of(k * rg, rg)
        r_d = pl.multiple_of(k * rd, rd)
        wg_b[wslot, pl.ds(r_g, rg), :] = stg_g[sl].astype(jnp.bfloat16)
        wu_b[wslot, pl.ds(r_g, rg), :] = stg_u[sl].astype(jnp.bfloat16)
        wd_b[wslot, pl.ds(r_d, rd), :] = stg_d[sl].astype(jnp.bfloat16)

    @pl.when(step == 0)
    def _():
        x_copy(0, 0, 0).start()

        def load(k, carry):
            sl = (k + tiles) % 2
            for cp in slab_copies(0, k, sl):
                cp.start()
            for cp in slab_copies(0, k, sl):
                cp.wait()
            cast_slab(0, k, sl)
            return carry

        lax.fori_loop(0, tiles, load, 0)

    j_prev = jnp.where(j > 0, j - 1, tiles - 1)
    e_prev = jnp.where(j > 0, i + 1, i)
    pending = jnp.logical_and(step > 0, e_prev < n_exp)

    @pl.when(pending)
    def _():
        for cp in slab_copies(e_prev, j_prev, prev_slot):
            cp.wait()

    @pl.when(i + 1 < n_exp)
    def _():
        for cp in slab_copies(i + 1, j, slot):
            cp.start()

    cast_slab(e_prev % 2, j_prev, prev_slot)

    @pl.when(step + 1 < n_exp * tiles)
    def _():
        wrap = j + 1 == tiles
        x_copy(jnp.where(wrap, i + 1, i), jnp.where(wrap, 0, j + 1), 1 - slot).start()

    x_copy(i, j, slot).wait()
    x_lo, x_hi = _unpack_pairs(xbuf[slot])
    half = x_lo.shape[1]
    d_ff = wg_b.shape[2]
    wg, wu, wd = wg_b.at[i % 2], wu_b.at[i % 2], wd_b.at[i % 2]
    acc = None
    for c in range(d_ff // FFN_TF):
        f = slice(c * FFN_TF, (c + 1) * FFN_TF)
        g = jnp.dot(x_lo, wg[:half, f], preferred_element_type=jnp.float32)
        g = g + jnp.dot(x_hi, wg[half:, f], preferred_element_type=jnp.float32)
        u = jnp.dot(x_lo, wu[:half, f], preferred_element_type=jnp.float32)
        u = u + jnp.dot(x_hi, wu[half:, f], preferred_element_type=jnp.float32)
        act = (g * jax.nn.sigmoid(g) * u).astype(jnp.bfloat16)
        part = jnp.dot(act, wd[f, :], preferred_element_type=jnp.float32)
        acc = part if acc is None else acc + part
    o_ref[...] = acc.astype(o_ref.dtype)


def _expert_ffn(xe, w_gate, w_up, w_down, layer, cap, cpad, tm):
    _, e, d, d_ff = w_gate.shape
    tiles = cap // tm
    assert d % tiles == 0 and (d // tiles) % 16 == 0
    any_spec = pl.BlockSpec(memory_space=pl.ANY)
    return pl.pallas_call(
        functools.partial(_ffn_kernel, layer, cpad, tiles, tm),
        grid=(e, tiles),
        in_specs=[any_spec, any_spec, any_spec, any_spec],
        out_specs=pl.BlockSpec((tm, d), lambda i, j: (i * tiles + j, 0)),
        out_shape=jax.ShapeDtypeStruct((e * cap, d), jnp.bfloat16),
        scratch_shapes=[
            pltpu.VMEM((2, tm, d // 2), jnp.uint32),
            pltpu.VMEM((2, d, d_ff), jnp.bfloat16),
            pltpu.VMEM((2, d, d_ff), jnp.bfloat16),
            pltpu.VMEM((2, d_ff, d), jnp.bfloat16),
            pltpu.VMEM((2, d // tiles, d_ff), jnp.float32),
            pltpu.VMEM((2, d // tiles, d_ff), jnp.float32),
            pltpu.VMEM((2, d_ff // tiles, d), jnp.float32),
            pltpu.SemaphoreType.DMA((2,)),
            pltpu.SemaphoreType.DMA((2,)),
        ],
        compiler_params=pltpu.CompilerParams(
            dimension_semantics=("arbitrary", "arbitrary"),
            vmem_limit_bytes=_vmem_limit(58 * 1024 * 1024)),
        name="expert_ffn",
    )(xe, w_gate, w_up, w_down)


def _combine_kernel(cap, nt, w, lo_ref, x_ref, rel_ref, m_ref, ye_hbm, o_ref,
                    ybuf, ybuf_x, gate_ref, sem, sem_x):
    for t in range(RT_TPS):
        _combine_tile(cap, nt, w, pl.program_id(0) * RT_TPS + t, t, lo_ref, x_ref, rel_ref,
                      m_ref, ye_hbm, o_ref, ybuf, ybuf_x, gate_ref, sem, sem_x)


def _combine_tile(cap, nt, w, j, t, lo_ref, x_ref, rel_ref, m_ref, ye_hbm, o_ref,
                  ybuf, ybuf_x, gate_ref, sem, sem_x):
    slot = j % 2
    cols = slice(t * RT_T, (t + 1) * RT_T)
    tile_shape = (o_ref.shape[0],) + o_ref.shape[2:]
    last_start = N_EXPERTS * cap - RT_CH
    align = 16

    def start_row(e, jj, c):
        lo = lo_ref[e * w + jj]
        a = e * cap + lo - (lo & (align - 1)) + c * RT_CH
        return pl.multiple_of(jnp.minimum(a, last_start), align)

    def fetch(e, jj, sl):
        return pltpu.make_async_copy(
            ye_hbm.at[pl.ds(start_row(e, jj, 0), RT_CH)],
            ybuf.at[sl, pl.ds(e * RT_CH, RT_CH)], sem.at[sl])

    @pl.when(j == 0)
    def _():
        for e in range(N_EXPERTS):
            fetch(e, 0, 0).start()

    @pl.when(j + 1 < nt)
    def _():
        for e in range(N_EXPERTS):
            fetch(e, j + 1, 1 - slot).start()

    gate_ref[...] = _tile_columns(m_ref, t)
    los = [lo_ref[e * w + j] for e in range(N_EXPERTS)]
    cnts = [lo_ref[e * w + j + 1] - los[e] for e in range(N_EXPERTS)]
    lead = [los[e] & (align - 1) for e in range(N_EXPERTS)]

    def weight_block(e, c):
        kio = lax.broadcasted_iota(jnp.int32, (RT_CH, RT_T), 0)
        r = rel_ref[e:e + 1, cols]
        p = r + lead[e]
        member = (r >= 0) & (p >= c * RT_CH) & (p < (c + 1) * RT_CH)
        off = e * cap + los[e] - start_row(e, j, c)
        hit = member & ((r + off) == kio)
        return jnp.where(hit, gate_ref[e:e + 1, :], 0.0).astype(jnp.bfloat16)

    tn = (((0,), (0,)), ((), ()))
    wt0 = jnp.concatenate([weight_block(e, 0) for e in range(N_EXPERTS)], axis=0)
    for e in range(N_EXPERTS):
        fetch(e, j, slot).wait()
    o_ref[:, t] = x_ref[:, t] + lax.dot_general(
        wt0, ybuf[slot], tn, preferred_element_type=jnp.float32).reshape(tile_shape)

    spans = [lead[e] + cnts[e] for e in range(N_EXPERTS)]
    most = functools.reduce(jnp.maximum, spans)
    n_pass = jnp.right_shift(most + (RT_CH - 1), RT_CH.bit_length() - 1)

    def extra(c, carry):
        def extra_fetch(e):
            return pltpu.make_async_copy(
                ye_hbm.at[pl.ds(start_row(e, j, c), RT_CH)],
                ybuf_x.at[pl.ds(e * RT_CH, RT_CH)], sem_x)

        for e in range(N_EXPERTS):
            @pl.when(spans[e] > c * RT_CH)
            def _():
                extra_fetch(e).start()
        for e in range(N_EXPERTS):
            @pl.when(spans[e] > c * RT_CH)
            def _():
                extra_fetch(e).wait()
        for e in range(N_EXPERTS):
            @pl.when(spans[e] > c * RT_CH)
            def _():
                wte = weight_block(e, c)
                o_ref[:, t] += lax.dot_general(
                    wte, ybuf_x[e * RT_CH:(e + 1) * RT_CH, :], tn,
                    preferred_element_type=jnp.float32).reshape(tile_shape)
        return carry

    lax.fori_loop(1, n_pass, extra, 0)


def _combine(x4, rel_t, m4, ye, lo_flat, cap):
    _, nt, _, d = x4.shape
    w = lo_flat.shape[0] // N_EXPERTS
    rows = N_EXPERTS * RT_CH
    return pl.pallas_call(
        functools.partial(_combine_kernel, cap, nt, w),
        grid_spec=pltpu.PrefetchScalarGridSpec(
            num_scalar_prefetch=1,
            grid=(nt // RT_TPS,),
            in_specs=[
                pl.BlockSpec((RT_T // RT_G, RT_TPS, RT_G, d), lambda j, lo: (0, j, 0, 0)),
                pl.BlockSpec((N_EXPERTS, RT_TPS * RT_T), lambda j, lo: (0, j)),
                pl.BlockSpec((RT_T // RT_G, RT_TPS, RT_G, LANE), lambda j, lo: (0, j, 0, 0)),
                pl.BlockSpec(memory_space=pl.ANY),
            ],
            out_specs=pl.BlockSpec((RT_T // RT_G, RT_TPS, RT_G, d), lambda j, lo: (0, j, 0, 0)),
            scratch_shapes=[
                pltpu.VMEM((2, rows, d), jnp.bfloat16),
                pltpu.VMEM((rows, d), jnp.bfloat16),
                pltpu.VMEM((N_EXPERTS, RT_T), jnp.float32),
                pltpu.SemaphoreType.DMA((2,)),
                pltpu.SemaphoreType.DMA(()),
            ],
        ),
        out_shape=jax.ShapeDtypeStruct(x4.shape, jnp.float32),
        compiler_params=pltpu.CompilerParams(
            dimension_semantics=("arbitrary",),
            vmem_limit_bytes=_vmem_limit(40 * 1024 * 1024)),
        name="combine",
    )(lo_flat, x4, rel_t, m4, ye)


def _window_bias_t():
    j = jnp.arange(3 * WIN_SUB)[:, None]
    i = jnp.arange(WIN_SUB)[None, :]
    dist = jnp.abs(i + WIN_SUB - j).astype(jnp.float32)
    slopes = jnp.exp2(-8.0 * (jnp.arange(N_HEADS_A, dtype=jnp.float32) + 1.0) / N_HEADS_A)
    b = jnp.where(dist[None] <= WINDOW, -(slopes[:, None, None] * dist[None]), NEG)
    return jnp.transpose(b, (1, 0, 2)).reshape(3 * WIN_SUB, N_HEADS_A * WIN_SUB) * LOG2E


def _na_bias_t(rpb):
    kk = jnp.arange(NA_KEY_ROWS)[:, None]
    rho = jnp.arange(NA_GROUP_ROWS)[None, :]
    rel = kk - NA_GROUP_ROWS
    r0 = jnp.stack([
        jnp.zeros_like(rho),
        rho - NA_ROWS // 2,
        jnp.full_like(rho, NA_GROUP_ROWS - NA_ROWS),
    ])
    row_ok = (rel[None] >= r0) & (rel[None] < r0 + NA_ROWS)
    dr = jnp.clip(rel - rho + (NA_ROWS - 1), 0, 2 * NA_ROWS - 2)
    ck = jnp.arange(GRID_W)[:, None]
    cq = jnp.arange(GRID_W)[None, :]
    c0 = jnp.clip(cq - NA_COLS // 2, 0, GRID_W - NA_COLS)
    col_ok = (ck >= c0) & (ck < c0 + NA_COLS)
    dc = jnp.clip(ck - cq + (NA_COLS - 1), 0, 2 * NA_COLS - 2)
    hi = lax.Precision.HIGHEST
    oh_r = (dr[:, :, None] == jnp.arange(2 * NA_ROWS - 1)).astype(jnp.float32)
    oh_c = (dc[None] == jnp.arange(2 * NA_COLS - 1)[:, None, None]).astype(jnp.float32)
    rows = jnp.einsum('krs,hsd->hkrd', oh_r, rpb.astype(jnp.float32), precision=hi)
    vals = jnp.einsum('hkrd,dcq->hkcrq', rows, oh_c, precision=hi)
    ok = row_ok[:, :, None, :, None] & col_ok[None, None, :, None, :]
    b = jnp.where(ok[:, None], vals[None], NEG)
    h = rpb.shape[0]
    b = b.reshape(3, h // 2, 2, NA_KEY_ROWS * GRID_W, NA_TQ)
    b = jnp.transpose(b, (0, 1, 3, 2, 4)).reshape(3, h // 2, NA_KEY_ROWS * GRID_W, 2 * NA_TQ)
    return b * LOG2E


def _layer_params(p, l):
    scale = LOG2E / math.sqrt(HEAD_DIM)
    gains = jnp.concatenate([p["qnorm_a"][l] * scale, p["knorm_a"][l],
                             p["qnorm_b"][l] * scale, p["knorm_b"][l]])
    wr = p["w_router"][l].T
    wr_hi = wr.astype(jnp.bfloat16)
    return dict(
        g_mix=p["norm_mix"][l][None, :],
        w_in_t=p["w_in"][l].T.astype(jnp.bfloat16),
        head_gains=jnp.broadcast_to(gains[:, None], (4 * HEAD_DIM, PROJ_TM)),
        sink_row=jnp.repeat(p["sink_a"][l].astype(jnp.float32) * LOG2E, WIN_SUB)[None, :],
        na_bias=_na_bias_t(p["rpb_b"][l]),
        g_a=p["onorm_a"][l][None, :],
        g_b=p["onorm_b"][l][None, :],
        w_out=p["w_out"][l].astype(jnp.bfloat16),
        g_ffn=p["norm_ffn"][l][None, :],
        wr_hilo=jnp.concatenate(
            [wr_hi, (wr - wr_hi.astype(jnp.float32)).astype(jnp.bfloat16)], axis=0),
        layer=l,
        w_gate=p["w_gate"],
        w_up=p["w_up"],
        w_down=p["w_down"],
    )


def _trunk(x, layers, win_bias):
    b, s, d = x.shape
    n = b * s
    assert s % WIN_TQ == 0 and s // NA_TQ >= 3 and n % PROJ_TM == 0 and n % RT_T == 0
    assert (n // RT_T) % RT_TPS == 0 and (n // RT_T) % min(RANK_TILES, n // RT_T) == 0
    cap = EC_CAPACITY * n // N_EXPERTS
    tm = min(FFN_TM, cap)
    assert cap % tm == 0 and tm >= max(RT_CH, DISP_CH)
    cpad = cap + tm
    x2 = x.reshape(n, d)
    for q in layers:
        qa_t, ka, va_t, qb_t, kb, vb_t = _in_proj(x2, q["g_mix"], q["w_in_t"], q["head_gains"])
        out_a = _window_attention(qa_t, ka, va_t, win_bias, q["sink_row"], s)
        out_b = _na_attention(qb_t, kb, vb_t, q["na_bias"], s)
        x1, h, aff_t = _post_attn(out_a, out_b, x2, q["g_a"], q["g_b"], q["w_out"],
                                  q["g_ffn"], q["wr_hilo"])
        nt = n // RT_T
        m4 = _retile(_select(aff_t, cap)).reshape(RT_T // RT_G, nt, RT_G, LANE)
        rel_t, lo = _rank(m4)
        lo_flat = lo.reshape(-1)
        xe = _dispatch(_granule_view(h, nt), rel_t, lo_flat, cap, cpad)
        ye = _expert_ffn(xe, q["w_gate"], q["w_up"], q["w_down"], q["layer"], cap, cpad, tm)
        x2 = _combine(_granule_view(x1, nt), rel_t, m4, ye, lo_flat, cap).reshape(n, d)
    return x2.reshape(b, s, d)


def kernel(x_prompt, x_sample, norm_mix, w_in, qnorm_a, knorm_a, sink_a, qnorm_b, knorm_b,
           rpb_b, onorm_a, onorm_b, w_out, norm_ffn, w_router, w_gate, w_up, w_down):
    p = dict(norm_mix=norm_mix, w_in=w_in, qnorm_a=qnorm_a, knorm_a=knorm_a, sink_a=sink_a,
             qnorm_b=qnorm_b, knorm_b=knorm_b, rpb_b=rpb_b, onorm_a=onorm_a, onorm_b=onorm_b,
             w_out=w_out, norm_ffn=norm_ffn, w_router=w_router, w_gate=w_gate, w_up=w_up,
             w_down=w_down)
    layers = [_layer_params(p, l) for l in range(w_in.shape[0])]
    win_bias = _window_bias_t()
    return (_trunk(x_prompt, layers, win_bias), _trunk(x_sample, layers, win_bias))
```

```python
import functools
import math

import jax
import jax.numpy as jnp
from jax import lax
from jax.experimental import pallas as pl
from jax.experimental.pallas import tpu as pltpu

HEAD_DIM = 64
N_HEADS_A = 8
N_KV_HEADS_A = 2
N_HEADS_B = 8
QA_W = N_HEADS_A * HEAD_DIM
KVA_W = N_KV_HEADS_A * HEAD_DIM
QKVB_W = N_HEADS_B * HEAD_DIM
PROJ_W = QA_W + 2 * KVA_W + 3 * QKVB_W
WINDOW = 128
GRID_W = 64
NA_ROWS = 8
NA_COLS = 16
N_EXPERTS = 16
EC_CAPACITY = 2
EPS = 1e-6
NEG = -1e30
LOG2E = 1.4426950408889634

LANE = 128
V7X_VMEM_BYTES = 64 * 1024 * 1024

PROJ_TM = 1024
WIN_TQ = 1024
WIN_SUB = WINDOW
NA_GROUP_ROWS = 4
NA_TQ = NA_GROUP_ROWS * GRID_W
NA_KEY_ROWS = 3 * NA_GROUP_ROWS
NA_KC = 128
POST_TM = 1024
FFN_TM = 512
FFN_TF = 512
RT_T = 256
RT_CH = 64
DISP_CH = 64
RT_G = 16
RT_TPS = 1
RETILE_ROWS = 2048
RANK_TILES = 8

_NT = (((1,), (1,)), ((), ()))


def _vmem_limit(nbytes):
    return int(min(nbytes, V7X_VMEM_BYTES - 4 * 1024 * 1024))


def _proj_kernel(x_ref, g_ref, w_ref, hg_ref,
                 qa_ref, ka_ref, va_ref, qb_ref, kb_ref, vb_ref):
    x = x_ref[...]
    ms = jnp.mean(x * x, axis=-1, keepdims=True)
    h = (x * lax.rsqrt(ms + EPS) * g_ref[...]).astype(jnp.bfloat16)

    def seg(lo, hi):
        return lax.dot_general(w_ref[lo:hi, :], h, _NT,
                               preferred_element_type=jnp.float32)

    def head_norm(blk, gain):
        ssq = jnp.sum(blk * blk, axis=0, keepdims=True)
        return blk * lax.rsqrt(ssq * (1.0 / HEAD_DIM) + EPS) * gain

    g_qa = hg_ref[0 * HEAD_DIM:1 * HEAD_DIM, :]
    g_ka = hg_ref[1 * HEAD_DIM:2 * HEAD_DIM, :]
    g_qb = hg_ref[2 * HEAD_DIM:3 * HEAD_DIM, :]
    g_kb = hg_ref[3 * HEAD_DIM:4 * HEAD_DIM, :]

    o = 0
    p = seg(o, o + QA_W)
    for hd in range(N_HEADS_A):
        r = slice(hd * HEAD_DIM, (hd + 1) * HEAD_DIM)
        qa_ref[r, :] = head_norm(p[r, :], g_qa).astype(qa_ref.dtype)
    o += QA_W
    p = seg(o, o + 2 * KVA_W)
    kn = jnp.concatenate(
        [head_norm(p[hd * HEAD_DIM:(hd + 1) * HEAD_DIM, :], g_ka)
         for hd in range(N_KV_HEADS_A)], axis=0)
    ka_ref[...] = kn.T.astype(ka_ref.dtype)
    va_ref[...] = p[KVA_W:2 * KVA_W, :].astype(va_ref.dtype)
    o += 2 * KVA_W
    p = seg(o, o + QKVB_W)
    for hd in range(N_HEADS_B):
        r = slice(hd * HEAD_DIM, (hd + 1) * HEAD_DIM)
        qb_ref[r, :] = head_norm(p[r, :], g_qb).astype(qb_ref.dtype)
    o += QKVB_W
    p = seg(o, o + QKVB_W)
    kn = jnp.concatenate(
        [head_norm(p[hd * HEAD_DIM:(hd + 1) * HEAD_DIM, :], g_kb)
         for hd in range(N_HEADS_B)], axis=0)
    kb_ref[...] = kn.T.astype(kb_ref.dtype)
    o += QKVB_W
    vb_ref[...] = seg(o, o + QKVB_W).astype(vb_ref.dtype)


def _in_proj(x2d, g_mix, w_in_t, head_gains):
    n, d = x2d.shape
    tm = PROJ_TM
    bf = jnp.bfloat16
    col = lambda i: (0, i)
    row = lambda i: (i, 0)
    const = lambda i: (0, 0)
    out_shape = (
        jax.ShapeDtypeStruct((QA_W, n), bf),
        jax.ShapeDtypeStruct((n, KVA_W), bf),
        jax.ShapeDtypeStruct((KVA_W, n), bf),
        jax.ShapeDtypeStruct((QKVB_W, n), bf),
        jax.ShapeDtypeStruct((n, QKVB_W), bf),
        jax.ShapeDtypeStruct((QKVB_W, n), bf),
    )
    out_specs = (
        pl.BlockSpec((QA_W, tm), col),
        pl.BlockSpec((tm, KVA_W), row),
        pl.BlockSpec((KVA_W, tm), col),
        pl.BlockSpec((QKVB_W, tm), col),
        pl.BlockSpec((tm, QKVB_W), row),
        pl.BlockSpec((QKVB_W, tm), col),
    )
    return pl.pallas_call(
        _proj_kernel,
        grid=(n // tm,),
        in_specs=[
            pl.BlockSpec((tm, d), row),
            pl.BlockSpec((1, d), const),
            pl.BlockSpec((PROJ_W, d), const),
            pl.BlockSpec((4 * HEAD_DIM, tm), const),
        ],
        out_specs=out_specs,
        out_shape=out_shape,
        compiler_params=pltpu.CompilerParams(
            dimension_semantics=("arbitrary",),
            vmem_limit_bytes=_vmem_limit(48 * 1024 * 1024)),
        name="in_proj",
    )(x2d, g_mix, w_in_t, head_gains)


def _fold(acc, v, op):
    return v if acc is None else op(acc, v)


def _window_kernel(blocks_per_seq, q_ref, kp_ref, kc_ref, kn_ref,
                   vp_ref, vc_ref, vn_ref, bias_ref, sink_ref, o_ref, s_scr, p_scr):
    i = pl.program_id(0)
    pos = i % blocks_per_seq
    pen_prev = jnp.where(pos == 0, NEG, 0.0).astype(jnp.float32)
    pen_next = jnp.where(pos == blocks_per_seq - 1, NEG, 0.0).astype(jnp.float32)

    sink = sink_ref[...]
    n_sub = WIN_TQ // WIN_SUB
    n_chunks = 3
    gq = N_HEADS_A // N_KV_HEADS_A
    zero = jnp.zeros((HEAD_DIM, WIN_SUB), jnp.bfloat16)
    krefs = (kp_ref, kc_ref, kn_ref)

    def key_block(kb):
        if kb == 0:
            return 0, 0
        if kb == n_sub + 1:
            return 2, 0
        return 1, (kb - 1) * WIN_SUB

    def qblock(j):
        cols = slice(j * WIN_SUB, (j + 1) * WIN_SUB)
        halves = []
        for kv in range(N_KV_HEADS_A):
            parts = []
            for hd in range(N_HEADS_A):
                if hd // gq == kv:
                    parts.append(q_ref[hd * HEAD_DIM:(hd + 1) * HEAD_DIM, cols])
                else:
                    parts.append(zero)
            halves.append(jnp.concatenate(parts, axis=1))
        return jnp.concatenate(halves, axis=0)

    def score_chunk(j, c, qblk):
        rows = slice(c * WIN_SUB, (c + 1) * WIN_SUB)
        r, off = key_block(j + c)
        s = jnp.dot(krefs[r][off:off + WIN_SUB, :], qblk,
                    preferred_element_type=jnp.float32)
        s = s + bias_ref[rows, :]
        if j + c == 0:
            s = s + pen_prev
        if j + c == n_sub + 1:
            s = s + pen_next
        s_scr[j % 2, rows, :] = s
        return jnp.max(s, axis=0, keepdims=True)

    def prob_chunk(j, c, m):
        rows = slice(c * WIN_SUB, (c + 1) * WIN_SUB)
        p_scr[j % 2, rows, :] = jnp.exp2(s_scr[j % 2, rows, :] - m).astype(jnp.bfloat16)

    def finish(j, m):
        cols = slice(j * WIN_SUB, (j + 1) * WIN_SUB)
        vparts = []
        for c in range(n_chunks):
            r, off = key_block(j + c)
            vparts.append((vp_ref, vc_ref, vn_ref)[r][:, off:off + WIN_SUB])
        vwin = jnp.concatenate(vparts, axis=1)
        ones = jnp.ones((16, n_chunks * WIN_SUB), jnp.bfloat16)
        sink_term = jnp.exp2(sink - m)
        outs = []
        for kv in range(N_KV_HEADS_A):
            lanes = slice(kv * gq * WIN_SUB, (kv + 1) * gq * WIN_SUB)
            vt = jnp.concatenate([vwin[kv * HEAD_DIM:(kv + 1) * HEAD_DIM, :], ones], axis=0)
            o_t = jnp.dot(vt, p_scr[j % 2, :, lanes],
                          preferred_element_type=jnp.float32)
            o_t = o_t[:HEAD_DIM] / (o_t[HEAD_DIM:HEAD_DIM + 1] + sink_term[:, lanes])
            for g in range(gq):
                outs.append(o_t[:, g * WIN_SUB:(g + 1) * WIN_SUB])
        for a in range(N_HEADS_A // 2):
            pair = jnp.concatenate([outs[2 * a], outs[2 * a + 1]], axis=0)
            o_ref[cols, a * LANE:(a + 1) * LANE] = pair.T.astype(o_ref.dtype)

    qb = qblock(0)
    m = None
    for c in range(n_chunks):
        m = _fold(m, score_chunk(0, c, qb), jnp.maximum)
    m = jnp.maximum(m, sink)
    m_done = None
    for j in range(n_sub):
        m_next = None
        if j + 1 < n_sub:
            qb = qblock(j + 1)
        for c in range(n_chunks):
            prob_chunk(j, c, m)
            if j + 1 < n_sub:
                m_next = _fold(m_next, score_chunk(j + 1, c, qb), jnp.maximum)
            if c == 0 and j >= 1:
                finish(j - 1, m_done)
        m_done = m
        if j + 1 < n_sub:
            m = jnp.maximum(m_next, sink)
    finish(n_sub - 1, m_done)


def _window_attention(qa_t, ka, va_t, bias_t, sink_row, seq_len):
    n = ka.shape[0]
    nblk = n // WIN_TQ
    bps = seq_len // WIN_TQ
    r = WIN_TQ // WIN_SUB
    nsub = n // WIN_SUB
    prev_i = lambda i: jnp.maximum(r * i - 1, 0)
    next_i = lambda i: jnp.minimum(r * i + r, nsub - 1)
    const = lambda i: (0, 0)
    return pl.pallas_call(
        functools.partial(_window_kernel, bps),
        grid=(nblk,),
        in_specs=[
            pl.BlockSpec((QA_W, WIN_TQ), lambda i: (0, i)),
            pl.BlockSpec((WIN_SUB, KVA_W), lambda i: (prev_i(i), 0)),
            pl.BlockSpec((WIN_TQ, KVA_W), lambda i: (i, 0)),
            pl.BlockSpec((WIN_SUB, KVA_W), lambda i: (next_i(i), 0)),
            pl.BlockSpec((KVA_W, WIN_SUB), lambda i: (0, prev_i(i))),
            pl.BlockSpec((KVA_W, WIN_TQ), lambda i: (0, i)),
            pl.BlockSpec((KVA_W, WIN_SUB), lambda i: (0, next_i(i))),
            pl.BlockSpec((3 * WIN_SUB, N_HEADS_A * WIN_SUB), const),
            pl.BlockSpec((1, N_HEADS_A * WIN_SUB), const),
        ],
        out_specs=pl.BlockSpec((WIN_TQ, QA_W), lambda i: (i, 0)),
        out_shape=jax.ShapeDtypeStruct((n, QA_W), jnp.bfloat16),
        scratch_shapes=[
            pltpu.VMEM((2, 3 * WIN_SUB, N_HEADS_A * WIN_SUB), jnp.float32),
            pltpu.VMEM((2, 3 * WIN_SUB, N_HEADS_A * WIN_SUB), jnp.bfloat16),
        ],
        compiler_params=pltpu.CompilerParams(
            dimension_semantics=("arbitrary",),
            vmem_limit_bytes=_vmem_limit(40 * 1024 * 1024)),
        name="window_attn",
    )(qa_t, ka, ka, ka, va_t, va_t, va_t, bias_t, sink_row)


def _na_kernel(q_ref, kp_ref, kc_ref, kn_ref, vp_ref, vc_ref, vn_ref,
               bias_ref, o_ref, s_scr, p_scr):
    zero = jnp.zeros((HEAD_DIM, NA_TQ), jnp.bfloat16)
    n_keys = NA_KEY_ROWS * GRID_W
    n_chunks = n_keys // NA_KC
    n_pairs = N_HEADS_B // 2
    krefs = (kp_ref, kc_ref, kn_ref)

    def qblock(pr):
        q0 = q_ref[(2 * pr) * HEAD_DIM:(2 * pr + 1) * HEAD_DIM, :]
        q1 = q_ref[(2 * pr + 1) * HEAD_DIM:(2 * pr + 2) * HEAD_DIM, :]
        return jnp.concatenate(
            [jnp.concatenate([q0, zero], axis=1),
             jnp.concatenate([zero, q1], axis=1)], axis=0)

    def score_chunk(pr, c, qblk):
        rows = slice(c * NA_KC, (c + 1) * NA_KC)
        blk, off = divmod(c * NA_KC, NA_TQ)
        kchunk = krefs[blk][off:off + NA_KC, pr * LANE:(pr + 1) * LANE]
        s = jnp.dot(kchunk, qblk, preferred_element_type=jnp.float32)
        s = s + bias_ref[0, pr, rows, :]
        s_scr[pr % 2, rows, :] = s
        return jnp.max(s, axis=0, keepdims=True)

    def prob_chunk(pr, c, m):
        rows = slice(c * NA_KC, (c + 1) * NA_KC)
        p_scr[pr % 2, rows, :] = jnp.exp2(s_scr[pr % 2, rows, :] - m).astype(jnp.bfloat16)

    def finish(pr):
        lanes = slice(pr * LANE, (pr + 1) * LANE)
        vwin = jnp.concatenate(
            [vp_ref[lanes, :], vc_ref[lanes, :], vn_ref[lanes, :]], axis=1)
        ones = jnp.ones((16, n_keys), jnp.bfloat16)
        outs = []
        for t in range(2):
            cols = slice(t * NA_TQ, (t + 1) * NA_TQ)
            vt = jnp.concatenate([vwin[t * HEAD_DIM:(t + 1) * HEAD_DIM, :], ones], axis=0)
            o_t = jnp.dot(vt, p_scr[pr % 2, :, cols],
                          preferred_element_type=jnp.float32)
            outs.append(o_t[:HEAD_DIM] / o_t[HEAD_DIM:HEAD_DIM + 1])
        pair = jnp.concatenate(outs, axis=0)
        o_ref[:, lanes] = pair.T.astype(o_ref.dtype)

    qb = qblock(0)
    m = None
    for c in range(n_chunks):
        m = _fold(m, score_chunk(0, c, qb), jnp.maximum)
    for pr in range(n_pairs):
        m_next = None
        if pr + 1 < n_pairs:
            qb = qblock(pr + 1)
        for c in range(n_chunks):
            prob_chunk(pr, c, m)
            if pr + 1 < n_pairs:
                m_next = _fold(m_next, score_chunk(pr + 1, c, qb), jnp.maximum)
            if c == 0 and pr >= 1:
                finish(pr - 1)
        m = m_next
    finish(n_pairs - 1)


def _na_attention(qb_t, kb, vb_t, bias, seq_len):
    n = kb.shape[0]
    ng = n // NA_TQ
    gps = seq_len // NA_TQ
    prev_i = lambda g: jnp.maximum(g - 1, 0)
    next_i = lambda g: jnp.minimum(g + 1, ng - 1)

    def variant(g):
        pos = g % gps
        return jnp.where(pos == 0, 0, jnp.where(pos == gps - 1, 2, 1))

    return pl.pallas_call(
        _na_kernel,
        grid=(ng,),
        in_specs=[
            pl.BlockSpec((QKVB_W, NA_TQ), lambda g: (0, g)),
            pl.BlockSpec((NA_TQ, QKVB_W), lambda g: (prev_i(g), 0)),
            pl.BlockSpec((NA_TQ, QKVB_W), lambda g: (g, 0)),
            pl.BlockSpec((NA_TQ, QKVB_W), lambda g: (next_i(g), 0)),
            pl.BlockSpec((QKVB_W, NA_TQ), lambda g: (0, prev_i(g))),
            pl.BlockSpec((QKVB_W, NA_TQ), lambda g: (0, g)),
            pl.BlockSpec((QKVB_W, NA_TQ), lambda g: (0, next_i(g))),
            pl.BlockSpec((1, N_HEADS_B // 2, NA_KEY_ROWS * GRID_W, 2 * NA_TQ),
                         lambda g: (variant(g), 0, 0, 0)),
        ],
        out_specs=pl.BlockSpec((NA_TQ, QKVB_W), lambda g: (g, 0)),
        out_shape=jax.ShapeDtypeStruct((n, QKVB_W), jnp.bfloat16),
        scratch_shapes=[
            pltpu.VMEM((2, NA_KEY_ROWS * GRID_W, 2 * NA_TQ), jnp.float32),
            pltpu.VMEM((2, NA_KEY_ROWS * GRID_W, 2 * NA_TQ), jnp.bfloat16),
        ],
        compiler_params=pltpu.CompilerParams(
            dimension_semantics=("arbitrary",),
            vmem_limit_bytes=_vmem_limit(48 * 1024 * 1024)),
        name="na_attn",
    )(qb_t, kb, kb, kb, vb_t, vb_t, vb_t, bias)


def _post_kernel(a_ref, b_ref, x_ref, ga_ref, gb_ref, w_ref, gf_ref,
                 wr_ref, x1_ref, h_ref, aff_ref):
    def rms(v, g):
        ms = jnp.mean(v * v, axis=-1, keepdims=True)
        return v * lax.rsqrt(ms + EPS) * g

    an = rms(a_ref[...].astype(jnp.float32), ga_ref[...]).astype(jnp.bfloat16)
    bn = rms(b_ref[...].astype(jnp.float32), gb_ref[...]).astype(jnp.bfloat16)
    y = jnp.dot(an, w_ref[:QA_W, :], preferred_element_type=jnp.float32)
    y = y + jnp.dot(bn, w_ref[QA_W:, :], preferred_element_type=jnp.float32)
    x1 = x_ref[...] + y
    x1_ref[...] = x1
    h = rms(x1, gf_ref[...])
    h_hi = h.astype(jnp.bfloat16)
    h_lo = (h - h_hi.astype(jnp.float32)).astype(jnp.bfloat16)
    h_ref[...] = h_hi
    both = lax.dot_general(wr_ref[...], h_hi, _NT, preferred_element_type=jnp.float32)
    logits = both[:N_EXPERTS] + both[N_EXPERTS:]
    logits = logits + lax.dot_general(wr_ref[:N_EXPERTS, :], h_lo, _NT,
                                      preferred_element_type=jnp.float32)
    m = jnp.max(logits, axis=0, keepdims=True)
    e = jnp.exp(logits - m)
    aff_ref[...] = e / jnp.sum(e, axis=0, keepdims=True)


def _post_attn(out_a, out_b, x2d, g_a, g_b, w_out, g_ffn, wr_hilo):
    n, d = x2d.shape
    tm = POST_TM
    row = lambda i: (i, 0)
    const = lambda i: (0, 0)
    return pl.pallas_call(
        _post_kernel,
        grid=(n // tm,),
        in_specs=[
            pl.BlockSpec((tm, QA_W), row),
            pl.BlockSpec((tm, QKVB_W), row),
            pl.BlockSpec((tm, d), row),
            pl.BlockSpec((1, QA_W), const),
            pl.BlockSpec((1, QKVB_W), const),
            pl.BlockSpec((QA_W + QKVB_W, d), const),
            pl.BlockSpec((1, d), const),
            pl.BlockSpec((2 * N_EXPERTS, d), const),
        ],
        out_specs=(
            pl.BlockSpec((tm, d), row),
            pl.BlockSpec((tm, d), row),
            pl.BlockSpec((N_EXPERTS, tm), lambda i: (0, i)),
        ),
        out_shape=(
            jax.ShapeDtypeStruct((n, d), jnp.float32),
            jax.ShapeDtypeStruct((n, d), jnp.bfloat16),
            jax.ShapeDtypeStruct((N_EXPERTS, n), jnp.float32),
        ),
        compiler_params=pltpu.CompilerParams(
            dimension_semantics=("arbitrary",),
            vmem_limit_bytes=_vmem_limit(40 * 1024 * 1024)),
        name="post_attn",
    )(out_a, out_b, x2d, g_a, g_b, w_out, g_ffn, wr_hilo)


def _strict_upper(n):
    r = lax.broadcasted_iota(jnp.int32, (n, n), 0)
    c = lax.broadcasted_iota(jnp.int32, (n, n), 1)
    return jnp.where(r < c, 1.0, 0.0).astype(jnp.bfloat16)


def _select_kernel(cap, aff_ref, sel_ref):
    n = aff_ref.shape[1]
    nt = n // RT_T
    cap_f = jnp.float32(cap)

    def count(mask):
        return jnp.sum(jnp.where(mask, 1.0, 0.0), axis=1, keepdims=True)

    def search(b, ans):
        cand = ans | jnp.left_shift(jnp.int32(1), 30 - b)
        bits = pltpu.bitcast(aff_ref[...], jnp.int32)
        return jnp.where(count(bits >= cand) >= cap_f, cand, ans)

    thr = lax.fori_loop(0, 31, search, jnp.zeros((N_EXPERTS, 1), jnp.int32))
    need = cap_f - count(pltpu.bitcast(aff_ref[...], jnp.int32) > thr)
    tri = _strict_upper(RT_T)

    def tile(c, run_eq):
        start = pl.multiple_of(c * RT_T, RT_T)
        bits = pltpu.bitcast(aff_ref[:, pl.ds(start, RT_T)], jnp.int32)
        eq = bits == thr
        eq_b = jnp.where(eq, 1.0, 0.0).astype(jnp.bfloat16)
        eq_rank = jnp.dot(eq_b, tri, preferred_element_type=jnp.float32)
        sel = (bits > thr) | (eq & (run_eq + eq_rank < need))
        sel_ref[:, pl.ds(start, RT_T)] = jnp.where(sel, aff_ref[:, pl.ds(start, RT_T)], -1.0)
        return run_eq + count(eq)

    lax.fori_loop(0, nt, tile, need * 0.0)


def _retile_kernel(m_ref, o_ref):
    pad = jnp.zeros((LANE - N_EXPERTS, RT_T), jnp.float32)
    for q in range(m_ref.shape[1] // RT_T):
        cols = slice(q * RT_T, (q + 1) * RT_T)
        o_ref[cols, :] = jnp.concatenate([m_ref[:, cols], pad], axis=0).T


def _tile_columns(m_ref, t):
    rows = m_ref[:, t].reshape(RT_T, LANE)
    return rows.T[:N_EXPERTS, :]


def _rank_kernel(nt, m_ref, rel_ref, lo_ref, run_ref):
    jb = pl.program_id(0)
    tb = m_ref.shape[1]
    w = lo_ref.shape[1]
    tri = _strict_upper(RT_T)
    lane = lax.broadcasted_iota(jnp.int32, (N_EXPERTS, w), 1)

    @pl.when(jb == 0)
    def _():
        lo_ref[...] = jnp.zeros_like(lo_ref)
        run_ref[...] = jnp.zeros_like(run_ref)

    for t in range(tb):
        sel = _tile_columns(m_ref, t) >= 0.0
        sel_f = jnp.where(sel, 1.0, 0.0)
        rank = jnp.dot(sel_f.astype(jnp.bfloat16), tri, preferred_element_type=jnp.float32)
        rel_ref[:, t * RT_T:(t + 1) * RT_T] = jnp.where(sel, rank, -1.0).astype(jnp.int32)
        run = run_ref[...]
        lo_ref[...] = jnp.where(lane == jb * tb + t, run.astype(jnp.int32), lo_ref[...])
        run_ref[...] = run + jnp.sum(sel_f, axis=1, keepdims=True)

    @pl.when(jb == pl.num_programs(0) - 1)
    def _():
        lo_ref[...] = jnp.where(lane >= nt, run_ref[...].astype(jnp.int32), lo_ref[...])


def _select(aff_t, cap):
    e, n = aff_t.shape
    full = lambda i: (0, 0)
    return pl.pallas_call(
        functools.partial(_select_kernel, cap),
        grid=(1,),
        in_specs=[pl.BlockSpec((e, n), full)],
        out_specs=pl.BlockSpec((e, n), full),
        out_shape=jax.ShapeDtypeStruct((e, n), jnp.float32),
        compiler_params=pltpu.CompilerParams(
            dimension_semantics=("arbitrary",),
            vmem_limit_bytes=_vmem_limit(40 * 1024 * 1024)),
        name="route_select",
    )(aff_t)


def _retile(m_t):
    e, n = m_t.shape
    rb = min(RETILE_ROWS, n)
    return pl.pallas_call(
        _retile_kernel,
        grid=(n // rb,),
        in_specs=[pl.BlockSpec((e, rb), lambda i: (0, i))],
        out_specs=pl.BlockSpec((rb, LANE), lambda i: (i, 0)),
        out_shape=jax.ShapeDtypeStruct((n, LANE), jnp.float32),
        compiler_params=pltpu.CompilerParams(dimension_semantics=("arbitrary",)),
        name="route_retile",
    )(m_t)


def _rank(m4):
    g, nt, rg, lanes = m4.shape
    tb = min(RANK_TILES, nt)
    w = nt + LANE
    return pl.pallas_call(
        functools.partial(_rank_kernel, nt),
        grid=(nt // tb,),
        in_specs=[pl.BlockSpec((g, tb, rg, lanes), lambda i: (0, i, 0, 0))],
        out_specs=(pl.BlockSpec((N_EXPERTS, tb * RT_T), lambda i: (0, i)),
                   pl.BlockSpec((N_EXPERTS, w), lambda i: (0, 0))),
        out_shape=(jax.ShapeDtypeStruct((N_EXPERTS, nt * RT_T), jnp.int32),
                   jax.ShapeDtypeStruct((N_EXPERTS, w), jnp.int32)),
        scratch_shapes=[pltpu.VMEM((N_EXPERTS, 1), jnp.float32)],
        compiler_params=pltpu.CompilerParams(dimension_semantics=("arbitrary",)),
        name="route_rank",
    )(m4)


def _granule_view(x, nt):
    n, d = x.shape
    return x.reshape(RT_T // RT_G, nt, RT_G, d)


def _pack_pairs(x):
    w = x.shape[1] // 2
    lo = pltpu.bitcast(x[:, :w], jnp.uint32)
    hi = pltpu.bitcast(x[:, w:], jnp.uint32)
    return lo | (hi >> 16)


def _unpack_pairs(p):
    lo = pltpu.bitcast(p & jnp.uint32(0xFFFF0000), jnp.float32).astype(jnp.bfloat16)
    hi = pltpu.bitcast(p << 16, jnp.float32).astype(jnp.bfloat16)
    return lo, hi


def _one_hot_rows(rel_ref, cols, shift):
    kio = lax.broadcasted_iota(jnp.int32, (DISP_CH, RT_T), 0)
    blocks = []
    for e in range(N_EXPERTS):
        hit = (rel_ref[e:e + 1, cols] - shift) == kio
        blocks.append(jnp.where(hit, 1.0, 0.0).astype(jnp.bfloat16))
    return jnp.concatenate(blocks, axis=0)


def _dispatch_kernel(cap, cpad, nt, w, lo_ref, h_ref, rel_ref, xe_hbm,
                     stage, stage_x, sem, sem_x):
    for t in range(RT_TPS):
        _dispatch_tile(cap, cpad, nt, w, pl.program_id(0) * RT_TPS + t, t,
                       lo_ref, h_ref, rel_ref, xe_hbm, stage, stage_x, sem, sem_x)


def _dispatch_tile(cap, cpad, nt, w, j, t, lo_ref, h_ref, rel_ref, xe_hbm,
                   stage, stage_x, sem, sem_x):
    slot = j % 2
    cols = slice(t * RT_T, (t + 1) * RT_T)

    def dst(e, jj, c):
        row = e * cpad + lo_ref[e * w + jj] + c * DISP_CH
        return xe_hbm.at[pl.ds(row, DISP_CH), 0]

    def chunk_copy(e, jj, sl):
        return pltpu.make_async_copy(
            stage.at[sl, pl.ds(e * DISP_CH, DISP_CH)], dst(e, jj, 0), sem.at[sl])

    @pl.when(j == 0)
    def _():
        pad = cpad - cap
        stage_x[...] = jnp.zeros_like(stage_x)
        fills = [pltpu.make_async_copy(
            stage_x.at[pl.ds(0, pad)], xe_hbm.at[pl.ds(e * cpad + cap, pad), 0], sem_x)
            for e in range(N_EXPERTS)]
        for f in fills:
            f.start()
        for f in fills:
            f.wait()

    h_tile = h_ref[:, t].reshape(RT_T, h_ref.shape[-1])
    x = jnp.dot(_one_hot_rows(rel_ref, cols, 0), h_tile, preferred_element_type=jnp.float32)
    stage[slot] = _pack_pairs(x)

    @pl.when(j > 0)
    def _():
        for e in range(N_EXPERTS):
            chunk_copy(e, j - 1, 1 - slot).wait()

    for e in range(N_EXPERTS):
        chunk_copy(e, j, slot).start()

    cnts = [lo_ref[e * w + j + 1] - lo_ref[e * w + j] for e in range(N_EXPERTS)]
    most = functools.reduce(jnp.maximum, cnts)
    n_pass = lax.div(most + (DISP_CH - 1), jnp.int32(DISP_CH))

    def extra(c, carry):
        kio = lax.broadcasted_iota(jnp.int32, (DISP_CH, RT_T), 0)

        def extra_copy(e):
            return pltpu.make_async_copy(
                stage_x.at[pl.ds(e * DISP_CH, DISP_CH)], dst(e, j, c), sem_x)

        for e in range(N_EXPERTS):
            @pl.when(cnts[e] > c * DISP_CH)
            def _():
                hit = (rel_ref[e:e + 1, cols] - c * DISP_CH) == kio
                xx = jnp.dot(jnp.where(hit, 1.0, 0.0).astype(jnp.bfloat16),
                             h_ref[:, t].reshape(RT_T, h_ref.shape[-1]),
                             preferred_element_type=jnp.float32)
                stage_x[e * DISP_CH:(e + 1) * DISP_CH, :] = _pack_pairs(xx)
                extra_copy(e).start()
        for e in range(N_EXPERTS):
            @pl.when(cnts[e] > c * DISP_CH)
            def _():
                extra_copy(e).wait()
        return carry

    lax.fori_loop(1, n_pass, extra, 0)

    @pl.when(j == nt - 1)
    def _():
        for e in range(N_EXPERTS):
            chunk_copy(e, j, slot).wait()


def _dispatch(h4, rel_t, lo_flat, cap, cpad):
    _, nt, _, d = h4.shape
    w = lo_flat.shape[0] // N_EXPERTS
    rows = N_EXPERTS * DISP_CH
    assert cpad - cap <= rows
    return pl.pallas_call(
        functools.partial(_dispatch_kernel, cap, cpad, nt, w),
        grid_spec=pltpu.PrefetchScalarGridSpec(
            num_scalar_prefetch=1,
            grid=(nt // RT_TPS,),
            in_specs=[
                pl.BlockSpec((RT_T // RT_G, RT_TPS, RT_G, d), lambda j, lo: (0, j, 0, 0)),
                pl.BlockSpec((N_EXPERTS, RT_TPS * RT_T), lambda j, lo: (0, j)),
            ],
            out_specs=pl.BlockSpec(memory_space=pl.ANY),
            scratch_shapes=[
                pltpu.VMEM((2, rows, d // 2), jnp.uint32),
                pltpu.VMEM((rows, d // 2), jnp.uint32),
                pltpu.SemaphoreType.DMA((2,)),
                pltpu.SemaphoreType.DMA(()),
            ],
        ),
        out_shape=jax.ShapeDtypeStruct((N_EXPERTS * cpad, 1, d // 2), jnp.uint32),
        compiler_params=pltpu.CompilerParams(
            dimension_semantics=("arbitrary",),
            vmem_limit_bytes=_vmem_limit(40 * 1024 * 1024)),
        name="dispatch",
    )(lo_flat, h4, rel_t)


def _ffn_kernel(layer, cpad, tiles, tm, x_hbm, wg_hbm, wu_hbm, wd_hbm, o_ref,
                xbuf, wg_b, wu_b, wd_b, stg_g, stg_u, stg_d, xsem, wsem):
    i = pl.program_id(0)
    j = pl.program_id(1)
    n_exp = pl.num_programs(0)
    step = i * tiles + j
    slot = step % 2
    prev_slot = (step + 1) % 2
    rg = wg_b.shape[1] // tiles
    rd = wd_b.shape[1] // tiles

    def x_copy(ii, jj, sl):
        return pltpu.make_async_copy(
            x_hbm.at[pl.ds(ii * cpad + jj * tm, tm), 0], xbuf.at[sl], xsem.at[sl])

    def slab_copies(e, k, sl):
        r_g = pl.multiple_of(k * rg, rg)
        r_d = pl.multiple_of(k * rd, rd)
        return (
            pltpu.make_async_copy(wg_hbm.at[layer, e, pl.ds(r_g, rg)], stg_g.at[sl], wsem.at[sl]),
            pltpu.make_async_copy(wu_hbm.at[layer, e, pl.ds(r_g, rg)], stg_u.at[sl], wsem.at[sl]),
            pltpu.make_async_copy(wd_hbm.at[layer, e, pl.ds(r_d, rd)], stg_d.at[sl], wsem.at[sl]),
        )

    def cast_slab(wslot, k, sl):
        r_g = pl.multiple_of(k * rg, rg)
        r_d = pl.multiple_of(k * rd, rd)
        wg_b[wslot, pl.ds(r_g, rg), :] = stg_g[sl].astype(jnp.bfloat16)
        wu_b[wslot, pl.ds(r_g, rg), :] = stg_u[sl].astype(jnp.bfloat16)
        wd_b[wslot, pl.ds(r_d, rd), :] = stg_d[sl].astype(jnp.bfloat16)

    @pl.when(step == 0)
    def _():
        x_copy(0, 0, 0).start()

        def load(k, carry):
            sl = (k + tiles) % 2
            for cp in slab_copies(0, k, sl):
                cp.start()
            for cp in slab_copies(0, k, sl):
                cp.wait()
            cast_slab(0, k, sl)
            return carry

        lax.fori_loop(0, tiles, load, 0)

    j_prev = jnp.where(j > 0, j - 1, tiles - 1)
    e_prev = jnp.where(j > 0, i + 1, i)
    pending = jnp.logical_and(step > 0, e_prev < n_exp)

    @pl.when(pending)
    def _():
        for cp in slab_copies(e_prev, j_prev, prev_slot):
            cp.wait()

    @pl.when(i + 1 < n_exp)
    def _():
        for cp in slab_copies(i + 1, j, slot):
            cp.start()

    cast_slab(e_prev % 2, j_prev, prev_slot)

    @pl.when(step + 1 < n_exp * tiles)
    def _():
        wrap = j + 1 == tiles
        x_copy(jnp.where(wrap, i + 1, i), jnp.where(wrap, 0, j + 1), 1 - slot).start()

    x_copy(i, j, slot).wait()
    x_lo, x_hi = _unpack_pairs(xbuf[slot])
    half = x_lo.shape[1]
    d_ff = wg_b.shape[2]
    wg, wu, wd = wg_b.at[i % 2], wu_b.at[i % 2], wd_b.at[i % 2]
    acc = None
    for c in range(d_ff // FFN_TF):
        f = slice(c * FFN_TF, (c + 1) * FFN_TF)
        g = jnp.dot(x_lo, wg[:half, f], preferred_element_type=jnp.float32)
        g = g + jnp.dot(x_hi, wg[half:, f], preferred_element_type=jnp.float32)
        u = jnp.dot(x_lo, wu[:half, f], preferred_element_type=jnp.float32)
        u = u + jnp.dot(x_hi, wu[half:, f], preferred_element_type=jnp.float32)
        act = (g * jax.nn.sigmoid(g) * u).astype(jnp.bfloat16)
        part = jnp.dot(act, wd[f, :], preferred_element_type=jnp.float32)
        acc = part if acc is None else acc + part
    o_ref[...] = acc.astype(o_ref.dtype)


def _expert_ffn(xe, w_gate, w_up, w_down, layer, cap, cpad, tm):
    _, e, d, d_ff = w_gate.shape
    tiles = cap // tm
    assert d % tiles == 0 and (d // tiles) % 16 == 0
    any_spec = pl.BlockSpec(memory_space=pl.ANY)
    return pl.pallas_call(
        functools.partial(_ffn_kernel, layer, cpad, tiles, tm),
        grid=(e, tiles),
        in_specs=[any_spec, any_spec, any_spec, any_spec],
        out_specs=pl.BlockSpec((tm, d), lambda i, j: (i * tiles + j, 0)),
        out_shape=jax.ShapeDtypeStruct((e * cap, d), jnp.bfloat16),
        scratch_shapes=[
            pltpu.VMEM((2, tm, d // 2), jnp.uint32),
            pltpu.VMEM((2, d, d_ff), jnp.bfloat16),
            pltpu.VMEM((2, d, d_ff), jnp.bfloat16),
            pltpu.VMEM((2, d_ff, d), jnp.bfloat16),
            pltpu.VMEM((2, d // tiles, d_ff), jnp.float32),
            pltpu.VMEM((2, d // tiles, d_ff), jnp.float32),
            pltpu.VMEM((2, d_ff // tiles, d), jnp.float32),
            pltpu.SemaphoreType.DMA((2,)),
            pltpu.SemaphoreType.DMA((2,)),
        ],
        compiler_params=pltpu.CompilerParams(
            dimension_semantics=("arbitrary", "arbitrary"),
            vmem_limit_bytes=_vmem_limit(58 * 1024 * 1024)),
        name="expert_ffn",
    )(xe, w_gate, w_up, w_down)


def _combine_kernel(cap, nt, w, lo_ref, x_ref, rel_ref, m_ref, ye_hbm, o_ref,
                    ybuf, ybuf_x, gate_ref, sem, sem_x):
    for t in range(RT_TPS):
        _combine_tile(cap, nt, w, pl.program_id(0) * RT_TPS + t, t, lo_ref, x_ref, rel_ref,
                      m_ref, ye_hbm, o_ref, ybuf, ybuf_x, gate_ref, sem, sem_x)


def _combine_tile(cap, nt, w, j, t, lo_ref, x_ref, rel_ref, m_ref, ye_hbm, o_ref,
                  ybuf, ybuf_x, gate_ref, sem, sem_x):
    slot = j % 2
    cols = slice(t * RT_T, (t + 1) * RT_T)
    tile_shape = (o_ref.shape[0],) + o_ref.shape[2:]
    last_start = N_EXPERTS * cap - RT_CH
    align = 16

    def start_row(e, jj, c):
        lo = lo_ref[e * w + jj]
        a = e * cap + lo - (lo & (align - 1)) + c * RT_CH
        return pl.multiple_of(jnp.minimum(a, last_start), align)

    def fetch(e, jj, sl):
        return pltpu.make_async_copy(
            ye_hbm.at[pl.ds(start_row(e, jj, 0), RT_CH)],
            ybuf.at[sl, pl.ds(e * RT_CH, RT_CH)], sem.at[sl])

    @pl.when(j == 0)
    def _():
        for e in range(N_EXPERTS):
            fetch(e, 0, 0).start()

    @pl.when(j + 1 < nt)
    def _():
        for e in range(N_EXPERTS):
            fetch(e, j + 1, 1 - slot).start()

    gate_ref[...] = _tile_columns(m_ref, t)
    los = [lo_ref[e * w + j] for e in range(N_EXPERTS)]
    cnts = [lo_ref[e * w + j + 1] - los[e] for e in range(N_EXPERTS)]
    lead = [los[e] & (align - 1) for e in range(N_EXPERTS)]

    def weight_block(e, c):
        kio = lax.broadcasted_iota(jnp.int32, (RT_CH, RT_T), 0)
        r = rel_ref[e:e + 1, cols]
        p = r + lead[e]
        member = (r >= 0) & (p >= c * RT_CH) & (p < (c + 1) * RT_CH)
        off = e * cap + los[e] - start_row(e, j, c)
        hit = member & ((r + off) == kio)
        return jnp.where(hit, gate_ref[e:e + 1, :], 0.0).astype(jnp.bfloat16)

    tn = (((0,), (0,)), ((), ()))
    wt0 = jnp.concatenate([weight_block(e, 0) for e in range(N_EXPERTS)], axis=0)
    for e in range(N_EXPERTS):
        fetch(e, j, slot).wait()
    o_ref[:, t] = x_ref[:, t] + lax.dot_general(
        wt0, ybuf[slot], tn, preferred_element_type=jnp.float32).reshape(tile_shape)

    spans = [lead[e] + cnts[e] for e in range(N_EXPERTS)]
    most = functools.reduce(jnp.maximum, spans)
    n_pass = jnp.right_shift(most + (RT_CH - 1), RT_CH.bit_length() - 1)

    def extra(c, carry):
        def extra_fetch(e):
            return pltpu.make_async_copy(
                ye_hbm.at[pl.ds(start_row(e, j, c), RT_CH)],
                ybuf_x.at[pl.ds(e * RT_CH, RT_CH)], sem_x)

        for e in range(N_EXPERTS):
            @pl.when(spans[e] > c * RT_CH)
            def _():
                extra_fetch(e).start()
        for e in range(N_EXPERTS):
            @pl.when(spans[e] > c * RT_CH)
            def _():
                extra_fetch(e).wait()
        for e in range(N_EXPERTS):
            @pl.when(spans[e] > c * RT_CH)
            def _():
                wte = weight_block(e, c)
                o_ref[:, t] += lax.dot_general(
                    wte, ybuf_x[e * RT_CH:(e + 1) * RT_CH, :], tn,
                    preferred_element_type=jnp.float32).reshape(tile_shape)
        return carry

    lax.fori_loop(1, n_pass, extra, 0)


def _combine(x4, rel_t, m4, ye, lo_flat, cap):
    _, nt, _, d = x4.shape
    w = lo_flat.shape[0] // N_EXPERTS
    rows = N_EXPERTS * RT_CH
    return pl.pallas_call(
        functools.partial(_combine_kernel, cap, nt, w),
        grid_spec=pltpu.PrefetchScalarGridSpec(
            num_scalar_prefetch=1,
            grid=(nt // RT_TPS,),
            in_specs=[
                pl.BlockSpec((RT_T // RT_G, RT_TPS, RT_G, d), lambda j, lo: (0, j, 0, 0)),
                pl.BlockSpec((N_EXPERTS, RT_TPS * RT_T), lambda j, lo: (0, j)),
                pl.BlockSpec((RT_T // RT_G, RT_TPS, RT_G, LANE), lambda j, lo: (0, j, 0, 0)),
                pl.BlockSpec(memory_space=pl.ANY),
            ],
            out_specs=pl.BlockSpec((RT_T // RT_G, RT_TPS, RT_G, d), lambda j, lo: (0, j, 0, 0)),
            scratch_shapes=[
                pltpu.VMEM((2, rows, d), jnp.bfloat16),
                pltpu.VMEM((rows, d), jnp.bfloat16),
                pltpu.VMEM((N_EXPERTS, RT_T), jnp.float32),
                pltpu.SemaphoreType.DMA((2,)),
                pltpu.SemaphoreType.DMA(()),
            ],
        ),
        out_shape=jax.ShapeDtypeStruct(x4.shape, jnp.float32),
        compiler_params=pltpu.CompilerParams(
            dimension_semantics=("arbitrary",),
            vmem_limit_bytes=_vmem_limit(40 * 1024 * 1024)),
        name="combine",
    )(lo_flat, x4, rel_t, m4, ye)


def _window_bias_t():
    j = jnp.arange(3 * WIN_SUB)[:, None]
    i = jnp.arange(WIN_SUB)[None, :]
    dist = jnp.abs(i + WIN_SUB - j).astype(jnp.float32)
    slopes = jnp.exp2(-8.0 * (jnp.arange(N_HEADS_A, dtype=jnp.float32) + 1.0) / N_HEADS_A)
    b = jnp.where(dist[None] <= WINDOW, -(slopes[:, None, None] * dist[None]), NEG)
    return jnp.transpose(b, (1, 0, 2)).reshape(3 * WIN_SUB, N_HEADS_A * WIN_SUB) * LOG2E


def _na_bias_t(rpb):
    kk = jnp.arange(NA_KEY_ROWS)[:, None]
    rho = jnp.arange(NA_GROUP_ROWS)[None, :]
    rel = kk - NA_GROUP_ROWS
    r0 = jnp.stack([
        jnp.zeros_like(rho),
        rho - NA_ROWS // 2,
        jnp.full_like(rho, NA_GROUP_ROWS - NA_ROWS),
    ])
    row_ok = (rel[None] >= r0) & (rel[None] < r0 + NA_ROWS)
    dr = jnp.clip(rel - rho + (NA_ROWS - 1), 0, 2 * NA_ROWS - 2)
    ck = jnp.arange(GRID_W)[:, None]
    cq = jnp.arange(GRID_W)[None, :]
    c0 = jnp.clip(cq - NA_COLS // 2, 0, GRID_W - NA_COLS)
    col_ok = (ck >= c0) & (ck < c0 + NA_COLS)
    dc = jnp.clip(ck - cq + (NA_COLS - 1), 0, 2 * NA_COLS - 2)
    hi = lax.Precision.HIGHEST
    oh_r = (dr[:, :, None] == jnp.arange(2 * NA_ROWS - 1)).astype(jnp.float32)
    oh_c = (dc[None] == jnp.arange(2 * NA_COLS - 1)[:, None, None]).astype(jnp.float32)
    rows = jnp.einsum('krs,hsd->hkrd', oh_r, rpb.astype(jnp.float32), precision=hi)
    vals = jnp.einsum('hkrd,dcq->hkcrq', rows, oh_c, precision=hi)
    ok = row_ok[:, :, None, :, None] & col_ok[None, None, :, None, :]
    b = jnp.where(ok[:, None], vals[None], NEG)
    h = rpb.shape[0]
    b = b.reshape(3, h // 2, 2, NA_KEY_ROWS * GRID_W, NA_TQ)
    b = jnp.transpose(b, (0, 1, 3, 2, 4)).reshape(3, h // 2, NA_KEY_ROWS * GRID_W, 2 * NA_TQ)
    return b * LOG2E


def _layer_params(p, l):
    scale = LOG2E / math.sqrt(HEAD_DIM)
    gains = jnp.concatenate([p["qnorm_a"][l] * scale, p["knorm_a"][l],
                             p["qnorm_b"][l] * scale, p["knorm_b"][l]])
    wr = p["w_router"][l].T
    wr_hi = wr.astype(jnp.bfloat16)
    return dict(
        g_mix=p["norm_mix"][l][None, :],
        w_in_t=p["w_in"][l].T.astype(jnp.bfloat16),
        head_gains=jnp.broadcast_to(gains[:, None], (4 * HEAD_DIM, PROJ_TM)),
        sink_row=jnp.repeat(p["sink_a"][l].astype(jnp.float32) * LOG2E, WIN_SUB)[None, :],
        na_bias=_na_bias_t(p["rpb_b"][l]),
        g_a=p["onorm_a"][l][None, :],
        g_b=p["onorm_b"][l][None, :],
        w_out=p["w_out"][l].astype(jnp.bfloat16),
        g_ffn=p["norm_ffn"][l][None, :],
        wr_hilo=jnp.concatenate(
            [wr_hi, (wr - wr_hi.astype(jnp.float32)).astype(jnp.bfloat16)], axis=0),
        layer=l,
        w_gate=p["w_gate"],
        w_up=p["w_up"],
        w_down=p["w_down"],
    )


def _trunk(x, layers, win_bias):
    b, s, d = x.shape
    n = b * s
    assert s % WIN_TQ == 0 and s // NA_TQ >= 3 and n % PROJ_TM == 0 and n % RT_T == 0
    assert (n // RT_T) % RT_TPS == 0 and (n // RT_T) % min(RANK_TILES, n // RT_T) == 0
    cap = EC_CAPACITY * n // N_EXPERTS
    tm = min(FFN_TM, cap)
    assert cap % tm == 0 and tm >= max(RT_CH, DISP_CH)
    cpad = cap + tm
    x2 = x.reshape(n, d)
    for q in layers:
        qa_t, ka, va_t, qb_t, kb, vb_t = _in_proj(x2, q["g_mix"], q["w_in_t"], q["head_gains"])
        out_a = _window_attention(qa_t, ka, va_t, win_bias, q["sink_row"], s)
        out_b = _na_attention(qb_t, kb, vb_t, q["na_bias"], s)
        x1, h, aff_t = _post_attn(out_a, out_b, x2, q["g_a"], q["g_b"], q["w_out"],
                                  q["g_ffn"], q["wr_hilo"])
        nt = n // RT_T
        m4 = _retile(_select(aff_t, cap)).reshape(RT_T // RT_G, nt, RT_G, LANE)
        rel_t, lo = _rank(m4)
        lo_flat = lo.reshape(-1)
        xe = _dispatch(_granule_view(h, nt), rel_t, lo_flat, cap, cpad)
        ye = _expert_ffn(xe, q["w_gate"], q["w_up"], q["w_down"], q["layer"], cap, cpad, tm)
        x2 = _combine(_granule_view(x1, nt), rel_t, m4, ye, lo_flat, cap).reshape(n, d)
    return x2.reshape(b, s, d)


def kernel(x_prompt, x_sample, norm_mix, w_in, qnorm_a, knorm_a, sink_a, qnorm_b, knorm_b,
           rpb_b, onorm_a, onorm_b, w_out, norm_ffn, w_router, w_gate, w_up, w_down):
    p = dict(norm_mix=norm_mix, w_in=w_in, qnorm_a=qnorm_a, knorm_a=knorm_a, sink_a=sink_a,
             qnorm_b=qnorm_b, knorm_b=knorm_b, rpb_b=rpb_b, onorm_a=onorm_a, onorm_b=onorm_b,
             w_out=w_out, norm_ffn=norm_ffn, w_router=w_router, w_gate=w_gate, w_up=w_up,
             w_down=w_down)
    layers = [_layer_params(p, l) for l in range(w_in.shape[0])]
    win_bias = _window_bias_t()
    return (_trunk(x_prompt, layers, win_bias), _trunk(x_sample, layers, win_bias))
```

```python
import functools
import math

import jax
import jax.numpy as jnp
from jax import lax
from jax.experimental import pallas as pl
from jax.experimental.pallas import tpu as pltpu

HEAD_DIM = 64
N_HEADS_A = 8
N_KV_HEADS_A = 2
N_HEADS_B = 8
QA_W = N_HEADS_A * HEAD_DIM
KVA_W = N_KV_HEADS_A * HEAD_DIM
QKVB_W = N_HEADS_B * HEAD_DIM
PROJ_W = QA_W + 2 * KVA_W + 3 * QKVB_W
WINDOW = 128
GRID_W = 64
NA_ROWS = 8
NA_COLS = 16
N_EXPERTS = 16
EC_CAPACITY = 2
EPS = 1e-6
NEG = -1e30
LOG2E = 1.4426950408889634

LANE = 128
V7X_VMEM_BYTES = 64 * 1024 * 1024

PROJ_TM = 1024
WIN_TQ = 2048
WIN_SUB = WINDOW
NA_GROUP_ROWS = 4
NA_TQ = NA_GROUP_ROWS * GRID_W
NA_KEY_ROWS = 3 * NA_GROUP_ROWS
NA_KC = 128
POST_TM = 1024
FFN_TM = 1024
FFN_TF = 512
RT_T = 256
RT_CH = 64
DISP_CH = 64
RT_G = 16
RT_TPS = 1
RETILE_ROWS = 2048
RANK_TILES = 8

_NT = (((1,), (1,)), ((), ()))


def _vmem_limit(nbytes):
    return int(min(nbytes, V7X_VMEM_BYTES - 4 * 1024 * 1024))


def _proj_kernel(x_ref, g_ref, w_ref, hg_ref,
                 qa_ref, ka_ref, va_ref, qb_ref, kb_ref, vb_ref):
    x = x_ref[...]
    ms = jnp.mean(x * x, axis=-1, keepdims=True)
    h = (x * lax.rsqrt(ms + EPS) * g_ref[...]).astype(jnp.bfloat16)

    def seg(lo, hi):
        return lax.dot_general(w_ref[lo:hi, :], h, _NT,
                               preferred_element_type=jnp.float32)

    def head_norm(blk, gain):
        ssq = jnp.sum(blk * blk, axis=0, keepdims=True)
        return blk * lax.rsqrt(ssq * (1.0 / HEAD_DIM) + EPS) * gain

    g_qa = hg_ref[0 * HEAD_DIM:1 * HEAD_DIM, :]
    g_ka = hg_ref[1 * HEAD_DIM:2 * HEAD_DIM, :]
    g_qb = hg_ref[2 * HEAD_DIM:3 * HEAD_DIM, :]
    g_kb = hg_ref[3 * HEAD_DIM:4 * HEAD_DIM, :]

    o = 0
    p = seg(o, o + QA_W)
    for hd in range(N_HEADS_A):
        r = slice(hd * HEAD_DIM, (hd + 1) * HEAD_DIM)
        qa_ref[r, :] = head_norm(p[r, :], g_qa).astype(qa_ref.dtype)
    o += QA_W
    p = seg(o, o + 2 * KVA_W)
    kn = jnp.concatenate(
        [head_norm(p[hd * HEAD_DIM:(hd + 1) * HEAD_DIM, :], g_ka)
         for hd in range(N_KV_HEADS_A)], axis=0)
    ka_ref[...] = kn.T.astype(ka_ref.dtype)
    va_ref[...] = p[KVA_W:2 * KVA_W, :].astype(va_ref.dtype)
    o += 2 * KVA_W
    p = seg(o, o + QKVB_W)
    for hd in range(N_HEADS_B):
        r = slice(hd * HEAD_DIM, (hd + 1) * HEAD_DIM)
        qb_ref[r, :] = head_norm(p[r, :], g_qb).astype(qb_ref.dtype)
    o += QKVB_W
    p = seg(o, o + QKVB_W)
    kn = jnp.concatenate(
        [head_norm(p[hd * HEAD_DIM:(hd + 1) * HEAD_DIM, :], g_kb)
         for hd in range(N_HEADS_B)], axis=0)
    kb_ref[...] = kn.T.astype(kb_ref.dtype)
    o += QKVB_W
    vb_ref[...] = seg(o, o + QKVB_W).astype(vb_ref.dtype)


def _in_proj(x2d, g_mix, w_in_t, head_gains):
    n, d = x2d.shape
    tm = PROJ_TM
    bf = jnp.bfloat16
    col = lambda i: (0, i)
    row = lambda i: (i, 0)
    const = lambda i: (0, 0)
    out_shape = (
        jax.ShapeDtypeStruct((QA_W, n), bf),
        jax.ShapeDtypeStruct((n, KVA_W), bf),
        jax.ShapeDtypeStruct((KVA_W, n), bf),
        jax.ShapeDtypeStruct((QKVB_W, n), bf),
        jax.ShapeDtypeStruct((n, QKVB_W), bf),
        jax.ShapeDtypeStruct((QKVB_W, n), bf),
    )
    out_specs = (
        pl.BlockSpec((QA_W, tm), col),
        pl.BlockSpec((tm, KVA_W), row),
        pl.BlockSpec((KVA_W, tm), col),
        pl.BlockSpec((QKVB_W, tm), col),
        pl.BlockSpec((tm, QKVB_W), row),
        pl.BlockSpec((QKVB_W, tm), col),
    )
    return pl.pallas_call(
        _proj_kernel,
        grid=(n // tm,),
        in_specs=[
            pl.BlockSpec((tm, d), row),
            pl.BlockSpec((1, d), const),
            pl.BlockSpec((PROJ_W, d), const),
            pl.BlockSpec((4 * HEAD_DIM, tm), const),
        ],
        out_specs=out_specs,
        out_shape=out_shape,
        compiler_params=pltpu.CompilerParams(
            dimension_semantics=("arbitrary",),
            vmem_limit_bytes=_vmem_limit(48 * 1024 * 1024)),
        name="in_proj",
    )(x2d, g_mix, w_in_t, head_gains)


def _fold(acc, v, op):
    return v if acc is None else op(acc, v)


def _window_kernel(blocks_per_seq, q_ref, kp_ref, kc_ref, kn_ref,
                   vp_ref, vc_ref, vn_ref, bias_ref, sink_ref, o_ref, s_scr, p_scr):
    i = pl.program_id(0)
    pos = i % blocks_per_seq
    pen_prev = jnp.where(pos == 0, NEG, 0.0).astype(jnp.float32)
    pen_next = jnp.where(pos == blocks_per_seq - 1, NEG, 0.0).astype(jnp.float32)

    sink = sink_ref[...]
    n_sub = WIN_TQ // WIN_SUB
    n_chunks = 3
    gq = N_HEADS_A // N_KV_HEADS_A
    zero = jnp.zeros((HEAD_DIM, WIN_SUB), jnp.bfloat16)
    krefs = (kp_ref, kc_ref, kn_ref)

    def key_block(kb):
        if kb == 0:
            return 0, 0
        if kb == n_sub + 1:
            return 2, 0
        return 1, (kb - 1) * WIN_SUB

    def qblock(j):
        cols = slice(j * WIN_SUB, (j + 1) * WIN_SUB)
        halves = []
        for kv in range(N_KV_HEADS_A):
            parts = []
            for hd in range(N_HEADS_A):
                if hd // gq == kv:
                    parts.append(q_ref[hd * HEAD_DIM:(hd + 1) * HEAD_DIM, cols])
                else:
                    parts.append(zero)
            halves.append(jnp.concatenate(parts, axis=1))
        return jnp.concatenate(halves, axis=0)

    def score_chunk(j, c, qblk):
        rows = slice(c * WIN_SUB, (c + 1) * WIN_SUB)
        r, off = key_block(j + c)
        s = jnp.dot(krefs[r][off:off + WIN_SUB, :], qblk,
                    preferred_element_type=jnp.float32)
        s = s + bias_ref[rows, :]
        if j + c == 0:
            s = s + pen_prev
        if j + c == n_sub + 1:
            s = s + pen_next
        s_scr[j % 2, rows, :] = s
        return jnp.max(s, axis=0, keepdims=True)

    def prob_chunk(j, c, m):
        rows = slice(c * WIN_SUB, (c + 1) * WIN_SUB)
        p_scr[j % 2, rows, :] = jnp.exp2(s_scr[j % 2, rows, :] - m).astype(jnp.bfloat16)

    def finish(j, m):
        cols = slice(j * WIN_SUB, (j + 1) * WIN_SUB)
        vparts = []
        for c in range(n_chunks):
            r, off = key_block(j + c)
            vparts.append((vp_ref, vc_ref, vn_ref)[r][:, off:off + WIN_SUB])
        vwin = jnp.concatenate(vparts, axis=1)
        ones = jnp.ones((16, n_chunks * WIN_SUB), jnp.bfloat16)
        sink_term = jnp.exp2(sink - m)
        outs = []
        for kv in range(N_KV_HEADS_A):
            lanes = slice(kv * gq * WIN_SUB, (kv + 1) * gq * WIN_SUB)
            vt = jnp.concatenate([vwin[kv * HEAD_DIM:(kv + 1) * HEAD_DIM, :], ones], axis=0)
            o_t = jnp.dot(vt, p_scr[j % 2, :, lanes],
                          preferred_element_type=jnp.float32)
            o_t = o_t[:HEAD_DIM] / (o_t[HEAD_DIM:HEAD_DIM + 1] + sink_term[:, lanes])
            for g in range(gq):
                outs.append(o_t[:, g * WIN_SUB:(g + 1) * WIN_SUB])
        for a in range(N_HEADS_A // 2):
            pair = jnp.concatenate([outs[2 * a], outs[2 * a + 1]], axis=0)
            o_ref[cols, a * LANE:(a + 1) * LANE] = pair.T.astype(o_ref.dtype)

    qb = qblock(0)
    m = None
    for c in range(n_chunks):
        m = _fold(m, score_chunk(0, c, qb), jnp.maximum)
    m = jnp.maximum(m, sink)
    m_done = None
    for j in range(n_sub):
        m_next = None
        if j + 1 < n_sub:
            qb = qblock(j + 1)
        for c in range(n_chunks):
            prob_chunk(j, c, m)
            if j + 1 < n_sub:
                m_next = _fold(m_next, score_chunk(j + 1, c, qb), jnp.maximum)
            if c == 0 and j >= 1:
                finish(j - 1, m_done)
        m_done = m
        if j + 1 < n_sub:
            m = jnp.maximum(m_next, sink)
    finish(n_sub - 1, m_done)


def _window_attention(qa_t, ka, va_t, bias_t, sink_row, seq_len):
    n = ka.shape[0]
    nblk = n // WIN_TQ
    bps = seq_len // WIN_TQ
    r = WIN_TQ // WIN_SUB
    nsub = n // WIN_SUB
    prev_i = lambda i: jnp.maximum(r * i - 1, 0)
    next_i = lambda i: jnp.minimum(r * i + r, nsub - 1)
    const = lambda i: (0, 0)
    return pl.pallas_call(
        functools.partial(_window_kernel, bps),
        grid=(nblk,),
        in_specs=[
            pl.BlockSpec((QA_W, WIN_TQ), lambda i: (0, i)),
            pl.BlockSpec((WIN_SUB, KVA_W), lambda i: (prev_i(i), 0)),
            pl.BlockSpec((WIN_TQ, KVA_W), lambda i: (i, 0)),
            pl.BlockSpec((WIN_SUB, KVA_W), lambda i: (next_i(i), 0)),
            pl.BlockSpec((KVA_W, WIN_SUB), lambda i: (0, prev_i(i))),
            pl.BlockSpec((KVA_W, WIN_TQ), lambda i: (0, i)),
            pl.BlockSpec((KVA_W, WIN_SUB), lambda i: (0, next_i(i))),
            pl.BlockSpec((3 * WIN_SUB, N_HEADS_A * WIN_SUB), const),
            pl.BlockSpec((1, N_HEADS_A * WIN_SUB), const),
        ],
        out_specs=pl.BlockSpec((WIN_TQ, QA_W), lambda i: (i, 0)),
        out_shape=jax.ShapeDtypeStruct((n, QA_W), jnp.bfloat16),
        scratch_shapes=[
            pltpu.VMEM((2, 3 * WIN_SUB, N_HEADS_A * WIN_SUB), jnp.float32),
            pltpu.VMEM((2, 3 * WIN_SUB, N_HEADS_A * WIN_SUB), jnp.bfloat16),
        ],
        compiler_params=pltpu.CompilerParams(
            dimension_semantics=("arbitrary",),
            vmem_limit_bytes=_vmem_limit(40 * 1024 * 1024)),
        name="window_attn",
    )(qa_t, ka, ka, ka, va_t, va_t, va_t, bias_t, sink_row)


def _na_kernel(q_ref, kp_ref, kc_ref, kn_ref, vp_ref, vc_ref, vn_ref,
               bias_ref, o_ref, s_scr, p_scr):
    zero = jnp.zeros((HEAD_DIM, NA_TQ), jnp.bfloat16)
    n_keys = NA_KEY_ROWS * GRID_W
    n_chunks = n_keys // NA_KC
    n_pairs = N_HEADS_B // 2
    krefs = (kp_ref, kc_ref, kn_ref)

    def qblock(pr):
        q0 = q_ref[(2 * pr) * HEAD_DIM:(2 * pr + 1) * HEAD_DIM, :]
        q1 = q_ref[(2 * pr + 1) * HEAD_DIM:(2 * pr + 2) * HEAD_DIM, :]
        return jnp.concatenate(
            [jnp.concatenate([q0, zero], axis=1),
             jnp.concatenate([zero, q1], axis=1)], axis=0)

    def score_chunk(pr, c, qblk):
        rows = slice(c * NA_KC, (c + 1) * NA_KC)
        blk, off = divmod(c * NA_KC, NA_TQ)
        kchunk = krefs[blk][off:off + NA_KC, pr * LANE:(pr + 1) * LANE]
        s = jnp.dot(kchunk, qblk, preferred_element_type=jnp.float32)
        s = s + bias_ref[0, pr, rows, :]
        s_scr[pr % 2, rows, :] = s
        return jnp.max(s, axis=0, keepdims=True)

    def prob_chunk(pr, c, m):
        rows = slice(c * NA_KC, (c + 1) * NA_KC)
        p_scr[pr % 2, rows, :] = jnp.exp2(s_scr[pr % 2, rows, :] - m).astype(jnp.bfloat16)

    def finish(pr):
        lanes = slice(pr * LANE, (pr + 1) * LANE)
        vwin = jnp.concatenate(
            [vp_ref[lanes, :], vc_ref[lanes, :], vn_ref[lanes, :]], axis=1)
        ones = jnp.ones((16, n_keys), jnp.bfloat16)
        outs = []
        for t in range(2):
            cols = slice(t * NA_TQ, (t + 1) * NA_TQ)
            vt = jnp.concatenate([vwin[t * HEAD_DIM:(t + 1) * HEAD_DIM, :], ones], axis=0)
            o_t = jnp.dot(vt, p_scr[pr % 2, :, cols],
                          preferred_element_type=jnp.float32)
            outs.append(o_t[:HEAD_DIM] / o_t[HEAD_DIM:HEAD_DIM + 1])
        pair = jnp.concatenate(outs, axis=0)
        o_ref[:, lanes] = pair.T.astype(o_ref.dtype)

    qb = qblock(0)
    m = None
    for c in range(n_chunks):
        m = _fold(m, score_chunk(0, c, qb), jnp.maximum)
    for pr in range(n_pairs):
        m_next = None
        if pr + 1 < n_pairs:
            qb = qblock(pr + 1)
        for c in range(n_chunks):
            prob_chunk(pr, c, m)
            if pr + 1 < n_pairs:
                m_next = _fold(m_next, score_chunk(pr + 1, c, qb), jnp.maximum)
            if c == 0 and pr >= 1:
                finish(pr - 1)
        m = m_next
    finish(n_pairs - 1)


def _na_attention(qb_t, kb, vb_t, bias, seq_len):
    n = kb.shape[0]
    ng = n // NA_TQ
    gps = seq_len // NA_TQ
    prev_i = lambda g: jnp.maximum(g - 1, 0)
    next_i = lambda g: jnp.minimum(g + 1, ng - 1)

    def variant(g):
        pos = g % gps
        return jnp.where(pos == 0, 0, jnp.where(pos == gps - 1, 2, 1))

    return pl.pallas_call(
        _na_kernel,
        grid=(ng,),
        in_specs=[
            pl.BlockSpec((QKVB_W, NA_TQ), lambda g: (0, g)),
            pl.BlockSpec((NA_TQ, QKVB_W), lambda g: (prev_i(g), 0)),
            pl.BlockSpec((NA_TQ, QKVB_W), lambda g: (g, 0)),
            pl.BlockSpec((NA_TQ, QKVB_W), lambda g: (next_i(g), 0)),
            pl.BlockSpec((QKVB_W, NA_TQ), lambda g: (0, prev_i(g))),
            pl.BlockSpec((QKVB_W, NA_TQ), lambda g: (0, g)),
            pl.BlockSpec((QKVB_W, NA_TQ), lambda g: (0, next_i(g))),
            pl.BlockSpec((1, N_HEADS_B // 2, NA_KEY_ROWS * GRID_W, 2 * NA_TQ),
                         lambda g: (variant(g), 0, 0, 0)),
        ],
        out_specs=pl.BlockSpec((NA_TQ, QKVB_W), lambda g: (g, 0)),
        out_shape=jax.ShapeDtypeStruct((n, QKVB_W), jnp.bfloat16),
        scratch_shapes=[
            pltpu.VMEM((2, NA_KEY_ROWS * GRID_W, 2 * NA_TQ), jnp.float32),
            pltpu.VMEM((2, NA_KEY_ROWS * GRID_W, 2 * NA_TQ), jnp.bfloat16),
        ],
        compiler_params=pltpu.CompilerParams(
            dimension_semantics=("arbitrary",),
            vmem_limit_bytes=_vmem_limit(48 * 1024 * 1024)),
        name="na_attn",
    )(qb_t, kb, kb, kb, vb_t, vb_t, vb_t, bias)


def _post_kernel(a_ref, b_ref, x_ref, ga_ref, gb_ref, w_ref, gf_ref,
                 wr_ref, x1_ref, h_ref, aff_ref):
    def rms(v, g):
        ms = jnp.mean(v * v, axis=-1, keepdims=True)
        return v * lax.rsqrt(ms + EPS) * g

    an = rms(a_ref[...].astype(jnp.float32), ga_ref[...]).astype(jnp.bfloat16)
    bn = rms(b_ref[...].astype(jnp.float32), gb_ref[...]).astype(jnp.bfloat16)
    y = jnp.dot(an, w_ref[:QA_W, :], preferred_element_type=jnp.float32)
    y = y + jnp.dot(bn, w_ref[QA_W:, :], preferred_element_type=jnp.float32)
    x1 = x_ref[...] + y
    x1_ref[...] = x1
    h = rms(x1, gf_ref[...])
    h_hi = h.astype(jnp.bfloat16)
    h_lo = (h - h_hi.astype(jnp.float32)).astype(jnp.bfloat16)
    h_ref[...] = h_hi
    both = lax.dot_general(wr_ref[...], h_hi, _NT, preferred_element_type=jnp.float32)
    logits = both[:N_EXPERTS] + both[N_EXPERTS:]
    logits = logits + lax.dot_general(wr_ref[:N_EXPERTS, :], h_lo, _NT,
                                      preferred_element_type=jnp.float32)
    m = jnp.max(logits, axis=0, keepdims=True)
    e = jnp.exp(logits - m)
    aff_ref[...] = e / jnp.sum(e, axis=0, keepdims=True)


def _post_attn(out_a, out_b, x2d, g_a, g_b, w_out, g_ffn, wr_hilo):
    n, d = x2d.shape
    tm = POST_TM
    row = lambda i: (i, 0)
    const = lambda i: (0, 0)
    return pl.pallas_call(
        _post_kernel,
        grid=(n // tm,),
        in_specs=[
            pl.BlockSpec((tm, QA_W), row),
            pl.BlockSpec((tm, QKVB_W), row),
            pl.BlockSpec((tm, d), row),
            pl.BlockSpec((1, QA_W), const),
            pl.BlockSpec((1, QKVB_W), const),
            pl.BlockSpec((QA_W + QKVB_W, d), const),
            pl.BlockSpec((1, d), const),
            pl.BlockSpec((2 * N_EXPERTS, d), const),
        ],
        out_specs=(
            pl.BlockSpec((tm, d), row),
            pl.BlockSpec((tm, d), row),
            pl.BlockSpec((N_EXPERTS, tm), lambda i: (0, i)),
        ),
        out_shape=(
            jax.ShapeDtypeStruct((n, d), jnp.float32),
            jax.ShapeDtypeStruct((n, d), jnp.bfloat16),
            jax.ShapeDtypeStruct((N_EXPERTS, n), jnp.float32),
        ),
        compiler_params=pltpu.CompilerParams(
            dimension_semantics=("arbitrary",),
            vmem_limit_bytes=_vmem_limit(40 * 1024 * 1024)),
        name="post_attn",
    )(out_a, out_b, x2d, g_a, g_b, w_out, g_ffn, wr_hilo)


def _strict_upper(n):
    r = lax.broadcasted_iota(jnp.int32, (n, n), 0)
    c = lax.broadcasted_iota(jnp.int32, (n, n), 1)
    return jnp.where(r < c, 1.0, 0.0).astype(jnp.bfloat16)


def _select_kernel(cap, aff_ref, sel_ref):
    n = aff_ref.shape[1]
    nt = n // RT_T
    cap_f = jnp.float32(cap)

    def count(mask):
        return jnp.sum(jnp.where(mask, 1.0, 0.0), axis=1, keepdims=True)

    def search(b, ans):
        cand = ans | jnp.left_shift(jnp.int32(1), 30 - b)
        bits = pltpu.bitcast(aff_ref[...], jnp.int32)
        return jnp.where(count(bits >= cand) >= cap_f, cand, ans)

    thr = lax.fori_loop(0, 31, search, jnp.zeros((N_EXPERTS, 1), jnp.int32))
    need = cap_f - count(pltpu.bitcast(aff_ref[...], jnp.int32) > thr)
    tri = _strict_upper(RT_T)

    def tile(c, run_eq):
        start = pl.multiple_of(c * RT_T, RT_T)
        bits = pltpu.bitcast(aff_ref[:, pl.ds(start, RT_T)], jnp.int32)
        eq = bits == thr
        eq_b = jnp.where(eq, 1.0, 0.0).astype(jnp.bfloat16)
        eq_rank = jnp.dot(eq_b, tri, preferred_element_type=jnp.float32)
        sel = (bits > thr) | (eq & (run_eq + eq_rank < need))
        sel_ref[:, pl.ds(start, RT_T)] = jnp.where(sel, aff_ref[:, pl.ds(start, RT_T)], -1.0)
        return run_eq + count(eq)

    lax.fori_loop(0, nt, tile, need * 0.0)


def _retile_kernel(m_ref, o_ref):
    pad = jnp.zeros((LANE - N_EXPERTS, RT_T), jnp.float32)
    for q in range(m_ref.shape[1] // RT_T):
        cols = slice(q * RT_T, (q + 1) * RT_T)
        o_ref[cols, :] = jnp.concatenate([m_ref[:, cols], pad], axis=0).T


def _tile_columns(m_ref, t):
    rows = m_ref[:, t].reshape(RT_T, LANE)
    return rows.T[:N_EXPERTS, :]


def _rank_kernel(nt, m_ref, rel_ref, lo_ref, run_ref):
    jb = pl.program_id(0)
    tb = m_ref.shape[1]
    w = lo_ref.shape[1]
    tri = _strict_upper(RT_T)
    lane = lax.broadcasted_iota(jnp.int32, (N_EXPERTS, w), 1)

    @pl.when(jb == 0)
    def _():
        lo_ref[...] = jnp.zeros_like(lo_ref)
        run_ref[...] = jnp.zeros_like(run_ref)

    for t in range(tb):
        sel = _tile_columns(m_ref, t) >= 0.0
        sel_f = jnp.where(sel, 1.0, 0.0)
        rank = jnp.dot(sel_f.astype(jnp.bfloat16), tri, preferred_element_type=jnp.float32)
        rel_ref[:, t * RT_T:(t + 1) * RT_T] = jnp.where(sel, rank, -1.0).astype(jnp.int32)
        run = run_ref[...]
        lo_ref[...] = jnp.where(lane == jb * tb + t, run.astype(jnp.int32), lo_ref[...])
        run_ref[...] = run + jnp.sum(sel_f, axis=1, keepdims=True)

    @pl.when(jb == pl.num_programs(0) - 1)
    def _():
        lo_ref[...] = jnp.where(lane >= nt, run_ref[...].astype(jnp.int32), lo_ref[...])


def _select(aff_t, cap):
    e, n = aff_t.shape
    full = lambda i: (0, 0)
    return pl.pallas_call(
        functools.partial(_select_kernel, cap),
        grid=(1,),
        in_specs=[pl.BlockSpec((e, n), full)],
        out_specs=pl.BlockSpec((e, n), full),
        out_shape=jax.ShapeDtypeStruct((e, n), jnp.float32),
        compiler_params=pltpu.CompilerParams(
            dimension_semantics=("arbitrary",),
            vmem_limit_bytes=_vmem_limit(40 * 1024 * 1024)),
        name="route_select",
    )(aff_t)


def _retile(m_t):
    e, n = m_t.shape
    rb = min(RETILE_ROWS, n)
    return pl.pallas_call(
        _retile_kernel,
        grid=(n // rb,),
        in_specs=[pl.BlockSpec((e, rb), lambda i: (0, i))],
        out_specs=pl.BlockSpec((rb, LANE), lambda i: (i, 0)),
        out_shape=jax.ShapeDtypeStruct((n, LANE), jnp.float32),
        compiler_params=pltpu.CompilerParams(dimension_semantics=("arbitrary",)),
        name="route_retile",
    )(m_t)


def _rank(m4):
    g, nt, rg, lanes = m4.shape
    tb = min(RANK_TILES, nt)
    w = nt + LANE
    return pl.pallas_call(
        functools.partial(_rank_kernel, nt),
        grid=(nt // tb,),
        in_specs=[pl.BlockSpec((g, tb, rg, lanes), lambda i: (0, i, 0, 0))],
        out_specs=(pl.BlockSpec((N_EXPERTS, tb * RT_T), lambda i: (0, i)),
                   pl.BlockSpec((N_EXPERTS, w), lambda i: (0, 0))),
        out_shape=(jax.ShapeDtypeStruct((N_EXPERTS, nt * RT_T), jnp.int32),
                   jax.ShapeDtypeStruct((N_EXPERTS, w), jnp.int32)),
        scratch_shapes=[pltpu.VMEM((N_EXPERTS, 1), jnp.float32)],
        compiler_params=pltpu.CompilerParams(dimension_semantics=("arbitrary",)),
        name="route_rank",
    )(m4)


def _granule_view(x, nt):
    n, d = x.shape
    return x.reshape(RT_T // RT_G, nt, RT_G, d)


def _pack_pairs(x):
    w = x.shape[1] // 2
    lo = pltpu.bitcast(x[:, :w], jnp.uint32)
    hi = pltpu.bitcast(x[:, w:], jnp.uint32)
    return lo | (hi >> 16)


def _unpack_pairs(p):
    lo = pltpu.bitcast(p & jnp.uint32(0xFFFF0000), jnp.float32).astype(jnp.bfloat16)
    hi = pltpu.bitcast(p << 16, jnp.float32).astype(jnp.bfloat16)
    return lo, hi


def _one_hot_rows(rel_ref, cols, shift):
    kio = lax.broadcasted_iota(jnp.int32, (DISP_CH, RT_T), 0)
    blocks = []
    for e in range(N_EXPERTS):
        hit = (rel_ref[e:e + 1, cols] - shift) == kio
        blocks.append(jnp.where(hit, 1.0, 0.0).astype(jnp.bfloat16))
    return jnp.concatenate(blocks, axis=0)


def _dispatch_kernel(cap, cpad, nt, w, lo_ref, h_ref, rel_ref, xe_hbm,
                     stage, stage_x, sem, sem_x):
    for t in range(RT_TPS):
        _dispatch_tile(cap, cpad, nt, w, pl.program_id(0) * RT_TPS + t, t,
                       lo_ref, h_ref, rel_ref, xe_hbm, stage, stage_x, sem, sem_x)


def _dispatch_tile(cap, cpad, nt, w, j, t, lo_ref, h_ref, rel_ref, xe_hbm,
                   stage, stage_x, sem, sem_x):
    slot = j % 2
    cols = slice(t * RT_T, (t + 1) * RT_T)

    def dst(e, jj, c):
        row = e * cpad + lo_ref[e * w + jj] + c * DISP_CH
        return xe_hbm.at[pl.ds(row, DISP_CH), 0]

    def chunk_copy(e, jj, sl):
        return pltpu.make_async_copy(
            stage.at[sl, pl.ds(e * DISP_CH, DISP_CH)], dst(e, jj, 0), sem.at[sl])

    @pl.when(j == 0)
    def _():
        pad = cpad - cap
        stage_x[...] = jnp.zeros_like(stage_x)
        fills = [pltpu.make_async_copy(
            stage_x.at[pl.ds(0, pad)], xe_hbm.at[pl.ds(e * cpad + cap, pad), 0], sem_x)
            for e in range(N_EXPERTS)]
        for f in fills:
            f.start()
        for f in fills:
            f.wait()

    h_tile = h_ref[:, t].reshape(RT_T, h_ref.shape[-1])
    x = jnp.dot(_one_hot_rows(rel_ref, cols, 0), h_tile, preferred_element_type=jnp.float32)
    stage[slot] = _pack_pairs(x)

    @pl.when(j > 0)
    def _():
        for e in range(N_EXPERTS):
            chunk_copy(e, j - 1, 1 - slot).wait()

    for e in range(N_EXPERTS):
        chunk_copy(e, j, slot).start()

    cnts = [lo_ref[e * w + j + 1] - lo_ref[e * w + j] for e in range(N_EXPERTS)]
    most = functools.reduce(jnp.maximum, cnts)
    n_pass = lax.div(most + (DISP_CH - 1), jnp.int32(DISP_CH))

    def extra(c, carry):
        kio = lax.broadcasted_iota(jnp.int32, (DISP_CH, RT_T), 0)

        def extra_copy(e):
            return pltpu.make_async_copy(
                stage_x.at[pl.ds(e * DISP_CH, DISP_CH)], dst(e, j, c), sem_x)

        for e in range(N_EXPERTS):
            @pl.when(cnts[e] > c * DISP_CH)
            def _():
                hit = (rel_ref[e:e + 1, cols] - c * DISP_CH) == kio
                xx = jnp.dot(jnp.where(hit, 1.0, 0.0).astype(jnp.bfloat16),
                             h_ref[:, t].reshape(RT_T, h_ref.shape[-1]),
                             preferred_element_type=jnp.float32)
                stage_x[e * DISP_CH:(e + 1) * DISP_CH, :] = _pack_pairs(xx)
                extra_copy(e).start()
        for e in range(N_EXPERTS):
            @pl.when(cnts[e] > c * DISP_CH)
            def _():
                extra_copy(e).wait()
        return carry

    lax.fori_loop(1, n_pass, extra, 0)

    @pl.when(j == nt - 1)
    def _():
        for e in range(N_EXPERTS):
            chunk_copy(e, j, slot).wait()


def _dispatch(h4, rel_t, lo_flat, cap, cpad):
    _, nt, _, d = h4.shape
    w = lo_flat.shape[0] // N_EXPERTS
    rows = N_EXPERTS * DISP_CH
    assert cpad - cap <= rows
    return pl.pallas_call(
        functools.partial(_dispatch_kernel, cap, cpad, nt, w),
        grid_spec=pltpu.PrefetchScalarGridSpec(
            num_scalar_prefetch=1,
            grid=(nt // RT_TPS,),
            in_specs=[
                pl.BlockSpec((RT_T // RT_G, RT_TPS, RT_G, d), lambda j, lo: (0, j, 0, 0)),
                pl.BlockSpec((N_EXPERTS, RT_TPS * RT_T), lambda j, lo: (0, j)),
            ],
            out_specs=pl.BlockSpec(memory_space=pl.ANY),
            scratch_shapes=[
                pltpu.VMEM((2, rows, d // 2), jnp.uint32),
                pltpu.VMEM((rows, d // 2), jnp.uint32),
                pltpu.SemaphoreType.DMA((2,)),
                pltpu.SemaphoreType.DMA(()),
            ],
        ),
        out_shape=jax.ShapeDtypeStruct((N_EXPERTS * cpad, 1, d // 2), jnp.uint32),
        compiler_params=pltpu.CompilerParams(
            dimension_semantics=("arbitrary",),
            vmem_limit_bytes=_vmem_limit(40 * 1024 * 1024)),
        name="dispatch",
    )(lo_flat, h4, rel_t)


def _ffn_kernel(layer, cpad, tiles, tm, x_hbm, wg_hbm, wu_hbm, wd_hbm, o_ref,
                xbuf, wg_b, wu_b, wd_b, stg_g, stg_u, stg_d, xsem, wsem):
    i = pl.program_id(0)
    j = pl.program_id(1)
    n_exp = pl.num_programs(0)
    step = i * tiles + j
    slot = step % 2
    prev_slot = (step + 1) % 2
    rg = wg_b.shape[1] // tiles
    rd = wd_b.shape[1] // tiles

    def x_copy(ii, jj, sl):
        return pltpu.make_async_copy(
            x_hbm.at[pl.ds(ii * cpad + jj * tm, tm), 0], xbuf.at[sl], xsem.at[sl])

    def slab_copies(e, k, sl):
        r_g = pl.multiple_of(k * rg, rg)
        r_d = pl.multiple_of(k * rd, rd)
        return (
            pltpu.make_async_copy(wg_hbm.at[layer, e, pl.ds(r_g, rg)], stg_g.at[sl], wsem.at[sl]),
            pltpu.make_async_copy(wu_hbm.at[layer, e, pl.ds(r_g, rg)], stg_u.at[sl], wsem.at[sl]),
            pltpu.make_async_copy(wd_hbm.at[layer, e, pl.ds(r_d, rd)], stg_d.at[sl], wsem.at[sl]),
        )

    def cast_slab(wslot, k, sl):
        r_g = pl.multiple_of(k * rg, rg)
        r_d = pl.multiple_of(k * rd, rd)
        wg_b[wslot, pl.ds(r_g, rg), :] = stg_g[sl].astype(jnp.bfloat16)
        wu_b[wslot, pl.ds(r_g, rg), :] = stg_u[sl].astype(jnp.bfloat16)
        wd_b[wslot, pl.ds(r_d, rd), :] = stg_d[sl].astype(jnp.bfloat16)

    @pl.when(step == 0)
    def _():
        x_copy(0, 0, 0).start()

        def load(k, carry):
            sl = (k + tiles) % 2
            for cp in slab_copies(0, k, sl):
                cp.start()
            for cp in slab_copies(0, k, sl):
                cp.wait()
            cast_slab(0, k, sl)
            return carry

        lax.fori_loop(0, tiles, load, 0)

    j_prev = jnp.where(j > 0, j - 1, tiles - 1)
    e_prev = jnp.where(j > 0, i + 1, i)
    pending = jnp.logical_and(step > 0, e_prev < n_exp)

    @pl.when(pending)
    def _():
        for cp in slab_copies(e_prev, j_prev, prev_slot):
            cp.wait()

    @pl.when(i + 1 < n_exp)
    def _():
        for cp in slab_copies(i + 1, j, slot):
            cp.start()

    cast_slab(e_prev % 2, j_prev, prev_slot)

    @pl.when(step + 1 < n_exp * tiles)
    def _():
        wrap = j + 1 == tiles
        x_copy(jnp.where(wrap, i + 1, i), jnp.where(wrap, 0, j + 1), 1 - slot).start()

    x_copy(i, j, slot).wait()
    x_lo, x_hi = _unpack_pairs(xbuf[slot])
    half = x_lo.shape[1]
    d_ff = wg_b.shape[2]
    wg, wu, wd = wg_b.at[i % 2], wu_b.at[i % 2], wd_b.at[i % 2]
    acc = None
    for c in range(d_ff // FFN_TF):
        f = slice(c * FFN_TF, (c + 1) * FFN_TF)
        g = jnp.dot(x_lo, wg[:half, f], preferred_element_type=jnp.float32)
        g = g + jnp.dot(x_hi, wg[half:, f], preferred_element_type=jnp.float32)
        u = jnp.dot(x_lo, wu[:half, f], preferred_element_type=jnp.float32)
        u = u + jnp.dot(x_hi, wu[half:, f], preferred_element_type=jnp.float32)
        act = (g * jax.nn.sigmoid(g) * u).astype(jnp.bfloat16)
        part = jnp.dot(act, wd[f, :], preferred_element_type=jnp.float32)
        acc = part if acc is None else acc + part
    o_ref[...] = acc.astype(o_ref.dtype)


def _expert_ffn(xe, w_gate, w_up, w_down, layer, cap, cpad, tm):
    _, e, d, d_ff = w_gate.shape
    tiles = cap // tm
    assert d % tiles == 0 and (d // tiles) % 16 == 0
    any_spec = pl.BlockSpec(memory_space=pl.ANY)
    return pl.pallas_call(
        functools.partial(_ffn_kernel, layer, cpad, tiles, tm),
        grid=(e, tiles),
        in_specs=[any_spec, any_spec, any_spec, any_spec],
        out_specs=pl.BlockSpec((tm, d), lambda i, j: (i * tiles + j, 0)),
        out_shape=jax.ShapeDtypeStruct((e * cap, d), jnp.bfloat16),
        scratch_shapes=[
            pltpu.VMEM((2, tm, d // 2), jnp.uint32),
            pltpu.VMEM((2, d, d_ff), jnp.bfloat16),
            pltpu.VMEM((2, d, d_ff), jnp.bfloat16),
            pltpu.VMEM((2, d_ff, d), jnp.bfloat16),
            pltpu.VMEM((2, d // tiles, d_ff), jnp.float32),
            pltpu.VMEM((2, d // tiles, d_ff), jnp.float32),
            pltpu.VMEM((2, d_ff // tiles, d), jnp.float32),
            pltpu.SemaphoreType.DMA((2,)),
            pltpu.SemaphoreType.DMA((2,)),
        ],
        compiler_params=pltpu.CompilerParams(
            dimension_semantics=("arbitrary", "arbitrary"),
            vmem_limit_bytes=_vmem_limit(58 * 1024 * 1024)),
        name="expert_ffn",
    )(xe, w_gate, w_up, w_down)


def _combine_kernel(cap, nt, w, lo_ref, x_ref, rel_ref, m_ref, ye_hbm, o_ref,
                    ybuf, ybuf_x, gate_ref, sem, sem_x):
    for t in range(RT_TPS):
        _combine_tile(cap, nt, w, pl.program_id(0) * RT_TPS + t, t, lo_ref, x_ref, rel_ref,
                      m_ref, ye_hbm, o_ref, ybuf, ybuf_x, gate_ref, sem, sem_x)


def _combine_tile(cap, nt, w, j, t, lo_ref, x_ref, rel_ref, m_ref, ye_hbm, o_ref,
                  ybuf, ybuf_x, gate_ref, sem, sem_x):
    slot = j % 2
    cols = slice(t * RT_T, (t + 1) * RT_T)
    tile_shape = (o_ref.shape[0],) + o_ref.shape[2:]
    last_start = N_EXPERTS * cap - RT_CH
    align = 16

    def start_row(e, jj, c):
        lo = lo_ref[e * w + jj]
        a = e * cap + lo - (lo & (align - 1)) + c * RT_CH
        return pl.multiple_of(jnp.minimum(a, last_start), align)

    def fetch(e, jj, sl):
        return pltpu.make_async_copy(
            ye_hbm.at[pl.ds(start_row(e, jj, 0), RT_CH)],
            ybuf.at[sl, pl.ds(e * RT_CH, RT_CH)], sem.at[sl])

    @pl.when(j == 0)
    def _():
        for e in range(N_EXPERTS):
            fetch(e, 0, 0).start()

    @pl.when(j + 1 < nt)
    def _():
        for e in range(N_EXPERTS):
            fetch(e, j + 1, 1 - slot).start()

    gate_ref[...] = _tile_columns(m_ref, t)
    los = [lo_ref[e * w + j] for e in range(N_EXPERTS)]
    cnts = [lo_ref[e * w + j + 1] - los[e] for e in range(N_EXPERTS)]
    lead = [los[e] & (align - 1) for e in range(N_EXPERTS)]

    def weight_block(e, c):
        kio = lax.broadcasted_iota(jnp.int32, (RT_CH, RT_T), 0)
        r = rel_ref[e:e + 1, cols]
        p = r + lead[e]
        member = (r >= 0) & (p >= c * RT_CH) & (p < (c + 1) * RT_CH)
        off = e * cap + los[e] - start_row(e, j, c)
        hit = member & ((r + off) == kio)
        return jnp.where(hit, gate_ref[e:e + 1, :], 0.0).astype(jnp.bfloat16)

    tn = (((0,), (0,)), ((), ()))
    wt0 = jnp.concatenate([weight_block(e, 0) for e in range(N_EXPERTS)], axis=0)
    for e in range(N_EXPERTS):
        fetch(e, j, slot).wait()
    o_ref[:, t] = x_ref[:, t] + lax.dot_general(
        wt0, ybuf[slot], tn, preferred_element_type=jnp.float32).reshape(tile_shape)

    spans = [lead[e] + cnts[e] for e in range(N_EXPERTS)]
    most = functools.reduce(jnp.maximum, spans)
    n_pass = jnp.right_shift(most + (RT_CH - 1), RT_CH.bit_length() - 1)

    def extra(c, carry):
        def extra_fetch(e):
            return pltpu.make_async_copy(
                ye_hbm.at[pl.ds(start_row(e, j, c), RT_CH)],
                ybuf_x.at[pl.ds(e * RT_CH, RT_CH)], sem_x)

        for e in range(N_EXPERTS):
            @pl.when(spans[e] > c * RT_CH)
            def _():
                extra_fetch(e).start()
        for e in range(N_EXPERTS):
            @pl.when(spans[e] > c * RT_CH)
            def _():
                extra_fetch(e).wait()
        for e in range(N_EXPERTS):
            @pl.when(spans[e] > c * RT_CH)
            def _():
                wte = weight_block(e, c)
                o_ref[:, t] += lax.dot_general(
                    wte, ybuf_x[e * RT_CH:(e + 1) * RT_CH, :], tn,
                    preferred_element_type=jnp.float32).reshape(tile_shape)
        return carry

    lax.fori_loop(1, n_pass, extra, 0)


def _combine(x4, rel_t, m4, ye, lo_flat, cap):
    _, nt, _, d = x4.shape
    w = lo_flat.shape[0] // N_EXPERTS
    rows = N_EXPERTS * RT_CH
    return pl.pallas_call(
        functools.partial(_combine_kernel, cap, nt, w),
        grid_spec=pltpu.PrefetchScalarGridSpec(
            num_scalar_prefetch=1,
            grid=(nt // RT_TPS,),
            in_specs=[
                pl.BlockSpec((RT_T // RT_G, RT_TPS, RT_G, d), lambda j, lo: (0, j, 0, 0)),
                pl.BlockSpec((N_EXPERTS, RT_TPS * RT_T), lambda j, lo: (0, j)),
                pl.BlockSpec((RT_T // RT_G, RT_TPS, RT_G, LANE), lambda j, lo: (0, j, 0, 0)),
                pl.BlockSpec(memory_space=pl.ANY),
            ],
            out_specs=pl.BlockSpec((RT_T // RT_G, RT_TPS, RT_G, d), lambda j, lo: (0, j, 0, 0)),
            scratch_shapes=[
                pltpu.VMEM((2, rows, d), jnp.bfloat16),
                pltpu.VMEM((rows, d), jnp.bfloat16),
                pltpu.VMEM((N_EXPERTS, RT_T), jnp.float32),
                pltpu.SemaphoreType.DMA((2,)),
                pltpu.SemaphoreType.DMA(()),
            ],
        ),
        out_shape=jax.ShapeDtypeStruct(x4.shape, jnp.float32),
        compiler_params=pltpu.CompilerParams(
            dimension_semantics=("arbitrary",),
            vmem_limit_bytes=_vmem_limit(40 * 1024 * 1024)),
        name="combine",
    )(lo_flat, x4, rel_t, m4, ye)


def _window_bias_t():
    j = jnp.arange(3 * WIN_SUB)[:, None]
    i = jnp.arange(WIN_SUB)[None, :]
    dist = jnp.abs(i + WIN_SUB - j).astype(jnp.float32)
    slopes = jnp.exp2(-8.0 * (jnp.arange(N_HEADS_A, dtype=jnp.float32) + 1.0) / N_HEADS_A)
    b = jnp.where(dist[None] <= WINDOW, -(slopes[:, None, None] * dist[None]), NEG)
    return jnp.transpose(b, (1, 0, 2)).reshape(3 * WIN_SUB, N_HEADS_A * WIN_SUB) * LOG2E


def _na_bias_t(rpb):
    kk = jnp.arange(NA_KEY_ROWS)[:, None]
    rho = jnp.arange(NA_GROUP_ROWS)[None, :]
    rel = kk - NA_GROUP_ROWS
    r0 = jnp.stack([
        jnp.zeros_like(rho),
        rho - NA_ROWS // 2,
        jnp.full_like(rho, NA_GROUP_ROWS - NA_ROWS),
    ])
    row_ok = (rel[None] >= r0) & (rel[None] < r0 + NA_ROWS)
    dr = jnp.clip(rel - rho + (NA_ROWS - 1), 0, 2 * NA_ROWS - 2)
    ck = jnp.arange(GRID_W)[:, None]
    cq = jnp.arange(GRID_W)[None, :]
    c0 = jnp.clip(cq - NA_COLS // 2, 0, GRID_W - NA_COLS)
    col_ok = (ck >= c0) & (ck < c0 + NA_COLS)
    dc = jnp.clip(ck - cq + (NA_COLS - 1), 0, 2 * NA_COLS - 2)
    hi = lax.Precision.HIGHEST
    oh_r = (dr[:, :, None] == jnp.arange(2 * NA_ROWS - 1)).astype(jnp.float32)
    oh_c = (dc[None] == jnp.arange(2 * NA_COLS - 1)[:, None, None]).astype(jnp.float32)
    rows = jnp.einsum('krs,hsd->hkrd', oh_r, rpb.astype(jnp.float32), precision=hi)
    vals = jnp.einsum('hkrd,dcq->hkcrq', rows, oh_c, precision=hi)
    ok = row_ok[:, :, None, :, None] & col_ok[None, None, :, None, :]
    b = jnp.where(ok[:, None], vals[None], NEG)
    h = rpb.shape[0]
    b = b.reshape(3, h // 2, 2, NA_KEY_ROWS * GRID_W, NA_TQ)
    b = jnp.transpose(b, (0, 1, 3, 2, 4)).reshape(3, h // 2, NA_KEY_ROWS * GRID_W, 2 * NA_TQ)
    return b * LOG2E


def _layer_params(p, l):
    scale = LOG2E / math.sqrt(HEAD_DIM)
    gains = jnp.concatenate([p["qnorm_a"][l] * scale, p["knorm_a"][l],
                             p["qnorm_b"][l] * scale, p["knorm_b"][l]])
    wr = p["w_router"][l].T
    wr_hi = wr.astype(jnp.bfloat16)
    return dict(
        g_mix=p["norm_mix"][l][None, :],
        w_in_t=p["w_in"][l].T.astype(jnp.bfloat16),
        head_gains=jnp.broadcast_to(gains[:, None], (4 * HEAD_DIM, PROJ_TM)),
        sink_row=jnp.repeat(p["sink_a"][l].astype(jnp.float32) * LOG2E, WIN_SUB)[None, :],
        na_bias=_na_bias_t(p["rpb_b"][l]),
        g_a=p["onorm_a"][l][None, :],
        g_b=p["onorm_b"][l][None, :],
        w_out=p["w_out"][l].astype(jnp.bfloat16),
        g_ffn=p["norm_ffn"][l][None, :],
        wr_hilo=jnp.concatenate(
            [wr_hi, (wr - wr_hi.astype(jnp.float32)).astype(jnp.bfloat16)], axis=0),
        layer=l,
        w_gate=p["w_gate"],
        w_up=p["w_up"],
        w_down=p["w_down"],
    )


def _trunk(x, layers, win_bias):
    b, s, d = x.shape
    n = b * s
    assert s % WIN_TQ == 0 and s // NA_TQ >= 3 and n % PROJ_TM == 0 and n % RT_T == 0
    assert (n // RT_T) % RT_TPS == 0 and (n // RT_T) % min(RANK_TILES, n // RT_T) == 0
    cap = EC_CAPACITY * n // N_EXPERTS
    tm = min(FFN_TM, cap)
    assert cap % tm == 0 and tm >= max(RT_CH, DISP_CH)
    cpad = cap + tm
    x2 = x.reshape(n, d)
    for q in layers:
        qa_t, ka, va_t, qb_t, kb, vb_t = _in_proj(x2, q["g_mix"], q["w_in_t"], q["head_gains"])
        out_a = _window_attention(qa_t, ka, va_t, win_bias, q["sink_row"], s)
        out_b = _na_attention(qb_t, kb, vb_t, q["na_bias"], s)
        x1, h, aff_t = _post_attn(out_a, out_b, x2, q["g_a"], q["g_b"], q["w_out"],
                                  q["g_ffn"], q["wr_hilo"])
        nt = n // RT_T
        m4 = _retile(_select(aff_t, cap)).reshape(RT_T // RT_G, nt, RT_G, LANE)
        rel_t, lo = _rank(m4)
        lo_flat = lo.reshape(-1)
        xe = _dispatch(_granule_view(h, nt), rel_t, lo_flat, cap, cpad)
        ye = _expert_ffn(xe, q["w_gate"], q["w_up"], q["w_down"], q["layer"], cap, cpad, tm)
        x2 = _combine(_granule_view(x1, nt), rel_t, m4, ye, lo_flat, cap).reshape(n, d)
    return x2.reshape(b, s, d)


def kernel(x_prompt, x_sample, norm_mix, w_in, qnorm_a, knorm_a, sink_a, qnorm_b, knorm_b,
           rpb_b, onorm_a, onorm_b, w_out, norm_ffn, w_router, w_gate, w_up, w_down):
    p = dict(norm_mix=norm_mix, w_in=w_in, qnorm_a=qnorm_a, knorm_a=knorm_a, sink_a=sink_a,
             qnorm_b=qnorm_b, knorm_b=knorm_b, rpb_b=rpb_b, onorm_a=onorm_a, onorm_b=onorm_b,
             w_out=w_out, norm_ffn=norm_ffn, w_router=w_router, w_gate=w_gate, w_up=w_up,
             w_down=w_down)
    layers = [_layer_params(p, l) for l in range(w_in.shape[0])]
    win_bias = _window_bias_t()
    return (_trunk(x_prompt, layers, win_bias), _trunk(x_sample, layers, win_bias))
```

```python
import functools
import math

import jax
import jax.numpy as jnp
from jax import lax
from jax.experimental import pallas as pl
from jax.experimental.pallas import tpu as pltpu

HEAD_DIM = 64
N_HEADS_A = 8
N_KV_HEADS_A = 2
N_HEADS_B = 8
QA_W = N_HEADS_A * HEAD_DIM
KVA_W = N_KV_HEADS_A * HEAD_DIM
QKVB_W = N_HEADS_B * HEAD_DIM
PROJ_W = QA_W + 2 * KVA_W + 3 * QKVB_W
WINDOW = 128
GRID_W = 64
NA_ROWS = 8
NA_COLS = 16
N_EXPERTS = 16
EC_CAPACITY = 2
EPS = 1e-6
NEG = -1e30
LOG2E = 1.4426950408889634

LANE = 128
V7X_VMEM_BYTES = 64 * 1024 * 1024

PROJ_TM = 1024
WIN_TQ = 2048
WIN_SUB = WINDOW
NA_GROUP_ROWS = 4
NA_TQ = NA_GROUP_ROWS * GRID_W
NA_KEY_ROWS = 3 * NA_GROUP_ROWS
NA_KC = 128
POST_TM = 1024
FFN_TM = 1024
FFN_TF = 512
RT_T = 256
RT_CH = 64
DISP_CH = 64
RT_G = 16
RT_TPS = 1
COMBINE_RING = 3
RETILE_ROWS = 2048
RANK_TILES = 8

_NT = (((1,), (1,)), ((), ()))


def _vmem_limit(nbytes):
    return int(min(nbytes, V7X_VMEM_BYTES - 4 * 1024 * 1024))


def _proj_kernel(x_ref, g_ref, w_ref, hg_ref,
                 qa_ref, ka_ref, va_ref, qb_ref, kb_ref, vb_ref):
    x = x_ref[...]
    ms = jnp.mean(x * x, axis=-1, keepdims=True)
    h = (x * lax.rsqrt(ms + EPS) * g_ref[...]).astype(jnp.bfloat16)

    def seg(lo, hi):
        return lax.dot_general(w_ref[lo:hi, :], h, _NT,
                               preferred_element_type=jnp.float32)

    def head_norm(blk, gain):
        ssq = jnp.sum(blk * blk, axis=0, keepdims=True)
        return blk * lax.rsqrt(ssq * (1.0 / HEAD_DIM) + EPS) * gain

    g_qa = hg_ref[0 * HEAD_DIM:1 * HEAD_DIM, :]
    g_ka = hg_ref[1 * HEAD_DIM:2 * HEAD_DIM, :]
    g_qb = hg_ref[2 * HEAD_DIM:3 * HEAD_DIM, :]
    g_kb = hg_ref[3 * HEAD_DIM:4 * HEAD_DIM, :]

    o = 0
    p = seg(o, o + QA_W)
    for hd in range(N_HEADS_A):
        r = slice(hd * HEAD_DIM, (hd + 1) * HEAD_DIM)
        qa_ref[r, :] = head_norm(p[r, :], g_qa).astype(qa_ref.dtype)
    o += QA_W
    p = seg(o, o + 2 * KVA_W)
    kn = jnp.concatenate(
        [head_norm(p[hd * HEAD_DIM:(hd + 1) * HEAD_DIM, :], g_ka)
         for hd in range(N_KV_HEADS_A)], axis=0)
    ka_ref[...] = kn.T.astype(ka_ref.dtype)
    va_ref[...] = p[KVA_W:2 * KVA_W, :].astype(va_ref.dtype)
    o += 2 * KVA_W
    p = seg(o, o + QKVB_W)
    for hd in range(N_HEADS_B):
        r = slice(hd * HEAD_DIM, (hd + 1) * HEAD_DIM)
        qb_ref[r, :] = head_norm(p[r, :], g_qb).astype(qb_ref.dtype)
    o += QKVB_W
    p = seg(o, o + QKVB_W)
    kn = jnp.concatenate(
        [head_norm(p[hd * HEAD_DIM:(hd + 1) * HEAD_DIM, :], g_kb)
         for hd in range(N_HEADS_B)], axis=0)
    kb_ref[...] = kn.T.astype(kb_ref.dtype)
    o += QKVB_W
    vb_ref[...] = seg(o, o + QKVB_W).astype(vb_ref.dtype)


def _in_proj(x2d, g_mix, w_in_t, head_gains):
    n, d = x2d.shape
    tm = PROJ_TM
    bf = jnp.bfloat16
    col = lambda i: (0, i)
    row = lambda i: (i, 0)
    const = lambda i: (0, 0)
    out_shape = (
        jax.ShapeDtypeStruct((QA_W, n), bf),
        jax.ShapeDtypeStruct((n, KVA_W), bf),
        jax.ShapeDtypeStruct((KVA_W, n), bf),
        jax.ShapeDtypeStruct((QKVB_W, n), bf),
        jax.ShapeDtypeStruct((n, QKVB_W), bf),
        jax.ShapeDtypeStruct((QKVB_W, n), bf),
    )
    out_specs = (
        pl.BlockSpec((QA_W, tm), col),
        pl.BlockSpec((tm, KVA_W), row),
        pl.BlockSpec((KVA_W, tm), col),
        pl.BlockSpec((QKVB_W, tm), col),
        pl.BlockSpec((tm, QKVB_W), row),
        pl.BlockSpec((QKVB_W, tm), col),
    )
    return pl.pallas_call(
        _proj_kernel,
        grid=(n // tm,),
        in_specs=[
            pl.BlockSpec((tm, d), row),
            pl.BlockSpec((1, d), const),
            pl.BlockSpec((PROJ_W, d), const),
            pl.BlockSpec((4 * HEAD_DIM, tm), const),
        ],
        out_specs=out_specs,
        out_shape=out_shape,
        compiler_params=pltpu.CompilerParams(
            dimension_semantics=("arbitrary",),
            vmem_limit_bytes=_vmem_limit(48 * 1024 * 1024)),
        name="in_proj",
    )(x2d, g_mix, w_in_t, head_gains)


def _fold(acc, v, op):
    return v if acc is None else op(acc, v)


def _window_kernel(blocks_per_seq, q_ref, kp_ref, kc_ref, kn_ref,
                   vp_ref, vc_ref, vn_ref, bias_ref, sink_ref, o_ref, s_scr, p_scr):
    i = pl.program_id(0)
    pos = i % blocks_per_seq
    pen_prev = jnp.where(pos == 0, NEG, 0.0).astype(jnp.float32)
    pen_next = jnp.where(pos == blocks_per_seq - 1, NEG, 0.0).astype(jnp.float32)

    sink = sink_ref[...]
    n_sub = WIN_TQ // WIN_SUB
    n_chunks = 3
    gq = N_HEADS_A // N_KV_HEADS_A
    zero = jnp.zeros((HEAD_DIM, WIN_SUB), jnp.bfloat16)
    krefs = (kp_ref, kc_ref, kn_ref)

    def key_block(kb):
        if kb == 0:
            return 0, 0
        if kb == n_sub + 1:
            return 2, 0
        return 1, (kb - 1) * WIN_SUB

    def qblock(j):
        cols = slice(j * WIN_SUB, (j + 1) * WIN_SUB)
        halves = []
        for kv in range(N_KV_HEADS_A):
            parts = []
            for hd in range(N_HEADS_A):
                if hd // gq == kv:
                    parts.append(q_ref[hd * HEAD_DIM:(hd + 1) * HEAD_DIM, cols])
                else:
                    parts.append(zero)
            halves.append(jnp.concatenate(parts, axis=1))
        return jnp.concatenate(halves, axis=0)

    def score_chunk(j, c, qblk):
        rows = slice(c * WIN_SUB, (c + 1) * WIN_SUB)
        r, off = key_block(j + c)
        s = jnp.dot(krefs[r][off:off + WIN_SUB, :], qblk,
                    preferred_element_type=jnp.float32)
        s = s + bias_ref[rows, :]
        if j + c == 0:
            s = s + pen_prev
        if j + c == n_sub + 1:
            s = s + pen_next
        s_scr[j % 2, rows, :] = s
        return jnp.max(s, axis=0, keepdims=True)

    def prob_chunk(j, c, m):
        rows = slice(c * WIN_SUB, (c + 1) * WIN_SUB)
        p_scr[j % 2, rows, :] = jnp.exp2(s_scr[j % 2, rows, :] - m).astype(jnp.bfloat16)

    def finish(j, m):
        cols = slice(j * WIN_SUB, (j + 1) * WIN_SUB)
        vparts = []
        for c in range(n_chunks):
            r, off = key_block(j + c)
            vparts.append((vp_ref, vc_ref, vn_ref)[r][:, off:off + WIN_SUB])
        vwin = jnp.concatenate(vparts, axis=1)
        ones = jnp.ones((16, n_chunks * WIN_SUB), jnp.bfloat16)
        sink_term = jnp.exp2(sink - m)
        outs = []
        for kv in range(N_KV_HEADS_A):
            lanes = slice(kv * gq * WIN_SUB, (kv + 1) * gq * WIN_SUB)
            vt = jnp.concatenate([vwin[kv * HEAD_DIM:(kv + 1) * HEAD_DIM, :], ones], axis=0)
            o_t = jnp.dot(vt, p_scr[j % 2, :, lanes],
                          preferred_element_type=jnp.float32)
            o_t = o_t[:HEAD_DIM] / (o_t[HEAD_DIM:HEAD_DIM + 1] + sink_term[:, lanes])
            for g in range(gq):
                outs.append(o_t[:, g * WIN_SUB:(g + 1) * WIN_SUB])
        for a in range(N_HEADS_A // 2):
            pair = jnp.concatenate([outs[2 * a], outs[2 * a + 1]], axis=0)
            o_ref[cols, a * LANE:(a + 1) * LANE] = pair.T.astype(o_ref.dtype)

    qb = qblock(0)
    m = None
    for c in range(n_chunks):
        m = _fold(m, score_chunk(0, c, qb), jnp.maximum)
    m = jnp.maximum(m, sink)
    m_done = None
    for j in range(n_sub):
        m_next = None
        if j + 1 < n_sub:
            qb = qblock(j + 1)
        for c in range(n_chunks):
            prob_chunk(j, c, m)
            if j + 1 < n_sub:
                m_next = _fold(m_next, score_chunk(j + 1, c, qb), jnp.maximum)
            if c == 0 and j >= 1:
                finish(j - 1, m_done)
        m_done = m
        if j + 1 < n_sub:
            m = jnp.maximum(m_next, sink)
    finish(n_sub - 1, m_done)


def _window_attention(qa_t, ka, va_t, bias_t, sink_row, seq_len):
    n = ka.shape[0]
    nblk = n // WIN_TQ
    bps = seq_len // WIN_TQ
    r = WIN_TQ // WIN_SUB
    nsub = n // WIN_SUB
    prev_i = lambda i: jnp.maximum(r * i - 1, 0)
    next_i = lambda i: jnp.minimum(r * i + r, nsub - 1)
    const = lambda i: (0, 0)
    return pl.pallas_call(
        functools.partial(_window_kernel, bps),
        grid=(nblk,),
        in_specs=[
            pl.BlockSpec((QA_W, WIN_TQ), lambda i: (0, i)),
            pl.BlockSpec((WIN_SUB, KVA_W), lambda i: (prev_i(i), 0)),
            pl.BlockSpec((WIN_TQ, KVA_W), lambda i: (i, 0)),
            pl.BlockSpec((WIN_SUB, KVA_W), lambda i: (next_i(i), 0)),
            pl.BlockSpec((KVA_W, WIN_SUB), lambda i: (0, prev_i(i))),
            pl.BlockSpec((KVA_W, WIN_TQ), lambda i: (0, i)),
            pl.BlockSpec((KVA_W, WIN_SUB), lambda i: (0, next_i(i))),
            pl.BlockSpec((3 * WIN_SUB, N_HEADS_A * WIN_SUB), const),
            pl.BlockSpec((1, N_HEADS_A * WIN_SUB), const),
        ],
        out_specs=pl.BlockSpec((WIN_TQ, QA_W), lambda i: (i, 0)),
        out_shape=jax.ShapeDtypeStruct((n, QA_W), jnp.bfloat16),
        scratch_shapes=[
            pltpu.VMEM((2, 3 * WIN_SUB, N_HEADS_A * WIN_SUB), jnp.float32),
            pltpu.VMEM((2, 3 * WIN_SUB, N_HEADS_A * WIN_SUB), jnp.bfloat16),
        ],
        compiler_params=pltpu.CompilerParams(
            dimension_semantics=("arbitrary",),
            vmem_limit_bytes=_vmem_limit(40 * 1024 * 1024)),
        name="window_attn",
    )(qa_t, ka, ka, ka, va_t, va_t, va_t, bias_t, sink_row)


def _na_kernel(q_ref, kp_ref, kc_ref, kn_ref, vp_ref, vc_ref, vn_ref,
               bias_ref, o_ref, s_scr, p_scr):
    zero = jnp.zeros((HEAD_DIM, NA_TQ), jnp.bfloat16)
    n_keys = NA_KEY_ROWS * GRID_W
    n_chunks = n_keys // NA_KC
    n_pairs = N_HEADS_B // 2
    krefs = (kp_ref, kc_ref, kn_ref)

    def qblock(pr):
        q0 = q_ref[(2 * pr) * HEAD_DIM:(2 * pr + 1) * HEAD_DIM, :]
        q1 = q_ref[(2 * pr + 1) * HEAD_DIM:(2 * pr + 2) * HEAD_DIM, :]
        return jnp.concatenate(
            [jnp.concatenate([q0, zero], axis=1),
             jnp.concatenate([zero, q1], axis=1)], axis=0)

    def score_chunk(pr, c, qblk):
        rows = slice(c * NA_KC, (c + 1) * NA_KC)
        blk, off = divmod(c * NA_KC, NA_TQ)
        kchunk = krefs[blk][off:off + NA_KC, pr * LANE:(pr + 1) * LANE]
        s = jnp.dot(kchunk, qblk, preferred_element_type=jnp.float32)
        s = s + bias_ref[0, pr, rows, :]
        s_scr[pr % 2, rows, :] = s
        return jnp.max(s, axis=0, keepdims=True)

    def prob_chunk(pr, c, m):
        rows = slice(c * NA_KC, (c + 1) * NA_KC)
        p_scr[pr % 2, rows, :] = jnp.exp2(s_scr[pr % 2, rows, :] - m).astype(jnp.bfloat16)

    def finish(pr):
        lanes = slice(pr * LANE, (pr + 1) * LANE)
        vwin = jnp.concatenate(
            [vp_ref[lanes, :], vc_ref[lanes, :], vn_ref[lanes, :]], axis=1)
        ones = jnp.ones((16, n_keys), jnp.bfloat16)
        outs = []
        for t in range(2):
            cols = slice(t * NA_TQ, (t + 1) * NA_TQ)
            vt = jnp.concatenate([vwin[t * HEAD_DIM:(t + 1) * HEAD_DIM, :], ones], axis=0)
            o_t = jnp.dot(vt, p_scr[pr % 2, :, cols],
                          preferred_element_type=jnp.float32)
            outs.append(o_t[:HEAD_DIM] / o_t[HEAD_DIM:HEAD_DIM + 1])
        pair = jnp.concatenate(outs, axis=0)
        o_ref[:, lanes] = pair.T.astype(o_ref.dtype)

    qb = qblock(0)
    m = None
    for c in range(n_chunks):
        m = _fold(m, score_chunk(0, c, qb), jnp.maximum)
    for pr in range(n_pairs):
        m_next = None
        if pr + 1 < n_pairs:
            qb = qblock(pr + 1)
        for c in range(n_chunks):
            prob_chunk(pr, c, m)
            if pr + 1 < n_pairs:
                m_next = _fold(m_next, score_chunk(pr + 1, c, qb), jnp.maximum)
            if c == 0 and pr >= 1:
                finish(pr - 1)
        m = m_next
    finish(n_pairs - 1)


def _na_attention(qb_t, kb, vb_t, bias, seq_len):
    n = kb.shape[0]
    ng = n // NA_TQ
    gps = seq_len // NA_TQ
    prev_i = lambda g: jnp.maximum(g - 1, 0)
    next_i = lambda g: jnp.minimum(g + 1, ng - 1)

    def variant(g):
        pos = g % gps
        return jnp.where(pos == 0, 0, jnp.where(pos == gps - 1, 2, 1))

    return pl.pallas_call(
        _na_kernel,
        grid=(ng,),
        in_specs=[
            pl.BlockSpec((QKVB_W, NA_TQ), lambda g: (0, g)),
            pl.BlockSpec((NA_TQ, QKVB_W), lambda g: (prev_i(g), 0)),
            pl.BlockSpec((NA_TQ, QKVB_W), lambda g: (g, 0)),
            pl.BlockSpec((NA_TQ, QKVB_W), lambda g: (next_i(g), 0)),
            pl.BlockSpec((QKVB_W, NA_TQ), lambda g: (0, prev_i(g))),
            pl.BlockSpec((QKVB_W, NA_TQ), lambda g: (0, g)),
            pl.BlockSpec((QKVB_W, NA_TQ), lambda g: (0, next_i(g))),
            pl.BlockSpec((1, N_HEADS_B // 2, NA_KEY_ROWS * GRID_W, 2 * NA_TQ),
                         lambda g: (variant(g), 0, 0, 0)),
        ],
        out_specs=pl.BlockSpec((NA_TQ, QKVB_W), lambda g: (g, 0)),
        out_shape=jax.ShapeDtypeStruct((n, QKVB_W), jnp.bfloat16),
        scratch_shapes=[
            pltpu.VMEM((2, NA_KEY_ROWS * GRID_W, 2 * NA_TQ), jnp.float32),
            pltpu.VMEM((2, NA_KEY_ROWS * GRID_W, 2 * NA_TQ), jnp.bfloat16),
        ],
        compiler_params=pltpu.CompilerParams(
            dimension_semantics=("arbitrary",),
            vmem_limit_bytes=_vmem_limit(48 * 1024 * 1024)),
        name="na_attn",
    )(qb_t, kb, kb, kb, vb_t, vb_t, vb_t, bias)


def _post_kernel(a_ref, b_ref, x_ref, ga_ref, gb_ref, w_ref, gf_ref,
                 wr_ref, x1_ref, h_ref, aff_ref):
    def rms(v, g):
        ms = jnp.mean(v * v, axis=-1, keepdims=True)
        return v * lax.rsqrt(ms + EPS) * g

    an = rms(a_ref[...].astype(jnp.float32), ga_ref[...]).astype(jnp.bfloat16)
    bn = rms(b_ref[...].astype(jnp.float32), gb_ref[...]).astype(jnp.bfloat16)
    y = jnp.dot(an, w_ref[:QA_W, :], preferred_element_type=jnp.float32)
    y = y + jnp.dot(bn, w_ref[QA_W:, :], preferred_element_type=jnp.float32)
    x1 = x_ref[...] + y
    x1_ref[...] = x1
    h = rms(x1, gf_ref[...])
    h_hi = h.astype(jnp.bfloat16)
    h_lo = (h - h_hi.astype(jnp.float32)).astype(jnp.bfloat16)
    h_ref[...] = h_hi
    both = lax.dot_general(wr_ref[...], h_hi, _NT, preferred_element_type=jnp.float32)
    logits = both[:N_EXPERTS] + both[N_EXPERTS:]
    logits = logits + lax.dot_general(wr_ref[:N_EXPERTS, :], h_lo, _NT,
                                      preferred_element_type=jnp.float32)
    m = jnp.max(logits, axis=0, keepdims=True)
    e = jnp.exp(logits - m)
    aff_ref[...] = e / jnp.sum(e, axis=0, keepdims=True)


def _post_attn(out_a, out_b, x2d, g_a, g_b, w_out, g_ffn, wr_hilo):
    n, d = x2d.shape
    tm = POST_TM
    row = lambda i: (i, 0)
    const = lambda i: (0, 0)
    return pl.pallas_call(
        _post_kernel,
        grid=(n // tm,),
        in_specs=[
            pl.BlockSpec((tm, QA_W), row),
            pl.BlockSpec((tm, QKVB_W), row),
            pl.BlockSpec((tm, d), row),
            pl.BlockSpec((1, QA_W), const),
            pl.BlockSpec((1, QKVB_W), const),
            pl.BlockSpec((QA_W + QKVB_W, d), const),
            pl.BlockSpec((1, d), const),
            pl.BlockSpec((2 * N_EXPERTS, d), const),
        ],
        out_specs=(
            pl.BlockSpec((tm, d), row),
            pl.BlockSpec((tm, d), row),
            pl.BlockSpec((N_EXPERTS, tm), lambda i: (0, i)),
        ),
        out_shape=(
            jax.ShapeDtypeStruct((n, d), jnp.float32),
            jax.ShapeDtypeStruct((n, d), jnp.bfloat16),
            jax.ShapeDtypeStruct((N_EXPERTS, n), jnp.float32),
        ),
        compiler_params=pltpu.CompilerParams(
            dimension_semantics=("arbitrary",),
            vmem_limit_bytes=_vmem_limit(40 * 1024 * 1024)),
        name="post_attn",
    )(out_a, out_b, x2d, g_a, g_b, w_out, g_ffn, wr_hilo)


def _strict_upper(n):
    r = lax.broadcasted_iota(jnp.int32, (n, n), 0)
    c = lax.broadcasted_iota(jnp.int32, (n, n), 1)
    return jnp.where(r < c, 1.0, 0.0).astype(jnp.bfloat16)


def _select_kernel(cap, aff_ref, sel_ref):
    n = aff_ref.shape[1]
    nt = n // RT_T
    cap_f = jnp.float32(cap)

    def count(mask):
        return jnp.sum(jnp.where(mask, 1.0, 0.0), axis=1, keepdims=True)

    def search(b, ans):
        cand = ans | jnp.left_shift(jnp.int32(1), 30 - b)
        bits = pltpu.bitcast(aff_ref[...], jnp.int32)
        return jnp.where(count(bits >= cand) >= cap_f, cand, ans)

    thr = lax.fori_loop(0, 31, search, jnp.zeros((N_EXPERTS, 1), jnp.int32))
    need = cap_f - count(pltpu.bitcast(aff_ref[...], jnp.int32) > thr)
    tri = _strict_upper(RT_T)

    def tile(c, run_eq):
        start = pl.multiple_of(c * RT_T, RT_T)
        bits = pltpu.bitcast(aff_ref[:, pl.ds(start, RT_T)], jnp.int32)
        eq = bits == thr
        eq_b = jnp.where(eq, 1.0, 0.0).astype(jnp.bfloat16)
        eq_rank = jnp.dot(eq_b, tri, preferred_element_type=jnp.float32)
        sel = (bits > thr) | (eq & (run_eq + eq_rank < need))
        sel_ref[:, pl.ds(start, RT_T)] = jnp.where(sel, aff_ref[:, pl.ds(start, RT_T)], -1.0)
        return run_eq + count(eq)

    lax.fori_loop(0, nt, tile, need * 0.0)


def _retile_kernel(m_ref, o_ref):
    pad = jnp.zeros((LANE - N_EXPERTS, RT_T), jnp.float32)
    for q in range(m_ref.shape[1] // RT_T):
        cols = slice(q * RT_T, (q + 1) * RT_T)
        o_ref[cols, :] = jnp.concatenate([m_ref[:, cols], pad], axis=0).T


def _tile_columns(m_ref, t):
    rows = m_ref[:, t].reshape(RT_T, LANE)
    return rows.T[:N_EXPERTS, :]


def _rank_kernel(nt, m_ref, rel_ref, lo_ref, run_ref):
    jb = pl.program_id(0)
    tb = m_ref.shape[1]
    w = lo_ref.shape[1]
    tri = _strict_upper(RT_T)
    lane = lax.broadcasted_iota(jnp.int32, (N_EXPERTS, w), 1)

    @pl.when(jb == 0)
    def _():
        lo_ref[...] = jnp.zeros_like(lo_ref)
        run_ref[...] = jnp.zeros_like(run_ref)

    for t in range(tb):
        sel = _tile_columns(m_ref, t) >= 0.0
        sel_f = jnp.where(sel, 1.0, 0.0)
        rank = jnp.dot(sel_f.astype(jnp.bfloat16), tri, preferred_element_type=jnp.float32)
        rel_ref[:, t * RT_T:(t + 1) * RT_T] = jnp.where(sel, rank, -1.0).astype(jnp.int32)
        run = run_ref[...]
        lo_ref[...] = jnp.where(lane == jb * tb + t, run.astype(jnp.int32), lo_ref[...])
        run_ref[...] = run + jnp.sum(sel_f, axis=1, keepdims=True)

    @pl.when(jb == pl.num_programs(0) - 1)
    def _():
        lo_ref[...] = jnp.where(lane >= nt, run_ref[...].astype(jnp.int32), lo_ref[...])


def _select(aff_t, cap):
    e, n = aff_t.shape
    full = lambda i: (0, 0)
    return pl.pallas_call(
        functools.partial(_select_kernel, cap),
        grid=(1,),
        in_specs=[pl.BlockSpec((e, n), full)],
        out_specs=pl.BlockSpec((e, n), full),
        out_shape=jax.ShapeDtypeStruct((e, n), jnp.float32),
        compiler_params=pltpu.CompilerParams(
            dimension_semantics=("arbitrary",),
            vmem_limit_bytes=_vmem_limit(40 * 1024 * 1024)),
        name="route_select",
    )(aff_t)


def _retile(m_t):
    e, n = m_t.shape
    rb = min(RETILE_ROWS, n)
    return pl.pallas_call(
        _retile_kernel,
        grid=(n // rb,),
        in_specs=[pl.BlockSpec((e, rb), lambda i: (0, i))],
        out_specs=pl.BlockSpec((rb, LANE), lambda i: (i, 0)),
        out_shape=jax.ShapeDtypeStruct((n, LANE), jnp.float32),
        compiler_params=pltpu.CompilerParams(dimension_semantics=("arbitrary",)),
        name="route_retile",
    )(m_t)


def _rank(m4):
    g, nt, rg, lanes = m4.shape
    tb = min(RANK_TILES, nt)
    w = nt + LANE
    return pl.pallas_call(
        functools.partial(_rank_kernel, nt),
        grid=(nt // tb,),
        in_specs=[pl.BlockSpec((g, tb, rg, lanes), lambda i: (0, i, 0, 0))],
        out_specs=(pl.BlockSpec((N_EXPERTS, tb * RT_T), lambda i: (0, i)),
                   pl.BlockSpec((N_EXPERTS, w), lambda i: (0, 0))),
        out_shape=(jax.ShapeDtypeStruct((N_EXPERTS, nt * RT_T), jnp.int32),
                   jax.ShapeDtypeStruct((N_EXPERTS, w), jnp.int32)),
        scratch_shapes=[pltpu.VMEM((N_EXPERTS, 1), jnp.float32)],
        compiler_params=pltpu.CompilerParams(dimension_semantics=("arbitrary",)),
        name="route_rank",
    )(m4)


def _granule_view(x, nt):
    n, d = x.shape
    return x.reshape(RT_T // RT_G, nt, RT_G, d)


def _pack_pairs(x):
    w = x.shape[1] // 2
    lo = pltpu.bitcast(x[:, :w], jnp.uint32)
    hi = pltpu.bitcast(x[:, w:], jnp.uint32)
    return lo | (hi >> 16)


def _unpack_pairs(p):
    lo = pltpu.bitcast(p & jnp.uint32(0xFFFF0000), jnp.float32).astype(jnp.bfloat16)
    hi = pltpu.bitcast(p << 16, jnp.float32).astype(jnp.bfloat16)
    return lo, hi


def _one_hot_rows(rel_ref, cols, shift):
    kio = lax.broadcasted_iota(jnp.int32, (DISP_CH, RT_T), 0)
    blocks = []
    for e in range(N_EXPERTS):
        hit = (rel_ref[e:e + 1, cols] - shift) == kio
        blocks.append(jnp.where(hit, 1.0, 0.0).astype(jnp.bfloat16))
    return jnp.concatenate(blocks, axis=0)


def _dispatch_kernel(cap, cpad, nt, w, lo_ref, h_ref, rel_ref, xe_hbm,
                     stage, stage_x, sem, sem_x):
    for t in range(RT_TPS):
        _dispatch_tile(cap, cpad, nt, w, pl.program_id(0) * RT_TPS + t, t,
                       lo_ref, h_ref, rel_ref, xe_hbm, stage, stage_x, sem, sem_x)


def _dispatch_tile(cap, cpad, nt, w, j, t, lo_ref, h_ref, rel_ref, xe_hbm,
                   stage, stage_x, sem, sem_x):
    slot = j % 2
    cols = slice(t * RT_T, (t + 1) * RT_T)

    def dst(e, jj, c):
        row = e * cpad + lo_ref[e * w + jj] + c * DISP_CH
        return xe_hbm.at[pl.ds(row, DISP_CH), 0]

    def chunk_copy(e, jj, sl):
        return pltpu.make_async_copy(
            stage.at[sl, pl.ds(e * DISP_CH, DISP_CH)], dst(e, jj, 0), sem.at[sl])

    @pl.when(j == 0)
    def _():
        pad = cpad - cap
        stage_x[...] = jnp.zeros_like(stage_x)
        fills = [pltpu.make_async_copy(
            stage_x.at[pl.ds(0, pad)], xe_hbm.at[pl.ds(e * cpad + cap, pad), 0], sem_x)
            for e in range(N_EXPERTS)]
        for f in fills:
            f.start()
        for f in fills:
            f.wait()

    h_tile = h_ref[:, t].reshape(RT_T, h_ref.shape[-1])
    x = jnp.dot(_one_hot_rows(rel_ref, cols, 0), h_tile, preferred_element_type=jnp.float32)
    stage[slot] = _pack_pairs(x)

    @pl.when(j > 0)
    def _():
        for e in range(N_EXPERTS):
            chunk_copy(e, j - 1, 1 - slot).wait()

    for e in range(N_EXPERTS):
        chunk_copy(e, j, slot).start()

    cnts = [lo_ref[e * w + j + 1] - lo_ref[e * w + j] for e in range(N_EXPERTS)]
    most = functools.reduce(jnp.maximum, cnts)
    n_pass = lax.div(most + (DISP_CH - 1), jnp.int32(DISP_CH))

    def extra(c, carry):
        kio = lax.broadcasted_iota(jnp.int32, (DISP_CH, RT_T), 0)

        def extra_copy(e):
            return pltpu.make_async_copy(
                stage_x.at[pl.ds(e * DISP_CH, DISP_CH)], dst(e, j, c), sem_x)

        for e in range(N_EXPERTS):
            @pl.when(cnts[e] > c * DISP_CH)
            def _():
                hit = (rel_ref[e:e + 1, cols] - c * DISP_CH) == kio
                xx = jnp.dot(jnp.where(hit, 1.0, 0.0).astype(jnp.bfloat16),
                             h_ref[:, t].reshape(RT_T, h_ref.shape[-1]),
                             preferred_element_type=jnp.float32)
                stage_x[e * DISP_CH:(e + 1) * DISP_CH, :] = _pack_pairs(xx)
                extra_copy(e).start()
        for e in range(N_EXPERTS):
            @pl.when(cnts[e] > c * DISP_CH)
            def _():
                extra_copy(e).wait()
        return carry

    lax.fori_loop(1, n_pass, extra, 0)

    @pl.when(j == nt - 1)
    def _():
        for e in range(N_EXPERTS):
            chunk_copy(e, j, slot).wait()


def _dispatch(h4, rel_t, lo_flat, cap, cpad):
    _, nt, _, d = h4.shape
    w = lo_flat.shape[0] // N_EXPERTS
    rows = N_EXPERTS * DISP_CH
    assert cpad - cap <= rows
    return pl.pallas_call(
        functools.partial(_dispatch_kernel, cap, cpad, nt, w),
        grid_spec=pltpu.PrefetchScalarGridSpec(
            num_scalar_prefetch=1,
            grid=(nt // RT_TPS,),
            in_specs=[
                pl.BlockSpec((RT_T // RT_G, RT_TPS, RT_G, d), lambda j, lo: (0, j, 0, 0)),
                pl.BlockSpec((N_EXPERTS, RT_TPS * RT_T), lambda j, lo: (0, j)),
            ],
            out_specs=pl.BlockSpec(memory_space=pl.ANY),
            scratch_shapes=[
                pltpu.VMEM((2, rows, d // 2), jnp.uint32),
                pltpu.VMEM((rows, d // 2), jnp.uint32),
                pltpu.SemaphoreType.DMA((2,)),
                pltpu.SemaphoreType.DMA(()),
            ],
        ),
        out_shape=jax.ShapeDtypeStruct((N_EXPERTS * cpad, 1, d // 2), jnp.uint32),
        compiler_params=pltpu.CompilerParams(
            dimension_semantics=("arbitrary",),
            vmem_limit_bytes=_vmem_limit(40 * 1024 * 1024)),
        name="dispatch",
    )(lo_flat, h4, rel_t)


def _ffn_kernel(layer, cpad, tiles, tm, x_hbm, wg_hbm, wu_hbm, wd_hbm, o_ref,
                xbuf, wg_b, wu_b, wd_b, stg_g, stg_u, stg_d, xsem, wsem):
    i = pl.program_id(0)
    j = pl.program_id(1)
    n_exp = pl.num_programs(0)
    step = i * tiles + j
    slot = step % 2
    prev_slot = (step + 1) % 2
    rg = wg_b.shape[1] // tiles
    rd = wd_b.shape[1] // tiles

    def x_copy(ii, jj, sl):
        return pltpu.make_async_copy(
            x_hbm.at[pl.ds(ii * cpad + jj * tm, tm), 0], xbuf.at[sl], xsem.at[sl])

    def slab_copies(e, k, sl):
        r_g = pl.multiple_of(k * rg, rg)
        r_d = pl.multiple_of(k * rd, rd)
        return (
            pltpu.make_async_copy(wg_hbm.at[layer, e, pl.ds(r_g, rg)], stg_g.at[sl], wsem.at[sl]),
            pltpu.make_async_copy(wu_hbm.at[layer, e, pl.ds(r_g, rg)], stg_u.at[sl], wsem.at[sl]),
            pltpu.make_async_copy(wd_hbm.at[layer, e, pl.ds(r_d, rd)], stg_d.at[sl], wsem.at[sl]),
        )

    def cast_slab(wslot, k, sl):
        r_g = pl.multiple_of(k * rg, rg)
        r_d = pl.multiple_of(k * rd, rd)
        wg_b[wslot, pl.ds(r_g, rg), :] = stg_g[sl].astype(jnp.bfloat16)
        wu_b[wslot, pl.ds(r_g, rg), :] = stg_u[sl].astype(jnp.bfloat16)
        wd_b[wslot, pl.ds(r_d, rd), :] = stg_d[sl].astype(jnp.bfloat16)

    @pl.when(step == 0)
    def _():
        x_copy(0, 0, 0).start()

        def load(k, carry):
            sl = (k + tiles) % 2
            for cp in slab_copies(0, k, sl):
                cp.start()
            for cp in slab_copies(0, k, sl):
                cp.wait()
            cast_slab(0, k, sl)
            return carry

        lax.fori_loop(0, tiles, load, 0)

    j_prev = jnp.where(j > 0, j - 1, tiles - 1)
    e_prev = jnp.where(j > 0, i + 1, i)
    pending = jnp.logical_and(step > 0, e_prev < n_exp)

    @pl.when(pending)
    def _():
        for cp in slab_copies(e_prev, j_prev, prev_slot):
            cp.wait()

    @pl.when(i + 1 < n_exp)
    def _():
        for cp in slab_copies(i + 1, j, slot):
            cp.start()

    cast_slab(e_prev % 2, j_prev, prev_slot)

    @pl.when(step + 1 < n_exp * tiles)
    def _():
        wrap = j + 1 == tiles
        x_copy(jnp.where(wrap, i + 1, i), jnp.where(wrap, 0, j + 1), 1 - slot).start()

    x_copy(i, j, slot).wait()
    x_lo, x_hi = _unpack_pairs(xbuf[slot])
    half = x_lo.shape[1]
    d_ff = wg_b.shape[2]
    wg, wu, wd = wg_b.at[i % 2], wu_b.at[i % 2], wd_b.at[i % 2]
    acc = None
    for c in range(d_ff // FFN_TF):
        f = slice(c * FFN_TF, (c + 1) * FFN_TF)
        g = jnp.dot(x_lo, wg[:half, f], preferred_element_type=jnp.float32)
        g = g + jnp.dot(x_hi, wg[half:, f], preferred_element_type=jnp.float32)
        u = jnp.dot(x_lo, wu[:half, f], preferred_element_type=jnp.float32)
        u = u + jnp.dot(x_hi, wu[half:, f], preferred_element_type=jnp.float32)
        act = (g * jax.nn.sigmoid(g) * u).astype(jnp.bfloat16)
        part = jnp.dot(act, wd[f, :], preferred_element_type=jnp.float32)
        acc = part if acc is None else acc + part
    o_ref[...] = acc.astype(o_ref.dtype)


def _expert_ffn(xe, w_gate, w_up, w_down, layer, cap, cpad, tm):
    _, e, d, d_ff = w_gate.shape
    tiles = cap // tm
    assert d % tiles == 0 and (d // tiles) % 16 == 0
    any_spec = pl.BlockSpec(memory_space=pl.ANY)
    return pl.pallas_call(
        functools.partial(_ffn_kernel, layer, cpad, tiles, tm),
        grid=(e, tiles),
        in_specs=[any_spec, any_spec, any_spec, any_spec],
        out_specs=pl.BlockSpec((tm, d), lambda i, j: (i * tiles + j, 0)),
        out_shape=jax.ShapeDtypeStruct((e * cap, d), jnp.bfloat16),
        scratch_shapes=[
            pltpu.VMEM((2, tm, d // 2), jnp.uint32),
            pltpu.VMEM((2, d, d_ff), jnp.bfloat16),
            pltpu.VMEM((2, d, d_ff), jnp.bfloat16),
            pltpu.VMEM((2, d_ff, d), jnp.bfloat16),
            pltpu.VMEM((2, d // tiles, d_ff), jnp.float32),
            pltpu.VMEM((2, d // tiles, d_ff), jnp.float32),
            pltpu.VMEM((2, d_ff // tiles, d), jnp.float32),
            pltpu.SemaphoreType.DMA((2,)),
            pltpu.SemaphoreType.DMA((2,)),
        ],
        compiler_params=pltpu.CompilerParams(
            dimension_semantics=("arbitrary", "arbitrary"),
            vmem_limit_bytes=_vmem_limit(58 * 1024 * 1024)),
        name="expert_ffn",
    )(xe, w_gate, w_up, w_down)


def _combine_kernel(cap, nt, w, lo_ref, x_hbm, rel_ref, m_ref, ye_hbm, o_ref,
                    ybuf, ybuf_x, gate_ref, xring, sem, sem_x, xsem):
    for t in range(RT_TPS):
        _combine_tile(cap, nt, w, pl.program_id(0) * RT_TPS + t, t, lo_ref, x_hbm, rel_ref,
                      m_ref, ye_hbm, o_ref, ybuf, ybuf_x, gate_ref, xring, sem, sem_x, xsem)


def _combine_tile(cap, nt, w, j, t, lo_ref, x_hbm, rel_ref, m_ref, ye_hbm, o_ref,
                  ybuf, ybuf_x, gate_ref, xring, sem, sem_x, xsem):
    slot = j % 2
    cols = slice(t * RT_T, (t + 1) * RT_T)
    tile_shape = (o_ref.shape[0],) + o_ref.shape[2:]
    last_start = N_EXPERTS * cap - RT_CH
    align = 16
    depth = xring.shape[0]

    def x_fetch(jj):
        sl = jj % depth
        return pltpu.make_async_copy(x_hbm.at[:, jj], xring.at[sl], xsem.at[sl])

    @pl.when(j == 0)
    def _():
        for k in range(min(depth - 1, nt)):
            x_fetch(k).start()

    @pl.when(j + depth - 1 < nt)
    def _():
        x_fetch(j + depth - 1).start()

    def start_row(e, jj, c):
        lo = lo_ref[e * w + jj]
        a = e * cap + lo - (lo & (align - 1)) + c * RT_CH
        return pl.multiple_of(jnp.minimum(a, last_start), align)

    def fetch(e, jj, sl):
        return pltpu.make_async_copy(
            ye_hbm.at[pl.ds(start_row(e, jj, 0), RT_CH)],
            ybuf.at[sl, pl.ds(e * RT_CH, RT_CH)], sem.at[sl])

    @pl.when(j == 0)
    def _():
        for e in range(N_EXPERTS):
            fetch(e, 0, 0).start()

    @pl.when(j + 1 < nt)
    def _():
        for e in range(N_EXPERTS):
            fetch(e, j + 1, 1 - slot).start()

    gate_ref[...] = _tile_columns(m_ref, t)
    los = [lo_ref[e * w + j] for e in range(N_EXPERTS)]
    cnts = [lo_ref[e * w + j + 1] - los[e] for e in range(N_EXPERTS)]
    lead = [los[e] & (align - 1) for e in range(N_EXPERTS)]

    def weight_block(e, c):
        kio = lax.broadcasted_iota(jnp.int32, (RT_CH, RT_T), 0)
        r = rel_ref[e:e + 1, cols]
        p = r + lead[e]
        member = (r >= 0) & (p >= c * RT_CH) & (p < (c + 1) * RT_CH)
        off = e * cap + los[e] - start_row(e, j, c)
        hit = member & ((r + off) == kio)
        return jnp.where(hit, gate_ref[e:e + 1, :], 0.0).astype(jnp.bfloat16)

    tn = (((0,), (0,)), ((), ()))
    wt0 = jnp.concatenate([weight_block(e, 0) for e in range(N_EXPERTS)], axis=0)
    for e in range(N_EXPERTS):
        fetch(e, j, slot).wait()
    x_fetch(j).wait()
    o_ref[:, t] = xring[j % depth] + lax.dot_general(
        wt0, ybuf[slot], tn, preferred_element_type=jnp.float32).reshape(tile_shape)

    spans = [lead[e] + cnts[e] for e in range(N_EXPERTS)]
    most = functools.reduce(jnp.maximum, spans)
    n_pass = jnp.right_shift(most + (RT_CH - 1), RT_CH.bit_length() - 1)

    def extra(c, carry):
        def extra_fetch(e):
            return pltpu.make_async_copy(
                ye_hbm.at[pl.ds(start_row(e, j, c), RT_CH)],
                ybuf_x.at[pl.ds(e * RT_CH, RT_CH)], sem_x)

        for e in range(N_EXPERTS):
            @pl.when(spans[e] > c * RT_CH)
            def _():
                extra_fetch(e).start()
        for e in range(N_EXPERTS):
            @pl.when(spans[e] > c * RT_CH)
            def _():
                extra_fetch(e).wait()
        for e in range(N_EXPERTS):
            @pl.when(spans[e] > c * RT_CH)
            def _():
                wte = weight_block(e, c)
                o_ref[:, t] += lax.dot_general(
                    wte, ybuf_x[e * RT_CH:(e + 1) * RT_CH, :], tn,
                    preferred_element_type=jnp.float32).reshape(tile_shape)
        return carry

    lax.fori_loop(1, n_pass, extra, 0)


def _combine(x4, rel_t, m4, ye, lo_flat, cap):
    _, nt, _, d = x4.shape
    w = lo_flat.shape[0] // N_EXPERTS
    rows = N_EXPERTS * RT_CH
    return pl.pallas_call(
        functools.partial(_combine_kernel, cap, nt, w),
        grid_spec=pltpu.PrefetchScalarGridSpec(
            num_scalar_prefetch=1,
            grid=(nt // RT_TPS,),
            in_specs=[
                pl.BlockSpec(memory_space=pl.ANY),
                pl.BlockSpec((N_EXPERTS, RT_TPS * RT_T), lambda j, lo: (0, j)),
                pl.BlockSpec((RT_T // RT_G, RT_TPS, RT_G, LANE), lambda j, lo: (0, j, 0, 0)),
                pl.BlockSpec(memory_space=pl.ANY),
            ],
            out_specs=pl.BlockSpec((RT_T // RT_G, RT_TPS, RT_G, d), lambda j, lo: (0, j, 0, 0)),
            scratch_shapes=[
                pltpu.VMEM((2, rows, d), jnp.bfloat16),
                pltpu.VMEM((rows, d), jnp.bfloat16),
                pltpu.VMEM((N_EXPERTS, RT_T), jnp.float32),
                pltpu.VMEM((COMBINE_RING, RT_T // RT_G, RT_G, d), jnp.float32),
                pltpu.SemaphoreType.DMA((2,)),
                pltpu.SemaphoreType.DMA(()),
                pltpu.SemaphoreType.DMA((COMBINE_RING,)),
            ],
        ),
        out_shape=jax.ShapeDtypeStruct(x4.shape, jnp.float32),
        compiler_params=pltpu.CompilerParams(
            dimension_semantics=("arbitrary",),
            vmem_limit_bytes=_vmem_limit(40 * 1024 * 1024)),
        name="combine",
    )(lo_flat, x4, rel_t, m4, ye)


def _window_bias_t():
    j = jnp.arange(3 * WIN_SUB)[:, None]
    i = jnp.arange(WIN_SUB)[None, :]
    dist = jnp.abs(i + WIN_SUB - j).astype(jnp.float32)
    slopes = jnp.exp2(-8.0 * (jnp.arange(N_HEADS_A, dtype=jnp.float32) + 1.0) / N_HEADS_A)
    b = jnp.where(dist[None] <= WINDOW, -(slopes[:, None, None] * dist[None]), NEG)
    return jnp.transpose(b, (1, 0, 2)).reshape(3 * WIN_SUB, N_HEADS_A * WIN_SUB) * LOG2E


def _na_bias_t(rpb):
    kk = jnp.arange(NA_KEY_ROWS)[:, None]
    rho = jnp.arange(NA_GROUP_ROWS)[None, :]
    rel = kk - NA_GROUP_ROWS
    r0 = jnp.stack([
        jnp.zeros_like(rho),
        rho - NA_ROWS // 2,
        jnp.full_like(rho, NA_GROUP_ROWS - NA_ROWS),
    ])
    row_ok = (rel[None] >= r0) & (rel[None] < r0 + NA_ROWS)
    dr = jnp.clip(rel - rho + (NA_ROWS - 1), 0, 2 * NA_ROWS - 2)
    ck = jnp.arange(GRID_W)[:, None]
    cq = jnp.arange(GRID_W)[None, :]
    c0 = jnp.clip(cq - NA_COLS // 2, 0, GRID_W - NA_COLS)
    col_ok = (ck >= c0) & (ck < c0 + NA_COLS)
    dc = jnp.clip(ck - cq + (NA_COLS - 1), 0, 2 * NA_COLS - 2)
    hi = lax.Precision.HIGHEST
    oh_r = (dr[:, :, None] == jnp.arange(2 * NA_ROWS - 1)).astype(jnp.float32)
    oh_c = (dc[None] == jnp.arange(2 * NA_COLS - 1)[:, None, None]).astype(jnp.float32)
    rows = jnp.einsum('krs,hsd->hkrd', oh_r, rpb.astype(jnp.float32), precision=hi)
    vals = jnp.einsum('hkrd,dcq->hkcrq', rows, oh_c, precision=hi)
    ok = row_ok[:, :, None, :, None] & col_ok[None, None, :, None, :]
    b = jnp.where(ok[:, None], vals[None], NEG)
    h = rpb.shape[0]
    b = b.reshape(3, h // 2, 2, NA_KEY_ROWS * GRID_W, NA_TQ)
    b = jnp.transpose(b, (0, 1, 3, 2, 4)).reshape(3, h // 2, NA_KEY_ROWS * GRID_W, 2 * NA_TQ)
    return b * LOG2E


def _layer_params(p, l):
    scale = LOG2E / math.sqrt(HEAD_DIM)
    gains = jnp.concatenate([p["qnorm_a"][l] * scale, p["knorm_a"][l],
                             p["qnorm_b"][l] * scale, p["knorm_b"][l]])
    wr = p["w_router"][l].T
    wr_hi = wr.astype(jnp.bfloat16)
    return dict(
        g_mix=p["norm_mix"][l][None, :],
        w_in_t=p["w_in"][l].T.astype(jnp.bfloat16),
        head_gains=jnp.broadcast_to(gains[:, None], (4 * HEAD_DIM, PROJ_TM)),
        sink_row=jnp.repeat(p["sink_a"][l].astype(jnp.float32) * LOG2E, WIN_SUB)[None, :],
        na_bias=_na_bias_t(p["rpb_b"][l]),
        g_a=p["onorm_a"][l][None, :],
        g_b=p["onorm_b"][l][None, :],
        w_out=p["w_out"][l].astype(jnp.bfloat16),
        g_ffn=p["norm_ffn"][l][None, :],
        wr_hilo=jnp.concatenate(
            [wr_hi, (wr - wr_hi.astype(jnp.float32)).astype(jnp.bfloat16)], axis=0),
        layer=l,
        w_gate=p["w_gate"],
        w_up=p["w_up"],
        w_down=p["w_down"],
    )


def _trunk(x, layers, win_bias):
    b, s, d = x.shape
    n = b * s
    assert s % WIN_TQ == 0 and s // NA_TQ >= 3 and n % PROJ_TM == 0 and n % RT_T == 0
    assert (n // RT_T) % RT_TPS == 0 and (n // RT_T) % min(RANK_TILES, n // RT_T) == 0
    cap = EC_CAPACITY * n // N_EXPERTS
    tm = min(FFN_TM, cap)
    assert cap % tm == 0 and tm >= max(RT_CH, DISP_CH)
    cpad = cap + tm
    x2 = x.reshape(n, d)
    for q in layers:
        qa_t, ka, va_t, qb_t, kb, vb_t = _in_proj(x2, q["g_mix"], q["w_in_t"], q["head_gains"])
        out_a = _window_attention(qa_t, ka, va_t, win_bias, q["sink_row"], s)
        out_b = _na_attention(qb_t, kb, vb_t, q["na_bias"], s)
        x1, h, aff_t = _post_attn(out_a, out_b, x2, q["g_a"], q["g_b"], q["w_out"],
                                  q["g_ffn"], q["wr_hilo"])
        nt = n // RT_T
        m4 = _retile(_select(aff_t, cap)).reshape(RT_T // RT_G, nt, RT_G, LANE)
        rel_t, lo = _rank(m4)
        lo_flat = lo.reshape(-1)
        xe = _dispatch(_granule_view(h, nt), rel_t, lo_flat, cap, cpad)
        ye = _expert_ffn(xe, q["w_gate"], q["w_up"], q["w_down"], q["layer"], cap, cpad, tm)
        x2 = _combine(_granule_view(x1, nt), rel_t, m4, ye, lo_flat, cap).reshape(n, d)
    return x2.reshape(b, s, d)


def kernel(x_prompt, x_sample, norm_mix, w_in, qnorm_a, knorm_a, sink_a, qnorm_b, knorm_b,
           rpb_b, onorm_a, onorm_b, w_out, norm_ffn, w_router, w_gate, w_up, w_down):
    p = dict(norm_mix=norm_mix, w_in=w_in, qnorm_a=qnorm_a, knorm_a=knorm_a, sink_a=sink_a,
             qnorm_b=qnorm_b, knorm_b=knorm_b, rpb_b=rpb_b, onorm_a=onorm_a, onorm_b=onorm_b,
             w_out=w_out, norm_ffn=norm_ffn, w_router=w_router, w_gate=w_gate, w_up=w_up,
             w_down=w_down)
    layers = [_layer_params(p, l) for l in range(w_in.shape[0])]
    win_bias = _window_bias_t()
    return (_trunk(x_prompt, layers, win_bias), _trunk(x_sample, layers, win_bias))
```
